```python
import math
import jax
import jax.numpy as jnp
from jax import lax
import numpy as np

D_MODEL = 1024
BATCH = 4
SEQ = 8192
DEPTH = 2

CTX_LEN = 256
GRID_W = 64
EPS = 1e-6

D_RNN = 384
LRU_HEADS = 6
LRU_HEAD_DIM = D_RNN // LRU_HEADS
CONV_WIDTH = 4
LRU_C = 8.0

D_SSM = 384
SSM_GROUP = 16
SSM_GROUPS = D_SSM // SSM_GROUP
SSM_STATE = 64
DT_MIN = 1e-3
DT_MAX = 1e-1

D_POOL = 256
POOL_WINDOWS = (2, 4, 8, 16)
POOL_GROUP = D_POOL // len(POOL_WINDOWS)

COL_AX = 0
COL_BU = D_RNN
COL_AG = D_RNN + D_SSM
COL_CV = 2 * D_RNN + D_SSM
SCAN_COLS = D_RNN + D_SSM
D_IN = 2 * D_RNN + D_SSM + D_POOL
D_MIX = D_RNN + D_SSM + D_POOL
D_FF = -(-8 * D_MODEL // (3 * 256)) * 256

kernel_name = 'hymba_style_rglru_s5_pool_diffusion_block'


def _rmsnorm(x, g):
    x32 = x.astype(jnp.float32)
    y = x32 * lax.rsqrt(jnp.mean(x32 * x32, axis=-1, keepdims=True) + EPS)
    return (y * g.astype(jnp.float32)).astype(x.dtype)


def _modulate(h, shift, scale):
    return h * (1.0 + scale) + shift


def _swiglu(h, w_gate, w_up, w_down):
    return (jax.nn.silu(h @ w_gate) * (h @ w_up)) @ w_down


def _dwconv(v, w, b):
    t = v.shape[1]
    left = CONV_WIDTH // 2
    vp = jnp.pad(v, ((0, 0), (left, CONV_WIDTH - 1 - left), (0, 0)))
    out = b + vp[:, 0:t] * w[0]
    for k in range(1, CONV_WIDTH):
        out = out + vp[:, k:k + t] * w[k]
    return out


def _combine_real(e1, e2):
    a1, b1 = e1
    a2, b2 = e2
    return a1 * a2, a2 * b1 + b2


def _linear_scan(a, b, h0, reverse):
    if reverse:
        a, b = jnp.flip(a, 1), jnp.flip(b, 1)
    a_cum, h = lax.associative_scan(_combine_real, (a, b), axis=1)
    if h0 is not None:
        h = h + a_cum * h0[:, None]
    if reverse:
        h = jnp.flip(h, 1)
    return h


def _combine_complex(e1, e2):
    ar1, ai1, br1, bi1 = e1
    ar2, ai2, br2, bi2 = e2
    return (ar1 * ar2 - ai1 * ai2, ar1 * ai2 + ai1 * ar2,
            ar2 * br1 - ai2 * bi1 + br2, ar2 * bi1 + ai2 * br1 + bi2)


def _complex_scan(ar, ai, br, bi, h0, reverse):
    if reverse:
        ar, ai, br, bi = [jnp.flip(e, 1) for e in (ar, ai, br, bi)]
    acr, aci, hr, hi = lax.associative_scan(_combine_complex, (ar, ai, br, bi), axis=1)
    if h0 is not None:
        h0r, h0i = h0[0][:, None], h0[1][:, None]
        hr, hi = hr + acr * h0r - aci * h0i, hi + acr * h0i + aci * h0r
    if reverse:
        hr, hi = jnp.flip(hr, 1), jnp.flip(hi, 1)
    return hr, hi


def _rglru(xc, wa, ba, wi, bi, lam, h0, reverse):
    f32 = jnp.float32
    bn, t, _ = xc.shape
    xh = xc.reshape(bn, t, LRU_HEADS, LRU_HEAD_DIM)
    r = jax.nn.sigmoid(jnp.einsum('bthi,hij->bthj', xh, wa.astype(f32)).reshape(bn, t, D_RNN) + ba.astype(f32))
    i = jax.nn.sigmoid(jnp.einsum('bthi,hij->bthj', xh, wi.astype(f32)).reshape(bn, t, D_RNN) + bi.astype(f32))
    log_a = -LRU_C * r * jax.nn.softplus(-lam.astype(f32))
    a = jnp.exp(log_a)
    b = jnp.sqrt(-jnp.expm1(2.0 * log_a)) * (i * xc)
    return _linear_scan(a, b, h0, reverse)


def _s5_states(u, lam_re, lam_im, log_dt, b_re, b_im, h0, reverse):
    f32 = jnp.float32
    lr, li = lam_re.astype(f32), lam_im.astype(f32)
    dt = jnp.exp(log_dt.astype(f32))[:, None]
    mag = jnp.exp(lr * dt)
    ang = li * dt
    bar_r, bar_i = mag * jnp.cos(ang), mag * jnp.sin(ang)
    den = lr * lr + li * li
    fr = ((bar_r - 1.0) * lr + bar_i * li) / den
    fi = (bar_i * lr - (bar_r - 1.0) * li) / den
    br_, bi_ = b_re.astype(f32), b_im.astype(f32)
    bbr = fr[..., None] * br_ - fi[..., None] * bi_
    bbi = fr[..., None] * bi_ + fi[..., None] * br_
    xr = jnp.einsum('btgk,gpk->btgp', u, bbr)
    xi = jnp.einsum('btgk,gpk->btgp', u, bbi)
    shape = (1, u.shape[1]) + bar_r.shape
    return _complex_scan(jnp.broadcast_to(bar_r, shape), jnp.broadcast_to(bar_i, shape), xr, xi, h0, reverse)


def _pool_latent(v):
    bn, t, ch = v.shape
    rows = t // GRID_W
    grid = v.reshape(bn, rows, GRID_W, ch)
    sat = jnp.pad(jnp.cumsum(jnp.cumsum(grid, axis=1), axis=2), ((0, 0), (1, 0), (1, 0), (0, 0)))
    r = jnp.arange(rows)
    col = jnp.arange(GRID_W)
    outs = []
    for g, w in enumerate(POOL_WINDOWS):
        half = w // 2
        r0, r1 = jnp.clip(r - half, 0, rows), jnp.clip(r + half, 0, rows)
        c0, c1 = jnp.clip(col - half, 0, GRID_W), jnp.clip(col + half, 0, GRID_W)
        sg = sat[..., g * POOL_GROUP:(g + 1) * POOL_GROUP]
        band = sg[:, r1] - sg[:, r0]
        box = band[:, :, c1] - band[:, :, c0]
        cnt = ((r1 - r0)[:, None] * (c1 - c0)[None, :]).astype(jnp.float32)
        outs.append(box / cnt[None, :, :, None])
    return jnp.concatenate(outs, axis=-1).reshape(bn, t, ch)


def _pool_seq(v):
    bn, t, ch = v.shape
    cs = jnp.pad(jnp.cumsum(v, axis=1), ((0, 0), (1, 0), (0, 0)))
    pos = jnp.arange(t)
    outs = []
    for g, w in enumerate(POOL_WINDOWS):
        half = w // 2
        t0, t1 = jnp.clip(pos - half, 0, t), jnp.clip(pos + half, 0, t)
        sg = cs[..., g * POOL_GROUP:(g + 1) * POOL_GROUP]
        cnt = (t1 - t0).astype(jnp.float32)[None, :, None]
        outs.append((sg[:, t1] - sg[:, t0]) / cnt)
    return jnp.concatenate(outs, axis=-1)


def _token_mixers(proj, p, h0, pool_fn, need_output, need_state):
    f32 = jnp.float32
    bn, t = proj.shape[0], proj.shape[1]
    xc = _dwconv(proj[..., COL_AX:COL_BU], p['conv_w'], p['conv_b']).astype(f32)
    u = proj[..., COL_BU:COL_AG].astype(f32).reshape(bn, t, SSM_GROUPS, SSM_GROUP)
    rnn_h, ssm_y, lru_fin, s5_fin = [], [], [], []
    for d, rev in enumerate((False, True)):
        end = 0 if rev else t - 1
        h = _rglru(xc, p['lru_wa'][d], p['lru_ba'][d], p['lru_wi'][d], p['lru_bi'][d], p['lru_lambda'][d],
                   None if h0 is None else h0[0][d], rev)
        hr, hi = _s5_states(u, p['s5_lre'][d], p['s5_lim'][d], p['s5_log_dt'][d], p['s5_b_re'], p['s5_b_im'],
                            None if h0 is None else h0[1][d], rev)
        if need_state:
            lru_fin.append(h[:, end])
            s5_fin.append((hr[:, end], hi[:, end]))
        if need_output:
            rnn_h.append(h)
            ssm_y.append(jnp.einsum('btgp,gkp->btgk', hr, p['s5_c_re'][d].astype(f32))
                         - jnp.einsum('btgp,gkp->btgk', hi, p['s5_c_im'][d].astype(f32)))
    finals = (tuple(lru_fin), tuple(s5_fin)) if need_state else None
    if not need_output:
        return None, finals
    y_a = jax.nn.gelu(proj[..., COL_AG:COL_CV].astype(f32)) * (rnn_h[0] + rnn_h[1])
    y_s = (p['s5_d'].astype(f32).reshape(SSM_GROUPS, SSM_GROUP) * u + ssm_y[0] + ssm_y[1]).reshape(bn, t, D_SSM)
    z = jax.nn.gelu(y_s)
    y_b = z * jax.nn.sigmoid(z @ p['s5_glu_w'].astype(f32) + p['s5_glu_b'].astype(f32))
    v = proj[..., COL_CV:].astype(f32)
    m = (pool_fn(v) - v).reshape(bn, t, len(POOL_WINDOWS), POOL_GROUP)
    y_c = (jnp.einsum('btgi,gij->btgj', m, p['pool_w'].astype(f32)).reshape(bn, t, D_POOL)
           + p['pool_b'].astype(f32)) * p['pool_scale'].astype(f32)
    y = jnp.concatenate([y_a, y_b, y_c], axis=-1).astype(proj.dtype)
    return y, finals


def setup_inputs(seed: int = 0) -> dict:
    key = jax.random.key(seed)
    ks = iter(jax.random.split(key, 48))
    f32 = jnp.float32
    L, D = DEPTH, D_MODEL

    def nrm(shape, scale):
        return scale * jax.random.normal(next(ks), shape, f32)

    def unif(shape, lo, hi):
        return jax.random.uniform(next(ks), shape, f32, lo, hi)

    x = nrm((BATCH, SEQ, D), 1.0)
    c = nrm((BATCH, D), 1.0)
    ctx = nrm((BATCH, CTX_LEN, D), 1.0)
    c_ctx = nrm((D,), 1.0)
    w_mod = nrm((L, D, 6 * D), D ** -0.5)
    b_mod = nrm((L, 6 * D), 0.02)
    norm1_g = 1.0 + nrm((L, D), 0.02)
    norm2_g = 1.0 + nrm((L, D), 0.02)
    w_in = nrm((L, D, D_IN), D ** -0.5)
    w_out = nrm((L, D_MIX, D), D_MIX ** -0.5)
    lru_conv_w = nrm((L, CONV_WIDTH, D_RNN), CONV_WIDTH ** -0.5)
    lru_conv_b = nrm((L, D_RNN), 0.02)
    lru_wa = nrm((L, 2, LRU_HEADS, LRU_HEAD_DIM, LRU_HEAD_DIM), LRU_HEAD_DIM ** -0.5)
    lru_ba = nrm((L, 2, D_RNN), 0.02)
    lru_wi = nrm((L, 2, LRU_HEADS, LRU_HEAD_DIM, LRU_HEAD_DIM), LRU_HEAD_DIM ** -0.5)
    lru_bi = nrm((L, 2, D_RNN), 0.02)
    a_c = unif((L, 2, D_RNN), 0.9, 0.999)
    a_base = a_c ** (1.0 / LRU_C)
    lru_lambda = jnp.log(a_base) - jnp.log1p(-a_base)
    s5_lambda_re = -0.5 + nrm((L, 2, SSM_GROUPS, SSM_STATE), 0.01)
    s5_lambda_im = jnp.pi * jnp.arange(SSM_STATE, dtype=f32) + nrm((L, 2, SSM_GROUPS, SSM_STATE), 0.01)
    s5_log_dt = unif((L, 2, SSM_GROUPS), math.log(DT_MIN), math.log(DT_MAX))
    s5_b_re = nrm((L, SSM_GROUPS, SSM_STATE, SSM_GROUP), (2 * SSM_GROUP) ** -0.5)
    s5_b_im = nrm((L, SSM_GROUPS, SSM_STATE, SSM_GROUP), (2 * SSM_GROUP) ** -0.5)
    s5_c_re = nrm((L, 2, SSM_GROUPS, SSM_GROUP, SSM_STATE), (2 * SSM_STATE) ** -0.5)
    s5_c_im = nrm((L, 2, SSM_GROUPS, SSM_GROUP, SSM_STATE), (2 * SSM_STATE) ** -0.5)
    s5_d = nrm((L, D_SSM), 1.0)
    s5_glu_w = nrm((L, D_SSM, D_SSM), D_SSM ** -0.5)
    s5_glu_b = nrm((L, D_SSM), 0.02)
    pool_w = nrm((L, len(POOL_WINDOWS), POOL_GROUP, POOL_GROUP), POOL_GROUP ** -0.5)
    pool_b = nrm((L, D_POOL), 0.02)
    pool_scale = 1.0 + nrm((L, D_POOL), 0.02)
    ffn_w_gate = nrm((L, D, D_FF), D ** -0.5)
    ffn_w_up = nrm((L, D, D_FF), D ** -0.5)
    ffn_w_down = nrm((L, D_FF, D), D_FF ** -0.5)
    final_g = 1.0 + nrm((D,), 0.02)
    return {'x': x, 'c': c, 'ctx': ctx, 'c_ctx': c_ctx, 'w_mod': w_mod, 'b_mod': b_mod,
            'norm1_g': norm1_g, 'norm2_g': norm2_g, 'w_in': w_in, 'w_out': w_out,
            'lru_conv_w': lru_conv_w, 'lru_conv_b': lru_conv_b, 'lru_wa': lru_wa, 'lru_ba': lru_ba,
            'lru_wi': lru_wi, 'lru_bi': lru_bi, 'lru_lambda': lru_lambda,
            's5_lambda_re': s5_lambda_re, 's5_lambda_im': s5_lambda_im, 's5_log_dt': s5_log_dt,
            's5_b_re': s5_b_re, 's5_b_im': s5_b_im, 's5_c_re': s5_c_re, 's5_c_im': s5_c_im,
            's5_d': s5_d, 's5_glu_w': s5_glu_w, 's5_glu_b': s5_glu_b,
            'pool_w': pool_w, 'pool_b': pool_b, 'pool_scale': pool_scale,
            'ffn_w_gate': ffn_w_gate, 'ffn_w_up': ffn_w_up, 'ffn_w_down': ffn_w_down, 'final_g': final_g}


def reference(x, c, ctx, c_ctx, w_mod, b_mod, norm1_g, norm2_g, w_in, w_out,
              lru_conv_w, lru_conv_b, lru_wa, lru_ba, lru_wi, lru_bi, lru_lambda,
              s5_lambda_re, s5_lambda_im, s5_log_dt, s5_b_re, s5_b_im, s5_c_re, s5_c_im,
              s5_d, s5_glu_w, s5_glu_b, pool_w, pool_b, pool_scale,
              ffn_w_gate, ffn_w_up, ffn_w_down, final_g):
    D = D_MODEL
    silu_c = jax.nn.silu(c)
    silu_cc = jax.nn.silu(c_ctx)
    for l in range(DEPTH):
        last = l == DEPTH - 1
        p = {'conv_w': lru_conv_w[l], 'conv_b': lru_conv_b[l], 'lru_wa': lru_wa[l], 'lru_ba': lru_ba[l],
             'lru_wi': lru_wi[l], 'lru_bi': lru_bi[l], 'lru_lambda': lru_lambda[l],
             's5_lre': s5_lambda_re[l], 's5_lim': s5_lambda_im[l], 's5_log_dt': s5_log_dt[l],
             's5_b_re': s5_b_re[l], 's5_b_im': s5_b_im[l], 's5_c_re': s5_c_re[l], 's5_c_im': s5_c_im[l],
             's5_d': s5_d[l], 's5_glu_w': s5_glu_w[l], 's5_glu_b': s5_glu_b[l],
             'pool_w': pool_w[l], 'pool_b': pool_b[l], 'pool_scale': pool_scale[l]}
        n_mod = 2 if last else 6
        mod_c = silu_cc @ w_mod[l][:, :n_mod * D] + b_mod[l][:n_mod * D]
        mc = jnp.split(mod_c, n_mod)
        hc = _modulate(_rmsnorm(ctx, norm1_g[l]), mc[0], mc[1])
        pc = hc @ (w_in[l][:, :SCAN_COLS] if last else w_in[l])
        yc, ctx_states = _token_mixers(pc, p, None, _pool_seq, not last, True)
        mod = silu_c @ w_mod[l] + b_mod[l]
        sh1, sc1, g1, sh2, sc2, g2 = [m[:, None, :] for m in jnp.split(mod, 6, axis=-1)]
        hx = _modulate(_rmsnorm(x, norm1_g[l]), sh1, sc1)
        yx, _ = _token_mixers(hx @ w_in[l], p, ctx_states, _pool_latent, True, False)
        x = x + g1 * (yx @ w_out[l])
        x = x + g2 * _swiglu(_modulate(_rmsnorm(x, norm2_g[l]), sh2, sc2),
                             ffn_w_gate[l], ffn_w_up[l], ffn_w_down[l])
        if not last:
            ctx = ctx + mc[2] * (yc @ w_out[l])
            ctx = ctx + mc[5] * _swiglu(_modulate(_rmsnorm(ctx, norm2_g[l]), mc[3], mc[4]),
                                        ffn_w_gate[l], ffn_w_up[l], ffn_w_down[l])
    return _rmsnorm(x, final_g)
```

```python
import functools

import jax
import jax.numpy as jnp
from jax import lax
from jax.experimental import pallas as pl
from jax.experimental.pallas import tpu as pltpu

F32 = jnp.float32
BF16 = jnp.bfloat16

D_MODEL = 1024
BATCH = 4
SEQ = 8192
DEPTH = 2
CTX_LEN = 256
GRID_W = 64
GRID_H = SEQ // GRID_W
EPS = 1e-6

D_RNN = 384
LRU_HEADS = 6
LRU_HEAD_DIM = D_RNN // LRU_HEADS
CONV_WIDTH = 4

D_SSM = 384
SSM_GROUP = 16
SSM_GROUPS = D_SSM // SSM_GROUP
SSM_STATE = 64
N_STATE = SSM_GROUPS * SSM_STATE

D_POOL = 256
POOL_WINDOWS = (2, 4, 8, 16)
POOL_GROUP = D_POOL // len(POOL_WINDOWS)
POOL_PAD = 8

D_IN = 2 * D_RNN + D_SSM + D_POOL
D_MIX = D_RNN + D_SSM + D_POOL
D_FF = 2816

LANES = 128
SUBLANES = 8
N_TOK = CTX_LEN + SEQ
TILE = 256
N_TILES = N_TOK // TILE
N_BLK = TILE // SUBLANES
S5_BLOCKS = D_SSM // LANES
S5_BLK_STATES = N_STATE // S5_BLOCKS
SCAN_CHUNK = 256
PROW = GRID_W + 2 * POOL_PAD

VMEM_LIMIT = 48 * 1024 * 1024


def _const_spec(shape):
    nd = len(shape)
    return pl.BlockSpec(shape, lambda *_: (0,) * nd, pipeline_mode=pl.Buffered(1))


def _gelu(x):
    return jax.nn.gelu(x)


def _rms(x):
    return x * lax.rsqrt(jnp.mean(x * x, axis=-1, keepdims=True) + EPS)


def _mod_kernel(c_ref, w_ref, b_ref, o_ref):
    s = jax.nn.silu(c_ref[...])
    o_ref[...] = jnp.dot(s.astype(BF16), w_ref[...].astype(BF16),
                         preferred_element_type=F32) + b_ref[...]


def _modulation(cc, w_mod, b_mod):
    n_col = 6 * D_MODEL
    blk = 1536
    return pl.pallas_call(
        _mod_kernel,
        grid=(DEPTH, n_col // blk),
        in_specs=[pl.BlockSpec((SUBLANES, D_MODEL), lambda l, j: (0, 0)),
                  pl.BlockSpec((None, D_MODEL, blk), lambda l, j: (l, 0, j)),
                  pl.BlockSpec((None, 1, blk), lambda l, j: (l, 0, j))],
        out_specs=pl.BlockSpec((None, SUBLANES, blk), lambda l, j: (l, 0, j)),
        out_shape=jax.ShapeDtypeStruct((DEPTH, SUBLANES, n_col), F32),
        compiler_params=pltpu.CompilerParams(vmem_limit_bytes=VMEM_LIMIT),
        name="modulation",
    )(cc, w_mod, b_mod.reshape(DEPTH, 1, n_col))


def _inproj_kernel(x_ref, mod_ref, g_ref, w_ref, ax_ref, u_ref, gate_ref, v_ref):
    h = _rms(x_ref[...]) * g_ref[...]
    h = h * (1.0 + mod_ref[1:2, :]) + mod_ref[0:1, :]
    p = jnp.dot(h.astype(BF16), w_ref[...], preferred_element_type=F32)
    ax_ref[...] = p[:, 0:D_RNN]
    u_ref[...] = p[:, D_RNN:D_RNN + D_SSM]
    gate_ref[...] = p[:, D_RNN + D_SSM:2 * D_RNN + D_SSM]
    v_ref[...] = p[:, 2 * D_RNN + D_SSM:]


def _inproj(xt, modtab, g, w_in):
    tok = lambda w: pl.BlockSpec((None, TILE, w), lambda b, i: (b, i, 0))
    out = lambda w: jax.ShapeDtypeStruct((BATCH, N_TOK, w), F32)
    return pl.pallas_call(
        _inproj_kernel,
        grid=(BATCH, N_TILES),
        in_specs=[tok(D_MODEL),
                  pl.BlockSpec((None, None, 6, D_MODEL), lambda b, i: (b, jnp.minimum(i, 1), 0, 0)),
                  _const_spec((1, D_MODEL)),
                  _const_spec((D_MODEL, D_IN))],
        out_specs=[tok(D_RNN), tok(D_SSM), tok(D_RNN), tok(D_POOL)],
        out_shape=[out(D_RNN), out(D_SSM), out(D_RNN), out(D_POOL)],
        compiler_params=pltpu.CompilerParams(
            dimension_semantics=("parallel", "parallel"), vmem_limit_bytes=VMEM_LIMIT),
        name="inproj",
    )(xt, modtab, g, w_in)


def _scan_tile(i, rev):
    if rev:
        return jnp.where(i == 0, 0, N_TILES - i)
    return i


def _scan_kernel(rev, axp_ref, ax_ref, axn_ref, u_ref, cw_ref, cb_ref, wg_ref, bg_ref, spl_ref,
                 bmat_ref, cmat_ref, cst_ref, h_ref, ys_ref, axs, a_s, b_s, xs, lcar, scar):
    i = pl.program_id(1)
    tile = _scan_tile(i, rev)

    @pl.when(i == 0)
    def _():
        lcar[...] = jnp.zeros_like(lcar)
        scar[...] = jnp.zeros_like(scar)

    prev_ok = tile >= 2
    next_ok = jnp.logical_and(tile >= 1, tile <= N_TILES - 2)
    axs[0:SUBLANES, :] = jnp.where(prev_ok, axp_ref[...], 0.0)
    axs[SUBLANES:SUBLANES + TILE, :] = ax_ref[...]
    axs[SUBLANES + TILE:, :] = jnp.where(next_ok, axn_ref[...], 0.0)
    left = CONV_WIDTH // 2
    xc = cb_ref[...] + axs[SUBLANES - left:SUBLANES - left + TILE, :] * cw_ref[0:1, :]
    for k in range(1, CONV_WIDTH):
        o = SUBLANES - left + k
        xc = xc + axs[o:o + TILE, :] * cw_ref[k:k + 1, :]

    g = jnp.dot(xc.astype(BF16), wg_ref[...], preferred_element_type=F32) + bg_ref[...]
    r = jax.nn.sigmoid(g[:, :D_RNN])
    ig = jax.nn.sigmoid(g[:, D_RNN:])
    log_a = -(r * spl_ref[...])
    a = jnp.exp(log_a)
    a_s[...] = a
    b_s[...] = jnp.sqrt(1.0 - a * a) * (ig * xc)

    def blk_start(j):
        blk = (N_BLK - 1 - j) if rev else j
        return pl.multiple_of(blk * SUBLANES, SUBLANES)

    def shift(k):
        return (SUBLANES - k) if rev else k

    row = lax.broadcasted_iota(jnp.int32, (SUBLANES, 1), 0)
    steps = (1, 2, 4)
    valid = [(row < SUBLANES - k) if rev else (row >= k) for k in steps]
    last = 0 if rev else SUBLANES - 1

    def lru_body(j, carry):
        r0 = blk_start(j)
        a = a_s[pl.ds(r0, SUBLANES), :]
        b = b_s[pl.ds(r0, SUBLANES), :]
        for k, ok in zip(steps, valid):
            a_sh = pltpu.roll(a, shift(k), 0)
            b_sh = pltpu.roll(b, shift(k), 0)
            b = b + a * jnp.where(ok, b_sh, 0.0)
            a = a * jnp.where(ok, a_sh, 1.0)
        h = b + a * carry
        h_ref[pl.ds(r0, SUBLANES), :] = h
        return h[last:last + 1, :]

    lcar[0:1, :] = lax.fori_loop(0, N_BLK, lru_body, lcar[0:1, :])

    ub = u_ref[...].astype(BF16)
    for j in range(S5_BLOCKS):
        xj = jnp.dot(ub[:, j * LANES:(j + 1) * LANES], bmat_ref[j], preferred_element_type=F32)
        xs[:, j * S5_BLK_STATES:(j + 1) * S5_BLK_STATES] = xj[:, :S5_BLK_STATES]
        xs[:, N_STATE + j * S5_BLK_STATES:N_STATE + (j + 1) * S5_BLK_STATES] = xj[:, S5_BLK_STATES:]

    for c in range(N_STATE // SCAN_CHUNK):
        re = slice(c * SCAN_CHUNK, (c + 1) * SCAN_CHUNK)
        im = slice(N_STATE + c * SCAN_CHUNK, N_STATE + (c + 1) * SCAN_CHUNK)
        mults = [(cst_ref[2 * n, :, re], cst_ref[2 * n + 1, :, re]) for n in range(3)]
        pr, pi = cst_ref[6, :, re], cst_ref[7, :, re]

        def s5_body(j, carry, re=re, im=im, mults=mults, pr=pr, pi=pi):
            cr, ci = carry
            r0 = blk_start(j)
            hr = xs[pl.ds(r0, SUBLANES), re]
            hi = xs[pl.ds(r0, SUBLANES), im]
            for k, (mr, mi) in zip(steps, mults):
                sr = pltpu.roll(hr, shift(k), 0)
                si = pltpu.roll(hi, shift(k), 0)
                hr, hi = hr + (mr * sr - mi * si), hi + (mr * si + mi * sr)
            hr, hi = hr + (pr * cr - pi * ci), hi + (pr * ci + pi * cr)
            xs[pl.ds(r0, SUBLANES), re] = hr
            xs[pl.ds(r0, SUBLANES), im] = hi
            return hr[last:last + 1, :], hi[last:last + 1, :]

        cr, ci = lax.fori_loop(0, N_BLK, s5_body, (scar[0:1, re], scar[0:1, im]))
        scar[0:1, re] = cr
        scar[0:1, im] = ci

    for j in range(S5_BLOCKS):
        sre = xs[:, j * S5_BLK_STATES:(j + 1) * S5_BLK_STATES].astype(BF16)
        sim = xs[:, N_STATE + j * S5_BLK_STATES:N_STATE + (j + 1) * S5_BLK_STATES].astype(BF16)
        y = jnp.dot(sre, cmat_ref[j, 0:S5_BLK_STATES, :], preferred_element_type=F32)
        y = y + jnp.dot(sim, cmat_ref[j, S5_BLK_STATES:, :], preferred_element_type=F32)
        ys_ref[:, j * LANES:(j + 1) * LANES] = y


def _scan(rev, ax, u, cw, cb, wg, bg, spl, bmat, cmat, cst):
    tpb = TILE // SUBLANES
    nb8 = N_TOK // SUBLANES
    tile_of = functools.partial(_scan_tile, rev=rev)
    tok = lambda w: pl.BlockSpec((None, TILE, w), lambda b, i: (b, tile_of(i), 0))
    halo_prev = pl.BlockSpec((None, SUBLANES, D_RNN),
                             lambda b, i: (b, jnp.maximum(tile_of(i) * tpb - 1, 0), 0))
    halo_next = pl.BlockSpec((None, SUBLANES, D_RNN),
                             lambda b, i: (b, jnp.minimum((tile_of(i) + 1) * tpb, nb8 - 1), 0))
    out = jax.ShapeDtypeStruct((BATCH, N_TOK, D_RNN), F32)
    return pl.pallas_call(
        functools.partial(_scan_kernel, rev),
        grid=(BATCH, N_TILES),
        in_specs=[halo_prev, tok(D_RNN), halo_next, tok(D_SSM),
                  _const_spec((CONV_WIDTH, D_RNN)), _const_spec((1, D_RNN)),
                  _const_spec((D_RNN, 2 * D_RNN)), _const_spec((1, 2 * D_RNN)),
                  _const_spec((1, D_RNN)),
                  _const_spec((S5_BLOCKS, LANES, 2 * S5_BLK_STATES)),
                  _const_spec((S5_BLOCKS, 2 * S5_BLK_STATES, LANES)),
                  _const_spec((8, SUBLANES, N_STATE))],
        out_specs=[tok(D_RNN), tok(D_SSM)],
        out_shape=[out, out],
        scratch_shapes=[pltpu.VMEM((TILE + 2 * SUBLANES, D_RNN), F32),
                        pltpu.VMEM((TILE, D_RNN), F32),
                        pltpu.VMEM((TILE, D_RNN), F32),
                        pltpu.VMEM((TILE, 2 * N_STATE), F32),
                        pltpu.VMEM((SUBLANES, D_RNN), F32),
                        pltpu.VMEM((SUBLANES, 2 * N_STATE), F32)],
        compiler_params=pltpu.CompilerParams(
            dimension_semantics=("parallel", "arbitrary"), vmem_limit_bytes=VMEM_LIMIT),
        name="scan_bwd" if rev else "scan_fwd",
    )(ax, ax, ax, u, cw, cb, wg, bg, spl, bmat, cmat, cst)


def _window_sums(load, ha, hb):
    sa = load(-ha)
    for o in range(-ha + 1, ha):
        sa = sa + load(o)
    sb = sa
    for o in list(range(-hb, -ha)) + list(range(ha, hb)):
        sb = sb + load(o)
    return sa, sb


def _pool_kernel(wa, wb, v_ref, m_ref, vp, sp, cp):
    ha, hb = wa // 2, wb // 2
    is_a = lax.broadcasted_iota(jnp.int32, (1, LANES), 1) < POOL_GROUP

    def count(pos, half, n):
        return (jnp.minimum(pos + half, n) - jnp.maximum(pos - half, 0)).astype(F32)

    cp[...] = jnp.zeros_like(cp)
    cp[POOL_PAD:POOL_PAD + CTX_LEN, :] = v_ref[0:CTX_LEN, :]
    ca, cb = _window_sums(lambda o: cp[POOL_PAD + o:POOL_PAD + o + CTX_LEN, :], ha, hb)
    t = lax.broadcasted_iota(jnp.int32, (CTX_LEN, 1), 0)
    pooled = jnp.where(is_a, ca / count(t, ha, CTX_LEN), cb / count(t, hb, CTX_LEN))
    m_ref[0:CTX_LEN, :] = pooled - v_ref[0:CTX_LEN, :]

    vp[...] = jnp.zeros_like(vp)

    def fill(r, _):
        dst = pl.multiple_of((r + POOL_PAD) * PROW + POOL_PAD, SUBLANES)
        src = pl.multiple_of(CTX_LEN + r * GRID_W, SUBLANES)
        vp[pl.ds(dst, GRID_W), :] = v_ref[pl.ds(src, GRID_W), :]
        return 0

    lax.fori_loop(0, GRID_H, fill, 0)
    col = lax.broadcasted_iota(jnp.int32, (GRID_W, 1), 0)
    ccnt_a, ccnt_b = count(col, ha, GRID_W), count(col, hb, GRID_W)

    def body(r, _):
        base = pl.multiple_of((r + POOL_PAD) * PROW, SUBLANES)
        sa, sb = _window_sums(
            lambda o: vp[pl.ds(pl.multiple_of(base + o * PROW, SUBLANES), PROW), :], ha, hb)
        sp[...] = jnp.where(is_a, sa, sb)
        ba, bb = _window_sums(lambda o: sp[POOL_PAD + o:POOL_PAD + o + GRID_W, :], ha, hb)
        rcnt_a, rcnt_b = count(r, ha, GRID_H), count(r, hb, GRID_H)
        pooled = jnp.where(is_a, ba / (rcnt_a * ccnt_a), bb / (rcnt_b * ccnt_b))
        src = pl.multiple_of(CTX_LEN + r * GRID_W, SUBLANES)
        m_ref[pl.ds(src, GRID_W), :] = pooled - v_ref[pl.ds(src, GRID_W), :]
        return 0

    lax.fori_loop(0, GRID_H, body, 0)


def _pool(v, half):
    wa, wb = POOL_WINDOWS[2 * half], POOL_WINDOWS[2 * half + 1]
    return pl.pallas_call(
        functools.partial(_pool_kernel, wa, wb),
        grid=(BATCH,),
        in_specs=[pl.BlockSpec((None, N_TOK, LANES), lambda b: (b, 0, half))],
        out_specs=pl.BlockSpec((None, N_TOK, LANES), lambda b: (b, 0, 0)),
        out_shape=jax.ShapeDtypeStruct((BATCH, N_TOK, LANES), F32),
        scratch_shapes=[pltpu.VMEM(((GRID_H + 2 * POOL_PAD) * PROW, LANES), F32),
                        pltpu.VMEM((PROW, LANES), F32),
                        pltpu.VMEM((CTX_LEN + 2 * POOL_PAD, LANES), F32)],
        compiler_params=pltpu.CompilerParams(
            dimension_semantics=("parallel",), vmem_limit_bytes=VMEM_LIMIT),
        name=f"pool{half}",
    )(v)


def _mix_kernel(gate_ref, hf_ref, hb_ref, u_ref, ysf_ref, ysb_ref, m0_ref, m1_ref, x_ref, mod_ref,
                dvec_ref, wglu_ref, bglu_ref, pw_ref, pb_ref, ps_ref, wo_ref, o_ref):
    dot = functools.partial(jnp.dot, preferred_element_type=F32)
    y_a = _gelu(gate_ref[...]) * (hf_ref[...] + hb_ref[...])
    y_s = dvec_ref[...] * u_ref[...] + ysf_ref[...] + ysb_ref[...]
    z = _gelu(y_s)
    y_b = z * jax.nn.sigmoid(dot(z.astype(BF16), wglu_ref[...]) + bglu_ref[...])
    y_c = dot(m0_ref[...].astype(BF16), pw_ref[0:LANES, :])
    y_c = y_c + dot(m1_ref[...].astype(BF16), pw_ref[LANES:, :])
    y_c = (y_c + pb_ref[...]) * ps_ref[...]
    o = dot(y_a.astype(BF16), wo_ref[0:D_RNN, :])
    o = o + dot(y_b.astype(BF16), wo_ref[D_RNN:D_RNN + D_SSM, :])
    o = o + dot(y_c.astype(BF16), wo_ref[D_RNN + D_SSM:, :])
    o_ref[...] = x_ref[...] + mod_ref[2:3, :] * o


def _mix(skip_ctx, gate, hf, hb, u, ysf, ysb, m0, m1, xt, modtab,
         dvec, wglu, bglu, pw, pb, ps, wo):
    off = 1 if skip_ctx else 0
    n_tiles = N_TILES - off
    tok = lambda w: pl.BlockSpec((None, TILE, w), lambda b, i: (b, i + off, 0))
    return pl.pallas_call(
        _mix_kernel,
        grid=(BATCH, n_tiles),
        in_specs=[tok(D_RNN), tok(D_RNN), tok(D_RNN), tok(D_SSM), tok(D_SSM), tok(D_SSM),
                  tok(LANES), tok(LANES), tok(D_MODEL),
                  pl.BlockSpec((None, None, 6, D_MODEL),
                               lambda b, i: (b, jnp.minimum(i + off, 1), 0, 0)),
                  _const_spec((1, D_SSM)), _const_spec((D_SSM, D_SSM)), _const_spec((1, D_SSM)),
                  _const_spec((D_POOL, D_POOL)), _const_spec((1, D_POOL)), _const_spec((1, D_POOL)),
                  _const_spec((D_MIX, D_MODEL))],
        out_specs=pl.BlockSpec((None, TILE, D_MODEL), lambda b, i: (b, i, 0)),
        out_shape=jax.ShapeDtypeStruct((BATCH, n_tiles * TILE, D_MODEL), F32),
        compiler_params=pltpu.CompilerParams(
            dimension_semantics=("parallel", "parallel"), vmem_limit_bytes=VMEM_LIMIT),
        name="mix_out",
    )(gate, hf, hb, u, ysf, ysb, m0, m1, xt, modtab, dvec, wglu, bglu, pw, pb, ps, wo)


def _ffn_kernel(final, x_ref, mod_ref, g_ref, wgu_ref, wd_ref, fg_ref, o_ref):
    x = x_ref[...]
    h = _rms(x) * g_ref[...]
    h = (h * (1.0 + mod_ref[4:5, :]) + mod_ref[3:4, :]).astype(BF16)
    gu = jnp.dot(h, wgu_ref[...], preferred_element_type=F32)
    act = (jax.nn.silu(gu[:, :D_FF]) * gu[:, D_FF:]).astype(BF16)
    y = x + mod_ref[5:6, :] * jnp.dot(act, wd_ref[...], preferred_element_type=F32)
    if final:
        y = _rms(y) * fg_ref[...]
    o_ref[...] = y


def _ffn(final, has_ctx, x, modtab, g, wgu, wd, fg):
    n_tiles = x.shape[1] // TILE
    seg = (lambda i: jnp.minimum(i, 1)) if has_ctx else (lambda i: 1)
    tok = pl.BlockSpec((None, TILE, D_MODEL), lambda b, i: (b, i, 0))
    return pl.pallas_call(
        functools.partial(_ffn_kernel, final),
        grid=(BATCH, n_tiles),
        in_specs=[tok,
                  pl.BlockSpec((None, None, 6, D_MODEL), lambda b, i: (b, seg(i), 0, 0)),
                  _const_spec((1, D_MODEL)),
                  _const_spec((D_MODEL, 2 * D_FF)),
                  _const_spec((D_FF, D_MODEL)),
                  _const_spec((1, D_MODEL))],
        out_specs=tok,
        out_shape=jax.ShapeDtypeStruct(x.shape, F32),
        compiler_params=pltpu.CompilerParams(
            dimension_semantics=("parallel", "parallel"), vmem_limit_bytes=VMEM_LIMIT),
        name="ffn",
    )(x, modtab, g, wgu, wd, fg)


def _block_diag(w):
    n, di, dj = w.shape
    return jnp.einsum('nij,nm->nimj', w, jnp.eye(n, dtype=w.dtype)).reshape(n * di, n * dj)


def _s5_params(lam_re, lam_im, log_dt, b_re, b_im, c_re, c_im, rev):
    lr, li = lam_re.astype(F32), lam_im.astype(F32)
    dt = jnp.exp(log_dt.astype(F32))[:, None]
    mag = jnp.exp(lr * dt)
    ang = li * dt
    bar_r, bar_i = mag * jnp.cos(ang), mag * jnp.sin(ang)
    den = lr * lr + li * li
    fr = ((bar_r - 1.0) * lr + bar_i * li) / den
    fi = (bar_i * lr - (bar_r - 1.0) * li) / den
    cr, ci = c_re.astype(F32), c_im.astype(F32)
    cfr = cr * fr[:, None, :] - ci * fi[:, None, :]
    cfi = cr * fi[:, None, :] + ci * fr[:, None, :]
    gpb = SSM_GROUPS // S5_BLOCKS

    def out_blocks(w):
        w = jnp.swapaxes(w, 1, 2).reshape(S5_BLOCKS, gpb, SSM_STATE, SSM_GROUP)
        return jnp.stack([_block_diag(w[j]) for j in range(S5_BLOCKS)])

    cmat = jnp.concatenate([out_blocks(cfr), out_blocks(-cfi)], axis=1).astype(BF16)

    def in_blocks(w):
        w = jnp.swapaxes(w.astype(F32), 1, 2).reshape(S5_BLOCKS, gpb, SSM_GROUP, SSM_STATE)
        return jnp.stack([_block_diag(w[j]) for j in range(S5_BLOCKS)])

    bmat = jnp.concatenate([in_blocks(b_re), in_blocks(b_im)], axis=2).astype(BF16)

    pr, pi = [bar_r.reshape(-1)], [bar_i.reshape(-1)]
    for _ in range(SUBLANES - 1):
        pr, pi = (pr + [pr[-1] * pr[0] - pi[-1] * pi[0]], pi + [pr[-1] * pi[0] + pi[-1] * pr[0]])
    row = jnp.arange(SUBLANES)[:, None]
    cst = []
    for k in (1, 2, 4):
        ok = (row < SUBLANES - k) if rev else (row >= k)
        cst += [jnp.where(ok, pr[k - 1][None, :], 0.0), jnp.where(ok, pi[k - 1][None, :], 0.0)]
    order = jnp.arange(SUBLANES - 1, -1, -1) if rev else jnp.arange(SUBLANES)
    cst += [jnp.stack(pr)[order], jnp.stack(pi)[order]]
    return bmat, cmat, jnp.stack(cst).astype(F32)


def kernel(x, c, ctx, c_ctx, w_mod, b_mod, norm1_g, norm2_g, w_in, w_out, lru_conv_w, lru_conv_b,
           lru_wa, lru_ba, lru_wi, lru_bi, lru_lambda, s5_lambda_re, s5_lambda_im, s5_log_dt,
           s5_b_re, s5_b_im, s5_c_re, s5_c_im, s5_d, s5_glu_w, s5_glu_b, pool_w, pool_b, pool_scale,
           ffn_w_gate, ffn_w_up, ffn_w_down, final_g):
    cc = jnp.zeros((SUBLANES, D_MODEL), F32).at[:BATCH].set(c).at[BATCH].set(c_ctx)
    mod = _modulation(cc, w_mod, b_mod)
    xt = jnp.concatenate([ctx, x], axis=1)
    fg = final_g.reshape(1, D_MODEL)

    for l in range(DEPTH):
        last = l == DEPTH - 1
        lat = mod[l, :BATCH].reshape(BATCH, 6, D_MODEL)
        cm = jnp.broadcast_to(mod[l, BATCH].reshape(1, 6, D_MODEL), (BATCH, 6, D_MODEL))
        modtab = jnp.stack([cm, lat], axis=1)

        ax, u, gate, v = _inproj(xt, modtab, norm1_g[l].reshape(1, D_MODEL), w_in[l].astype(BF16))

        outs = []
        for d, rev in enumerate((False, True)):
            wg = jnp.concatenate([_block_diag(lru_wa[l, d]), _block_diag(lru_wi[l, d])], axis=1)
            bg = jnp.concatenate([lru_ba[l, d], lru_bi[l, d]]).reshape(1, 2 * D_RNN)
            spl = (8.0 * jax.nn.softplus(-lru_lambda[l, d].astype(F32))).reshape(1, D_RNN)
            bmat, cmat, cst = _s5_params(s5_lambda_re[l, d], s5_lambda_im[l, d], s5_log_dt[l, d],
                                         s5_b_re[l], s5_b_im[l], s5_c_re[l, d], s5_c_im[l, d], rev)
            outs.append(_scan(rev, ax, u, lru_conv_w[l], lru_conv_b[l].reshape(1, D_RNN),
                              wg.astype(BF16), bg, spl, bmat, cmat, cst))
        (hf, ysf), (hb, ysb) = outs

        m0, m1 = _pool(v, 0), _pool(v, 1)
        x1 = _mix(last, gate, hf, hb, u, ysf, ysb, m0, m1, xt, modtab,
                  s5_d[l].reshape(1, D_SSM), s5_glu_w[l].astype(BF16), s5_glu_b[l].reshape(1, D_SSM),
                  _block_diag(pool_w[l]).astype(BF16), pool_b[l].reshape(1, D_POOL),
                  pool_scale[l].reshape(1, D_POOL), w_out[l].astype(BF16))
        wgu = jnp.concatenate([ffn_w_gate[l], ffn_w_up[l]], axis=1).astype(BF16)
        xt = _ffn(last, not last, x1, modtab, norm2_g[l].reshape(1, D_MODEL), wgu,
                  ffn_w_down[l].astype(BF16), fg)
    return xt
```

```python
import functools

import jax
import jax.numpy as jnp
from jax import lax
from jax.experimental import pallas as pl
from jax.experimental.pallas import tpu as pltpu

F32 = jnp.float32
BF16 = jnp.bfloat16

D_MODEL = 1024
BATCH = 4
SEQ = 8192
DEPTH = 2
CTX_LEN = 256
GRID_W = 64
GRID_H = SEQ // GRID_W
EPS = 1e-6

D_RNN = 384
CONV_WIDTH = 4

D_SSM = 384
SSM_GROUP = 16
SSM_GROUPS = D_SSM // SSM_GROUP
SSM_STATE = 64
N_STATE = SSM_GROUPS * SSM_STATE

D_POOL = 256
POOL_WINDOWS = (2, 4, 8, 16)
POOL_GROUP = D_POOL // len(POOL_WINDOWS)
POOL_PAD = 8

D_IN = 2 * D_RNN + D_SSM + D_POOL
D_MIX = D_RNN + D_SSM + D_POOL
D_FF = 2816

LANES = 128
SUBLANES = 8
HALF = SUBLANES // 2
assert BATCH == HALF

N_TOK = CTX_LEN + SEQ
N_ROWS = N_TOK * BATCH
CTX_ROWS = CTX_LEN * BATCH
SEQ_ROWS = SEQ * BATCH

ROW_TILE = 1024
N_ROW_TILES = N_ROWS // ROW_TILE
assert CTX_ROWS == ROW_TILE

SCAN_TOK = 128
SCAN_ROWS = SCAN_TOK * BATCH
SCAN_TILES = N_TOK // SCAN_TOK
SCAN_CTX_TILES = CTX_LEN // SCAN_TOK
SCAN_BLKS = SCAN_ROWS // SUBLANES
S5_BLOCKS = D_SSM // LANES
S5_BLK_STATES = N_STATE // S5_BLOCKS
SCAN_CHUNK = 256

POOL_GROWS = 16
POOL_TILE = POOL_GROWS * GRID_W * BATCH
POOL_HALO = POOL_PAD * GRID_W * BATCH
GROW = GRID_W * BATCH
PROW = (GRID_W + 2 * POOL_PAD) * BATCH

VMEM_LIMIT = 56 * 1024 * 1024


def _const_spec(shape):
    nd = len(shape)
    return pl.BlockSpec(shape, lambda *_: (0,) * nd, pipeline_mode=pl.Buffered(1))


def _rms(x):
    return x * lax.rsqrt(jnp.mean(x * x, axis=-1, keepdims=True) + EPS)


def _rows(pat, n):
    return jnp.broadcast_to(pat[None], (n // SUBLANES,) + pat.shape).reshape(n, pat.shape[-1])


def _mod_spec(off=0):
    return pl.BlockSpec((None, 6, SUBLANES, D_MODEL), lambda i: (jnp.minimum(i + off, 1), 0, 0, 0))


def _mod_kernel(c_ref, w_ref, b_ref, o_ref):
    s = jax.nn.silu(c_ref[...])
    o_ref[...] = jnp.dot(s.astype(BF16), w_ref[...].astype(BF16),
                         preferred_element_type=F32) + b_ref[...]


def _modulation(cc, w_mod, b_mod):
    n_col = 6 * D_MODEL
    blk = 1536
    return pl.pallas_call(
        _mod_kernel,
        grid=(DEPTH, n_col // blk),
        in_specs=[pl.BlockSpec((SUBLANES, D_MODEL), lambda l, j: (0, 0)),
                  pl.BlockSpec((None, D_MODEL, blk), lambda l, j: (l, 0, j)),
                  pl.BlockSpec((None, 1, blk), lambda l, j: (l, 0, j))],
        out_specs=pl.BlockSpec((None, SUBLANES, blk), lambda l, j: (l, 0, j)),
        out_shape=jax.ShapeDtypeStruct((DEPTH, SUBLANES, n_col), F32),
        compiler_params=pltpu.CompilerParams(vmem_limit_bytes=VMEM_LIMIT),
        name="modulation",
    )(cc, w_mod, b_mod.reshape(DEPTH, 1, n_col))


def _inproj_kernel(x_ref, mod_ref, g_ref, w_ref, ax_ref, u_ref, gate_ref, v_ref, vc_ref):
    h = _rms(x_ref[...]) * g_ref[...]
    h = h * (1.0 + _rows(mod_ref[1], ROW_TILE)) + _rows(mod_ref[0], ROW_TILE)
    p = jnp.dot(h.astype(BF16), w_ref[...], preferred_element_type=F32)
    ax_ref[...] = p[:, 0:D_RNN]
    u_ref[...] = p[:, D_RNN:D_RNN + D_SSM]
    gate_ref[...] = p[:, D_RNN + D_SSM:2 * D_RNN + D_SSM]
    v = p[:, 2 * D_RNN + D_SSM:]
    v_ref[...] = v

    @pl.when(pl.program_id(0) == 0)
    def _():
        vc_ref[...] = v


def _inproj(xt, modtab, g, w_in):
    tok = lambda w: pl.BlockSpec((ROW_TILE, w), lambda i: (i, 0))
    out = lambda w: jax.ShapeDtypeStruct((N_ROWS, w), F32)
    return pl.pallas_call(
        _inproj_kernel,
        grid=(N_ROW_TILES,),
        in_specs=[tok(D_MODEL), _mod_spec(), _const_spec((1, D_MODEL)), _const_spec((D_MODEL, D_IN))],
        out_specs=[tok(D_RNN), tok(D_SSM), tok(D_RNN),
                   pl.BlockSpec((ROW_TILE, D_POOL), lambda i: (jnp.maximum(i - 1, 0), 0)),
                   pl.BlockSpec((CTX_ROWS, D_POOL), lambda i: (0, 0))],
        out_shape=[out(D_RNN), out(D_SSM), out(D_RNN),
                   jax.ShapeDtypeStruct((SEQ_ROWS, D_POOL), F32),
                   jax.ShapeDtypeStruct((CTX_ROWS, D_POOL), F32)],
        compiler_params=pltpu.CompilerParams(
            dimension_semantics=("arbitrary",), vmem_limit_bytes=VMEM_LIMIT),
        name="inproj",
    )(xt, modtab, g, w_in)


def _fwd_tile(i):
    return i


def _bwd_tile(i):
    return jnp.where(i < SCAN_CTX_TILES, SCAN_CTX_TILES - 1 - i, SCAN_TILES + SCAN_CTX_TILES - 1 - i)


def _scan_kernel(axp_f, ax_f, axn_f, u_f, axp_b, ax_b, axn_b, u_b,
                 cw_ref, cb_ref, wg_ref, bg_ref, spl_ref, bmat_ref, cmat_ref, lam_ref,
                 hf_ref, ysf_ref, hb_ref, ysb_ref,
                 axs, a_f, b_f, a_b, b_b, xs_f, xs_b, lcar, scar):
    i = pl.program_id(0)

    @pl.when(i == 0)
    def _():
        lcar[...] = jnp.zeros_like(lcar)
        scar[...] = jnp.zeros_like(scar)

    def gates(d, tile, axp, ax, axn, a_out, b_out):
        prev_ok = jnp.logical_and(tile != 0, tile != SCAN_CTX_TILES)
        next_ok = jnp.logical_and(tile != SCAN_CTX_TILES - 1, tile != SCAN_TILES - 1)
        axs[0:SUBLANES, :] = jnp.where(prev_ok, axp[...], 0.0)
        axs[SUBLANES:SUBLANES + SCAN_ROWS, :] = ax[...]
        axs[SUBLANES + SCAN_ROWS:, :] = jnp.where(next_ok, axn[...], 0.0)
        xc = cb_ref[...] + axs[0:SCAN_ROWS, :] * cw_ref[0:1, :]
        for k in range(1, CONV_WIDTH):
            xc = xc + axs[k * BATCH:k * BATCH + SCAN_ROWS, :] * cw_ref[k:k + 1, :]
        g = jnp.dot(xc.astype(BF16), wg_ref[d], preferred_element_type=F32) + bg_ref[d]
        r = jax.nn.sigmoid(g[:, :D_RNN])
        ig = jax.nn.sigmoid(g[:, D_RNN:])
        a = jnp.exp(-(r * spl_ref[d]))
        a_out[...] = a
        b_out[...] = jnp.sqrt(1.0 - a * a) * (ig * xc)

    gates(0, _fwd_tile(i), axp_f, ax_f, axn_f, a_f, b_f)
    gates(1, _bwd_tile(i), axp_b, ax_b, axn_b, a_b, b_b)

    lo = lax.broadcasted_iota(jnp.int32, (SUBLANES, 1), 0) < HALF

    def swap(x):
        return pltpu.roll(x, HALF, 0)

    def rows_of(j):
        return (pl.multiple_of(j * SUBLANES, SUBLANES),
                pl.multiple_of((SCAN_BLKS - 1 - j) * SUBLANES, SUBLANES))

    def steps(vf, vb):
        return jnp.where(lo, vf, vb), swap(jnp.where(lo, vb, vf))

    def unsteps(h1, h2):
        h2s = swap(h2)
        return jnp.where(lo, h1, h2s), jnp.where(lo, h2s, h1)

    def lru_body(j, h):
        rf, rb = rows_of(j)
        a1, a2 = steps(a_f[pl.ds(rf, SUBLANES), :], a_b[pl.ds(rb, SUBLANES), :])
        b1, b2 = steps(b_f[pl.ds(rf, SUBLANES), :], b_b[pl.ds(rb, SUBLANES), :])
        h1 = a1 * h + b1
        h2 = a2 * h1 + b2
        of, ob = unsteps(h1, h2)
        hf_ref[pl.ds(rf, SUBLANES), :] = of
        hb_ref[pl.ds(rb, SUBLANES), :] = ob
        return h2

    lcar[...] = lax.fori_loop(0, SCAN_BLKS, lru_body, lcar[...])

    uf = u_f[...].astype(BF16)
    ub = u_b[...].astype(BF16)
    nst = S5_BLK_STATES
    for jb in range(S5_BLOCKS):
        xs_f[...] = jnp.dot(uf[:, jb * LANES:(jb + 1) * LANES], bmat_ref[jb], preferred_element_type=F32)
        xs_b[...] = jnp.dot(ub[:, jb * LANES:(jb + 1) * LANES], bmat_ref[jb], preferred_element_type=F32)
        for c in range(nst // SCAN_CHUNK):
            re = slice(c * SCAN_CHUNK, (c + 1) * SCAN_CHUNK)
            im = slice(nst + c * SCAN_CHUNK, nst + (c + 1) * SCAN_CHUNK)
            gre = slice(jb * nst + c * SCAN_CHUNK, jb * nst + (c + 1) * SCAN_CHUNK)
            gim = slice(N_STATE + jb * nst + c * SCAN_CHUNK, N_STATE + jb * nst + (c + 1) * SCAN_CHUNK)
            lr, li = lam_ref[0, :, gre], lam_ref[1, :, gre]

            def s5_body(j, carry, re=re, im=im, lr=lr, li=li):
                sr, si = carry
                rf, rb = rows_of(j)
                xr1, xr2 = steps(xs_f[pl.ds(rf, SUBLANES), re], xs_b[pl.ds(rb, SUBLANES), re])
                xi1, xi2 = steps(xs_f[pl.ds(rf, SUBLANES), im], xs_b[pl.ds(rb, SUBLANES), im])
                r1 = lr * sr - li * si + xr1
                i1 = lr * si + li * sr + xi1
                r2 = lr * r1 - li * i1 + xr2
                i2 = lr * i1 + li * r1 + xi2
                of, ob = unsteps(r1, r2)
                xs_f[pl.ds(rf, SUBLANES), re] = of
                xs_b[pl.ds(rb, SUBLANES), re] = ob
                of, ob = unsteps(i1, i2)
                xs_f[pl.ds(rf, SUBLANES), im] = of
                xs_b[pl.ds(rb, SUBLANES), im] = ob
                return r2, i2

            sr, si = lax.fori_loop(0, SCAN_BLKS, s5_body, (scar[:, gre], scar[:, gim]))
            scar[:, gre] = sr
            scar[:, gim] = si

        for d, xs, ys_ref in ((0, xs_f, ysf_ref), (1, xs_b, ysb_ref)):
            y = jnp.dot(xs[:, :nst].astype(BF16), cmat_ref[d, jb, 0:nst, :], preferred_element_type=F32)
            y = y + jnp.dot(xs[:, nst:].astype(BF16), cmat_ref[d, jb, nst:, :], preferred_element_type=F32)
            ys_ref[:, jb * LANES:(jb + 1) * LANES] = y


def _scan(ax, u, cw, cb, wg, bg, spl, bmat, cmat, lam):
    tpb = SCAN_ROWS // SUBLANES
    nb8 = N_ROWS // SUBLANES

    def specs(tile_of):
        tok = lambda w: pl.BlockSpec((SCAN_ROWS, w), lambda i: (tile_of(i), 0))
        prev = pl.BlockSpec((SUBLANES, D_RNN), lambda i: (jnp.maximum(tile_of(i) * tpb - 1, 0), 0))
        nxt = pl.BlockSpec((SUBLANES, D_RNN), lambda i: (jnp.minimum((tile_of(i) + 1) * tpb, nb8 - 1), 0))
        return tok, prev, nxt

    tok_f, prev_f, next_f = specs(_fwd_tile)
    tok_b, prev_b, next_b = specs(_bwd_tile)
    out = jax.ShapeDtypeStruct((N_ROWS, D_RNN), F32)
    return pl.pallas_call(
        _scan_kernel,
        grid=(SCAN_TILES,),
        in_specs=[prev_f, tok_f(D_RNN), next_f, tok_f(D_SSM),
                  prev_b, tok_b(D_RNN), next_b, tok_b(D_SSM),
                  _const_spec((CONV_WIDTH, D_RNN)), _const_spec((1, D_RNN)),
                  _const_spec((2, D_RNN, 2 * D_RNN)), _const_spec((2, 1, 2 * D_RNN)),
                  _const_spec((2, 1, D_RNN)),
                  _const_spec((S5_BLOCKS, LANES, 2 * S5_BLK_STATES)),
                  _const_spec((2, S5_BLOCKS, 2 * S5_BLK_STATES, LANES)),
                  _const_spec((2, SUBLANES, N_STATE))],
        out_specs=[tok_f(D_RNN), tok_f(D_SSM), tok_b(D_RNN), tok_b(D_SSM)],
        out_shape=[out, out, out, out],
        scratch_shapes=[pltpu.VMEM((SCAN_ROWS + 2 * SUBLANES, D_RNN), F32),
                        pltpu.VMEM((SCAN_ROWS, D_RNN), F32), pltpu.VMEM((SCAN_ROWS, D_RNN), F32),
                        pltpu.VMEM((SCAN_ROWS, D_RNN), F32), pltpu.VMEM((SCAN_ROWS, D_RNN), F32),
                        pltpu.VMEM((SCAN_ROWS, 2 * S5_BLK_STATES), F32),
                        pltpu.VMEM((SCAN_ROWS, 2 * S5_BLK_STATES), F32),
                        pltpu.VMEM((SUBLANES, D_RNN), F32),
                        pltpu.VMEM((SUBLANES, 2 * N_STATE), F32)],
        compiler_params=pltpu.CompilerParams(
            dimension_semantics=("arbitrary",), vmem_limit_bytes=VMEM_LIMIT),
        name="scan",
    )(ax, ax, ax, u, ax, ax, ax, u, cw, cb, wg, bg, spl, bmat, cmat, lam)


def _window_sums(load, ha, hb):
    sa = load(-ha)
    for o in range(-ha + 1, ha):
        sa = sa + load(o)
    sb = sa
    for o in list(range(-hb, -ha)) + list(range(ha, hb)):
        sb = sb + load(o)
    return sa, sb


def _count(pos, half, n):
    return (jnp.minimum(pos + half, n) - jnp.maximum(pos - half, 0)).astype(F32)


def _pool_halves():
    is_a = lax.broadcasted_iota(jnp.int32, (1, LANES), 1) < POOL_GROUP
    for half in range(D_POOL // LANES):
        wa, wb = POOL_WINDOWS[2 * half], POOL_WINDOWS[2 * half + 1]
        yield slice(half * LANES, (half + 1) * LANES), wa // 2, wb // 2, is_a


def _pool_ctx_kernel(v_ref, m_ref, cp):
    pad = POOL_PAD * BATCH
    tok = lax.broadcasted_iota(jnp.int32, (CTX_ROWS, 1), 0) // BATCH
    for lanes, ha, hb, is_a in _pool_halves():
        cp[...] = jnp.zeros_like(cp)
        cp[pad:pad + CTX_ROWS, :] = v_ref[:, lanes]
        ca, cb = _window_sums(lambda o: cp[pad + o * BATCH:pad + o * BATCH + CTX_ROWS, :], ha, hb)
        pooled = jnp.where(is_a, ca / _count(tok, ha, CTX_LEN), cb / _count(tok, hb, CTX_LEN))
        m_ref[:, lanes] = pooled - v_ref[:, lanes]


def _pool_ctx(vc):
    return pl.pallas_call(
        _pool_ctx_kernel,
        out_shape=jax.ShapeDtypeStruct((CTX_ROWS, D_POOL), F32),
        scratch_shapes=[pltpu.VMEM((CTX_ROWS + 2 * POOL_PAD * BATCH, LANES), F32)],
        name="pool_ctx",
    )(vc)


def _pool_kernel(vprev_ref, v_ref, vnext_ref, m_ref, vp, sp):
    i = pl.program_id(0)
    g0 = i * POOL_GROWS
    pad = POOL_PAD * BATCH
    col = lax.broadcasted_iota(jnp.int32, (GROW, 1), 0) // BATCH
    have_prev = i > 0
    have_next = i < pl.num_programs(0) - 1
    for lanes, ha, hb, is_a in _pool_halves():
        vp[...] = jnp.zeros_like(vp)

        def fill(src_ref, first, keep, lanes=lanes):
            def body(q, _):
                dst = pl.multiple_of((first + q) * PROW + pad, SUBLANES)
                src = pl.multiple_of(q * GROW, SUBLANES)
                vp[pl.ds(dst, GROW), :] = jnp.where(keep, src_ref[pl.ds(src, GROW), lanes], 0.0)
                return 0
            return body

        lax.fori_loop(0, POOL_PAD, fill(vprev_ref, 0, have_prev), 0)
        lax.fori_loop(0, POOL_GROWS, fill(v_ref, POOL_PAD, True), 0)
        lax.fori_loop(0, POOL_PAD, fill(vnext_ref, POOL_PAD + POOL_GROWS, have_next), 0)
        ccnt_a, ccnt_b = _count(col, ha, GRID_W), _count(col, hb, GRID_W)

        def body(r, _, lanes=lanes, ha=ha, hb=hb, is_a=is_a, ccnt_a=ccnt_a, ccnt_b=ccnt_b):
            base = (r + POOL_PAD) * PROW
            sa, sb = _window_sums(
                lambda o: vp[pl.ds(pl.multiple_of(base + o * PROW, SUBLANES), PROW), :], ha, hb)
            sp[...] = jnp.where(is_a, sa, sb)
            ba, bb = _window_sums(lambda o: sp[pad + o * BATCH:pad + o * BATCH + GROW, :], ha, hb)
            rcnt_a, rcnt_b = _count(g0 + r, ha, GRID_H), _count(g0 + r, hb, GRID_H)
            pooled = jnp.where(is_a, ba / (rcnt_a * ccnt_a), bb / (rcnt_b * ccnt_b))
            src = pl.multiple_of(r * GROW, SUBLANES)
            m_ref[pl.ds(src, GROW), lanes] = pooled - v_ref[pl.ds(src, GROW), lanes]
            return 0

        lax.fori_loop(0, POOL_GROWS, body, 0)


def _pool(v):
    n = SEQ_ROWS // POOL_TILE
    per = POOL_TILE // POOL_HALO
    nh = SEQ_ROWS // POOL_HALO
    return pl.pallas_call(
        _pool_kernel,
        grid=(n,),
        in_specs=[pl.BlockSpec((POOL_HALO, D_POOL), lambda i: (jnp.maximum(i * per - 1, 0), 0)),
                  pl.BlockSpec((POOL_TILE, D_POOL), lambda i: (i, 0)),
                  pl.BlockSpec((POOL_HALO, D_POOL), lambda i: (jnp.minimum((i + 1) * per, nh - 1), 0))],
        out_specs=pl.BlockSpec((POOL_TILE, D_POOL), lambda i: (i, 0)),
        out_shape=jax.ShapeDtypeStruct((SEQ_ROWS, D_POOL), F32),
        scratch_shapes=[pltpu.VMEM(((POOL_GROWS + 2 * POOL_PAD) * PROW, LANES), F32),
                        pltpu.VMEM((PROW, LANES), F32)],
        compiler_params=pltpu.CompilerParams(
            dimension_semantics=("parallel",), vmem_limit_bytes=VMEM_LIMIT),
        name="pool",
    )(v, v, v)


def _mix_kernel(skip_ctx, gate_ref, hf_ref, hb_ref, u_ref, ysf_ref, ysb_ref, m_ref, mc_ref, x_ref,
                mod_ref, dvec_ref, wglu_ref, bglu_ref, pw_ref, pb_ref, ps_ref, wo_ref, o_ref):
    dot = functools.partial(jnp.dot, preferred_element_type=F32)
    y_a = jax.nn.gelu(gate_ref[...]) * (hf_ref[...] + hb_ref[...])
    y_s = dvec_ref[...] * u_ref[...] + ysf_ref[...] + ysb_ref[...]
    z = jax.nn.gelu(y_s)
    y_b = z * jax.nn.sigmoid(dot(z.astype(BF16), wglu_ref[...]) + bglu_ref[...])
    m = m_ref[...] if skip_ctx else jnp.where(pl.program_id(0) == 0, mc_ref[...], m_ref[...])
    y_c = (dot(m.astype(BF16), pw_ref[...]) + pb_ref[...]) * ps_ref[...]
    o = dot(y_a.astype(BF16), wo_ref[0:D_RNN, :])
    o = o + dot(y_b.astype(BF16), wo_ref[D_RNN:D_RNN + D_SSM, :])
    o = o + dot(y_c.astype(BF16), wo_ref[D_RNN + D_SSM:, :])
    o_ref[...] = x_ref[...] + _rows(mod_ref[2], ROW_TILE) * o


def _mix(skip_ctx, gate, hf, hb, u, ysf, ysb, m, mc, xt, modtab, dvec, wglu, bglu, pw, pb, ps, wo):
    off = 1 if skip_ctx else 0
    n_tiles = N_ROW_TILES - off
    tok = lambda w: pl.BlockSpec((ROW_TILE, w), lambda i: (i + off, 0))
    return pl.pallas_call(
        functools.partial(_mix_kernel, skip_ctx),
        grid=(n_tiles,),
        in_specs=[tok(D_RNN), tok(D_RNN), tok(D_RNN), tok(D_SSM), tok(D_SSM), tok(D_SSM),
                  pl.BlockSpec((ROW_TILE, D_POOL), lambda i: (jnp.maximum(i + off - 1, 0), 0)),
                  pl.BlockSpec((CTX_ROWS, D_POOL), lambda i: (0, 0)),
                  tok(D_MODEL), _mod_spec(off),
                  _const_spec((1, D_SSM)), _const_spec((D_SSM, D_SSM)), _const_spec((1, D_SSM)),
                  _const_spec((D_POOL, D_POOL)), _const_spec((1, D_POOL)), _const_spec((1, D_POOL)),
                  _const_spec((D_MIX, D_MODEL))],
        out_specs=pl.BlockSpec((ROW_TILE, D_MODEL), lambda i: (i, 0)),
        out_shape=jax.ShapeDtypeStruct((n_tiles * ROW_TILE, D_MODEL), F32),
        compiler_params=pltpu.CompilerParams(
            dimension_semantics=("parallel",), vmem_limit_bytes=VMEM_LIMIT),
        name="mix_out",
    )(gate, hf, hb, u, ysf, ysb, m, mc, xt, modtab, dvec, wglu, bglu, pw, pb, ps, wo)


def _ffn_kernel(final, x_ref, mod_ref, g_ref, wgu_ref, wd_ref, fg_ref, o_ref):
    x = x_ref[...]
    h = _rms(x) * g_ref[...]
    h = (h * (1.0 + _rows(mod_ref[4], ROW_TILE)) + _rows(mod_ref[3], ROW_TILE)).astype(BF16)
    gu = jnp.dot(h, wgu_ref[...], preferred_element_type=F32)
    act = (jax.nn.silu(gu[:, :D_FF]) * gu[:, D_FF:]).astype(BF16)
    y = x + _rows(mod_ref[5], ROW_TILE) * jnp.dot(act, wd_ref[...], preferred_element_type=F32)
    if final:
        y = _rms(y) * fg_ref[...]
    o_ref[...] = y


def _ffn(final, has_ctx, x, modtab, g, wgu, wd, fg):
    n_tiles = x.shape[0] // ROW_TILE
    tok = pl.BlockSpec((ROW_TILE, D_MODEL), lambda i: (i, 0))
    return pl.pallas_call(
        functools.partial(_ffn_kernel, final),
        grid=(n_tiles,),
        in_specs=[tok, _mod_spec(0 if has_ctx else 1), _const_spec((1, D_MODEL)),
                  _const_spec((D_MODEL, 2 * D_FF)), _const_spec((D_FF, D_MODEL)),
                  _const_spec((1, D_MODEL))],
        out_specs=tok,
        out_shape=jax.ShapeDtypeStruct(x.shape, F32),
        compiler_params=pltpu.CompilerParams(
            dimension_semantics=("parallel",), vmem_limit_bytes=VMEM_LIMIT),
        name="ffn",
    )(x, modtab, g, wgu, wd, fg)


def _block_diag(w):
    n, di, dj = w.shape
    return jnp.einsum('nij,nm->nimj', w, jnp.eye(n, dtype=w.dtype)).reshape(n * di, n * dj)


def _s5_params(lam_re, lam_im, log_dt, b_re, b_im, c_re, c_im):
    gpb = SSM_GROUPS // S5_BLOCKS

    def in_blocks(w):
        w = jnp.swapaxes(w.astype(F32), 1, 2).reshape(S5_BLOCKS, gpb, SSM_GROUP, SSM_STATE)
        return jnp.stack([_block_diag(w[j]) for j in range(S5_BLOCKS)])

    def out_blocks(w):
        w = jnp.swapaxes(w, 1, 2).reshape(S5_BLOCKS, gpb, SSM_STATE, SSM_GROUP)
        return jnp.stack([_block_diag(w[j]) for j in range(S5_BLOCKS)])

    bmat = jnp.concatenate([in_blocks(b_re), in_blocks(b_im)], axis=2).astype(BF16)
    cmats, lams = [], []
    for d in range(2):
        lr, li = lam_re[d].astype(F32), lam_im[d].astype(F32)
        dt = jnp.exp(log_dt[d].astype(F32))[:, None]
        mag = jnp.exp(lr * dt)
        ang = li * dt
        bar_r, bar_i = mag * jnp.cos(ang), mag * jnp.sin(ang)
        den = lr * lr + li * li
        fr = ((bar_r - 1.0) * lr + bar_i * li) / den
        fi = (bar_i * lr - (bar_r - 1.0) * li) / den
        cr, ci = c_re[d].astype(F32), c_im[d].astype(F32)
        cfr = cr * fr[:, None, :] - ci * fi[:, None, :]
        cfi = cr * fi[:, None, :] + ci * fr[:, None, :]
        cmats.append(jnp.concatenate([out_blocks(cfr), out_blocks(-cfi)], axis=1))
        lams.append(jnp.stack([bar_r.reshape(-1), bar_i.reshape(-1)]))
    cmat = jnp.stack(cmats).astype(BF16)
    lam = jnp.concatenate([jnp.broadcast_to(lams[d][:, None, :], (2, HALF, N_STATE)) for d in range(2)],
                          axis=1)
    return bmat, cmat, lam


def kernel(x, c, ctx, c_ctx, w_mod, b_mod, norm1_g, norm2_g, w_in, w_out, lru_conv_w, lru_conv_b,
           lru_wa, lru_ba, lru_wi, lru_bi, lru_lambda, s5_lambda_re, s5_lambda_im, s5_log_dt,
           s5_b_re, s5_b_im, s5_c_re, s5_c_im, s5_d, s5_glu_w, s5_glu_b, pool_w, pool_b, pool_scale,
           ffn_w_gate, ffn_w_up, ffn_w_down, final_g):
    cc = jnp.zeros((SUBLANES, D_MODEL), F32).at[:BATCH].set(c).at[BATCH].set(c_ctx)
    mod = _modulation(cc, w_mod, b_mod)
    xt = jnp.transpose(jnp.concatenate([ctx, x], axis=1), (1, 0, 2)).reshape(N_ROWS, D_MODEL)
    fg = final_g.reshape(1, D_MODEL)

    for l in range(DEPTH):
        last = l == DEPTH - 1
        lat = jnp.transpose(mod[l, :BATCH].reshape(BATCH, 6, D_MODEL), (1, 0, 2))
        lat = jnp.concatenate([lat, lat], axis=1)
        cm = jnp.broadcast_to(mod[l, BATCH].reshape(6, 1, D_MODEL), (6, SUBLANES, D_MODEL))
        modtab = jnp.stack([cm, lat])

        ax, u, gate, v, vc = _inproj(xt, modtab, norm1_g[l].reshape(1, D_MODEL), w_in[l].astype(BF16))

        wg = jnp.stack([jnp.concatenate([_block_diag(lru_wa[l, d]), _block_diag(lru_wi[l, d])], axis=1)
                        for d in range(2)]).astype(BF16)
        bg = jnp.concatenate([lru_ba[l], lru_bi[l]], axis=-1).reshape(2, 1, 2 * D_RNN)
        spl = (8.0 * jax.nn.softplus(-lru_lambda[l].astype(F32))).reshape(2, 1, D_RNN)
        bmat, cmat, lam = _s5_params(s5_lambda_re[l], s5_lambda_im[l], s5_log_dt[l],
                                     s5_b_re[l], s5_b_im[l], s5_c_re[l], s5_c_im[l])
        hf, ysf, hb, ysb = _scan(ax, u, lru_conv_w[l], lru_conv_b[l].reshape(1, D_RNN),
                                 wg, bg, spl, bmat, cmat, lam)

        m = _pool(v)
        mc = vc if last else _pool_ctx(vc)
        x1 = _mix(last, gate, hf, hb, u, ysf, ysb, m, mc, xt, modtab,
                  s5_d[l].reshape(1, D_SSM), s5_glu_w[l].astype(BF16), s5_glu_b[l].reshape(1, D_SSM),
                  _block_diag(pool_w[l]).astype(BF16), pool_b[l].reshape(1, D_POOL),
                  pool_scale[l].reshape(1, D_POOL), w_out[l].astype(BF16))
        wgu = jnp.concatenate([ffn_w_gate[l], ffn_w_up[l]], axis=1).astype(BF16)
        xt = _ffn(last, not last, x1, modtab, norm2_g[l].reshape(1, D_MODEL), wgu,
                  ffn_w_down[l].astype(BF16), fg)
    return jnp.transpose(xt.reshape(SEQ, BATCH, D_MODEL), (1, 0, 2))
```

```python
import functools

import jax
import jax.numpy as jnp
from jax import lax
from jax.experimental import pallas as pl
from jax.experimental.pallas import tpu as pltpu

F32 = jnp.float32
BF16 = jnp.bfloat16

D_MODEL = 1024
BATCH = 4
SEQ = 8192
DEPTH = 2
CTX_LEN = 256
GRID_W = 64
GRID_H = SEQ // GRID_W
EPS = 1e-6

D_RNN = 384
CONV_WIDTH = 4

D_SSM = 384
SSM_GROUP = 16
SSM_GROUPS = D_SSM // SSM_GROUP
SSM_STATE = 64
N_STATE = SSM_GROUPS * SSM_STATE

D_POOL = 256
POOL_WINDOWS = (2, 4, 8, 16)
POOL_GROUP = D_POOL // len(POOL_WINDOWS)
POOL_PAD = 8

D_IN = 2 * D_RNN + D_SSM + D_POOL
D_MIX = D_RNN + D_SSM + D_POOL
D_FF = 2816

LANES = 128
SUBLANES = 8
HALF = SUBLANES // 2
assert BATCH == HALF

N_TOK = CTX_LEN + SEQ
N_ROWS = N_TOK * BATCH
CTX_ROWS = CTX_LEN * BATCH
SEQ_ROWS = SEQ * BATCH

ROW_TILE = 1024
N_ROW_TILES = N_ROWS // ROW_TILE
assert CTX_ROWS == ROW_TILE

SCAN_TOK = 128
SCAN_ROWS = SCAN_TOK * BATCH
SCAN_TILES = N_TOK // SCAN_TOK
SCAN_CTX_TILES = CTX_LEN // SCAN_TOK
SCAN_BLKS = SCAN_ROWS // SUBLANES
S5_BLOCKS = D_SSM // LANES
S5_BLK_STATES = N_STATE // S5_BLOCKS
SCAN_CHUNK = 512

POOL_GROWS = 16
POOL_TILE = POOL_GROWS * GRID_W * BATCH
POOL_HALO = POOL_PAD * GRID_W * BATCH
GROW = GRID_W * BATCH
PROW = (GRID_W + 2 * POOL_PAD) * BATCH

VMEM_LIMIT = 56 * 1024 * 1024


def _const_spec(shape):
    nd = len(shape)
    return pl.BlockSpec(shape, lambda *_: (0,) * nd, pipeline_mode=pl.Buffered(1))


def _rms(x):
    return x * lax.rsqrt(jnp.mean(x * x, axis=-1, keepdims=True) + EPS)


def _rows(pat, n):
    return jnp.broadcast_to(pat[None], (n // SUBLANES,) + pat.shape).reshape(n, pat.shape[-1])


def _mod_spec(off=0):
    return pl.BlockSpec((None, 6, SUBLANES, D_MODEL), lambda i: (jnp.minimum(i + off, 1), 0, 0, 0))


def _mod_kernel(c_ref, w_ref, b_ref, o_ref):
    s = jax.nn.silu(c_ref[...])
    o_ref[...] = jnp.dot(s.astype(BF16), w_ref[...].astype(BF16),
                         preferred_element_type=F32) + b_ref[...]


def _modulation(cc, w_mod, b_mod):
    n_col = 6 * D_MODEL
    blk = 1536
    return pl.pallas_call(
        _mod_kernel,
        grid=(DEPTH, n_col // blk),
        in_specs=[pl.BlockSpec((SUBLANES, D_MODEL), lambda l, j: (0, 0)),
                  pl.BlockSpec((None, D_MODEL, blk), lambda l, j: (l, 0, j)),
                  pl.BlockSpec((None, 1, blk), lambda l, j: (l, 0, j))],
        out_specs=pl.BlockSpec((None, SUBLANES, blk), lambda l, j: (l, 0, j)),
        out_shape=jax.ShapeDtypeStruct((DEPTH, SUBLANES, n_col), F32),
        compiler_params=pltpu.CompilerParams(vmem_limit_bytes=VMEM_LIMIT),
        name="modulation",
    )(cc, w_mod, b_mod.reshape(DEPTH, 1, n_col))


def _interleave(src_ref, slab):
    for b in range(BATCH):
        for s in range(D_MODEL // LANES):
            slab[s, pl.ds(b, ROW_TILE // BATCH, stride=BATCH), :] = src_ref[b, :, s * LANES:(s + 1) * LANES]


def _inproj_body(x, mod_ref, g_ref, w_ref, ax_ref, u_ref, gate_ref, v_ref, vc_ref):
    h = _rms(x) * g_ref[...]
    h = h * (1.0 + _rows(mod_ref[1], ROW_TILE)) + _rows(mod_ref[0], ROW_TILE)
    p = jnp.dot(h.astype(BF16), w_ref[...], preferred_element_type=F32)
    ax_ref[...] = p[:, 0:D_RNN]
    u_ref[...] = p[:, D_RNN:D_RNN + D_SSM]
    gate_ref[...] = p[:, D_RNN + D_SSM:2 * D_RNN + D_SSM]
    v = p[:, 2 * D_RNN + D_SSM:]
    v_ref[...] = v

    @pl.when(pl.program_id(0) == 0)
    def _():
        vc_ref[...] = v


def _inproj_kernel(x_ref, *rest):
    _inproj_body(x_ref[...], *rest)


def _inproj_first_kernel(ctx_ref, x_ref, mod_ref, g_ref, w_ref, xt_ref, *rest):
    slab = rest[-1]
    i = pl.program_id(0)

    @pl.when(i == 0)
    def _():
        _interleave(ctx_ref, slab)

    @pl.when(i > 0)
    def _():
        _interleave(x_ref, slab)

    x = jnp.concatenate([slab[s] for s in range(D_MODEL // LANES)], axis=1)
    xt_ref[...] = x
    _inproj_body(x, mod_ref, g_ref, w_ref, *rest[:-1])


def _inproj(xt, modtab, g, w_in):
    first = isinstance(xt, tuple)
    tok = lambda w: pl.BlockSpec((ROW_TILE, w), lambda i: (i, 0))
    out = lambda w: jax.ShapeDtypeStruct((N_ROWS, w), F32)
    out_specs = [tok(D_RNN), tok(D_SSM), tok(D_RNN),
                 pl.BlockSpec((ROW_TILE, D_POOL), lambda i: (jnp.maximum(i - 1, 0), 0)),
                 pl.BlockSpec((CTX_ROWS, D_POOL), lambda i: (0, 0))]
    out_shape = [out(D_RNN), out(D_SSM), out(D_RNN),
                 jax.ShapeDtypeStruct((SEQ_ROWS, D_POOL), F32),
                 jax.ShapeDtypeStruct((CTX_ROWS, D_POOL), F32)]
    common = [_mod_spec(), _const_spec((1, D_MODEL)), _const_spec((D_MODEL, D_IN))]
    params = pltpu.CompilerParams(dimension_semantics=("arbitrary",), vmem_limit_bytes=VMEM_LIMIT)
    if not first:
        return (xt,) + tuple(pl.pallas_call(
            _inproj_kernel, grid=(N_ROW_TILES,), in_specs=[tok(D_MODEL)] + common,
            out_specs=out_specs, out_shape=out_shape, compiler_params=params, name="inproj",
        )(xt, modtab, g, w_in))
    toks = ROW_TILE // BATCH
    src = [pl.BlockSpec((BATCH, CTX_LEN, D_MODEL), lambda i: (0, 0, 0), pipeline_mode=pl.Buffered(1)),
           pl.BlockSpec((BATCH, toks, D_MODEL), lambda i: (0, jnp.maximum(i - 1, 0), 0))]
    return tuple(pl.pallas_call(
        _inproj_first_kernel, grid=(N_ROW_TILES,), in_specs=src + common,
        out_specs=[tok(D_MODEL)] + out_specs, out_shape=[out(D_MODEL)] + out_shape,
        scratch_shapes=[pltpu.VMEM((D_MODEL // LANES, ROW_TILE, LANES), F32)],
        compiler_params=params, name="inproj_first",
    )(xt[0], xt[1], modtab, g, w_in))


def _fwd_tile(i):
    return i


def _bwd_tile(i):
    return jnp.where(i < SCAN_CTX_TILES, SCAN_CTX_TILES - 1 - i, SCAN_TILES + SCAN_CTX_TILES - 1 - i)


def _scan_kernel(axp_f, ax_f, axn_f, u_f, axp_b, ax_b, axn_b, u_b,
                 cw_ref, cb_ref, wg_ref, bg_ref, spl_ref, bmat_ref, cmat_ref, lam_ref,
                 hf_ref, ysf_ref, hb_ref, ysb_ref,
                 axs, a_f, b_f, a_b, b_b, us1, us2, xs1, xs2, ys1, ys2, lcar, scar):
    i = pl.program_id(0)

    @pl.when(i == 0)
    def _():
        lcar[...] = jnp.zeros_like(lcar)
        scar[...] = jnp.zeros_like(scar)

    def gates(d, tile, axp, ax, axn, a_out, b_out):
        prev_ok = jnp.logical_and(tile != 0, tile != SCAN_CTX_TILES)
        next_ok = jnp.logical_and(tile != SCAN_CTX_TILES - 1, tile != SCAN_TILES - 1)
        axs[0:SUBLANES, :] = jnp.where(prev_ok, axp[...], 0.0)
        axs[SUBLANES:SUBLANES + SCAN_ROWS, :] = ax[...]
        axs[SUBLANES + SCAN_ROWS:, :] = jnp.where(next_ok, axn[...], 0.0)
        xc = cb_ref[...] + axs[0:SCAN_ROWS, :] * cw_ref[0:1, :]
        for k in range(1, CONV_WIDTH):
            xc = xc + axs[k * BATCH:k * BATCH + SCAN_ROWS, :] * cw_ref[k:k + 1, :]
        g = jnp.dot(xc.astype(BF16), wg_ref[d], preferred_element_type=F32) + bg_ref[d]
        r = jax.nn.sigmoid(g[:, :D_RNN])
        ig = jax.nn.sigmoid(g[:, D_RNN:])
        a = jnp.exp(-(r * spl_ref[d]))
        a_out[...] = a
        b_out[...] = jnp.sqrt(1.0 - a * a) * (ig * xc)

    gates(0, _fwd_tile(i), axp_f, ax_f, axn_f, a_f, b_f)
    gates(1, _bwd_tile(i), axp_b, ax_b, axn_b, a_b, b_b)

    lo = lax.broadcasted_iota(jnp.int32, (SUBLANES, 1), 0) < HALF

    def swap(x):
        return pltpu.roll(x, HALF, 0)

    def rows_of(j):
        return (pl.multiple_of(j * SUBLANES, SUBLANES),
                pl.multiple_of((SCAN_BLKS - 1 - j) * SUBLANES, SUBLANES))

    def steps(vf, vb):
        return jnp.where(lo, vf, vb), swap(jnp.where(lo, vb, vf))

    def unsteps(h1, h2):
        h2s = swap(h2)
        return jnp.where(lo, h1, h2s), jnp.where(lo, h2s, h1)

    def lru_body(j, h):
        rf, rb = rows_of(j)
        a1, a2 = steps(a_f[pl.ds(rf, SUBLANES), :], a_b[pl.ds(rb, SUBLANES), :])
        b1, b2 = steps(b_f[pl.ds(rf, SUBLANES), :], b_b[pl.ds(rb, SUBLANES), :])
        h1 = a1 * h + b1
        h2 = a2 * h1 + b2
        of, ob = unsteps(h1, h2)
        hf_ref[pl.ds(rf, SUBLANES), :] = of
        hb_ref[pl.ds(rb, SUBLANES), :] = ob
        return h2

    lcar[...] = lax.fori_loop(0, SCAN_BLKS, lru_body, lcar[...], unroll=True)

    def to_steps(j, _):
        rf, rb = rows_of(j)
        s1, s2 = steps(u_f[pl.ds(rf, SUBLANES), :], u_b[pl.ds(rb, SUBLANES), :])
        us1[pl.ds(rf, SUBLANES), :] = s1
        us2[pl.ds(rf, SUBLANES), :] = s2
        return 0

    lax.fori_loop(0, SCAN_BLKS, to_steps, 0, unroll=True)

    nst = S5_BLK_STATES
    for jb in range(S5_BLOCKS):
        blk = slice(jb * LANES, (jb + 1) * LANES)
        x1, x2 = xs1.at[jb], xs2.at[jb]
        x1[...] = jnp.dot(us1[:, blk].astype(BF16), bmat_ref[jb], preferred_element_type=F32)
        x2[...] = jnp.dot(us2[:, blk].astype(BF16), bmat_ref[jb], preferred_element_type=F32)
        for c in range(nst // SCAN_CHUNK):
            re = slice(c * SCAN_CHUNK, (c + 1) * SCAN_CHUNK)
            im = slice(nst + c * SCAN_CHUNK, nst + (c + 1) * SCAN_CHUNK)
            gre = slice(jb * nst + c * SCAN_CHUNK, jb * nst + (c + 1) * SCAN_CHUNK)
            gim = slice(N_STATE + jb * nst + c * SCAN_CHUNK, N_STATE + jb * nst + (c + 1) * SCAN_CHUNK)
            lr, li = lam_ref[0, :, gre], lam_ref[1, :, gre]

            def s5_body(j, carry, re=re, im=im, lr=lr, li=li, x1=x1, x2=x2):
                sr, si = carry
                rows = pl.ds(pl.multiple_of(j * SUBLANES, SUBLANES), SUBLANES)
                r1 = lr * sr - li * si + x1[rows, re]
                i1 = lr * si + li * sr + x1[rows, im]
                r2 = lr * r1 - li * i1 + x2[rows, re]
                i2 = lr * i1 + li * r1 + x2[rows, im]
                x1[rows, re] = r1
                x1[rows, im] = i1
                x2[rows, re] = r2
                x2[rows, im] = i2
                return r2, i2

            sr, si = lax.fori_loop(0, SCAN_BLKS, s5_body, (scar[:, gre], scar[:, gim]), unroll=True)
            scar[:, gre] = sr
            scar[:, gim] = si

        for xs, ys in ((x1, ys1), (x2, ys2)):
            y = jnp.dot(xs[:, :nst].astype(BF16), cmat_ref[jb, 0:nst, :], preferred_element_type=F32)
            y = y + jnp.dot(xs[:, nst:].astype(BF16), cmat_ref[jb, nst:, :], preferred_element_type=F32)
            ys[:, 2 * jb * LANES:2 * (jb + 1) * LANES] = y

    def from_steps(j, _):
        rf, rb = rows_of(j)
        of, ob = unsteps(ys1[pl.ds(rf, SUBLANES), :], ys2[pl.ds(rf, SUBLANES), :])
        for jb in range(S5_BLOCKS):
            ysf_ref[pl.ds(rf, SUBLANES), jb * LANES:(jb + 1) * LANES] = of[:, 2 * jb * LANES:(2 * jb + 1) * LANES]
            ysb_ref[pl.ds(rb, SUBLANES), jb * LANES:(jb + 1) * LANES] = ob[:, (2 * jb + 1) * LANES:2 * (jb + 1) * LANES]
        return 0

    lax.fori_loop(0, SCAN_BLKS, from_steps, 0, unroll=True)


def _scan(ax, u, cw, cb, wg, bg, spl, bmat, cmat, lam):
    tpb = SCAN_ROWS // SUBLANES
    nb8 = N_ROWS // SUBLANES

    def specs(tile_of):
        tok = lambda w: pl.BlockSpec((SCAN_ROWS, w), lambda i: (tile_of(i), 0))
        prev = pl.BlockSpec((SUBLANES, D_RNN), lambda i: (jnp.maximum(tile_of(i) * tpb - 1, 0), 0))
        nxt = pl.BlockSpec((SUBLANES, D_RNN), lambda i: (jnp.minimum((tile_of(i) + 1) * tpb, nb8 - 1), 0))
        return tok, prev, nxt

    tok_f, prev_f, next_f = specs(_fwd_tile)
    tok_b, prev_b, next_b = specs(_bwd_tile)
    out = jax.ShapeDtypeStruct((N_ROWS, D_RNN), F32)
    return pl.pallas_call(
        _scan_kernel,
        grid=(SCAN_TILES,),
        in_specs=[prev_f, tok_f(D_RNN), next_f, tok_f(D_SSM),
                  prev_b, tok_b(D_RNN), next_b, tok_b(D_SSM),
                  _const_spec((CONV_WIDTH, D_RNN)), _const_spec((1, D_RNN)),
                  _const_spec((2, D_RNN, 2 * D_RNN)), _const_spec((2, 1, 2 * D_RNN)),
                  _const_spec((2, 1, D_RNN)),
                  _const_spec((S5_BLOCKS, LANES, 2 * S5_BLK_STATES)),
                  _const_spec((S5_BLOCKS, 2 * S5_BLK_STATES, 2 * LANES)),
                  _const_spec((2, SUBLANES, N_STATE))],
        out_specs=[tok_f(D_RNN), tok_f(D_SSM), tok_b(D_RNN), tok_b(D_SSM)],
        out_shape=[out, out, out, out],
        scratch_shapes=[pltpu.VMEM((SCAN_ROWS + 2 * SUBLANES, D_RNN), F32),
                        pltpu.VMEM((SCAN_ROWS, D_RNN), F32), pltpu.VMEM((SCAN_ROWS, D_RNN), F32),
                        pltpu.VMEM((SCAN_ROWS, D_RNN), F32), pltpu.VMEM((SCAN_ROWS, D_RNN), F32),
                        pltpu.VMEM((SCAN_ROWS, D_SSM), F32), pltpu.VMEM((SCAN_ROWS, D_SSM), F32),
                        pltpu.VMEM((S5_BLOCKS, SCAN_ROWS, 2 * S5_BLK_STATES), F32),
                        pltpu.VMEM((S5_BLOCKS, SCAN_ROWS, 2 * S5_BLK_STATES), F32),
                        pltpu.VMEM((SCAN_ROWS, 2 * D_SSM), F32), pltpu.VMEM((SCAN_ROWS, 2 * D_SSM), F32),
                        pltpu.VMEM((SUBLANES, D_RNN), F32),
                        pltpu.VMEM((SUBLANES, 2 * N_STATE), F32)],
        compiler_params=pltpu.CompilerParams(
            dimension_semantics=("arbitrary",), vmem_limit_bytes=VMEM_LIMIT),
        name="scan",
    )(ax, ax, ax, u, ax, ax, ax, u, cw, cb, wg, bg, spl, bmat, cmat, lam)


def _window_sums(load, ha, hb):
    sa = load(-ha)
    for o in range(-ha + 1, ha):
        sa = sa + load(o)
    sb = sa
    for o in list(range(-hb, -ha)) + list(range(ha, hb)):
        sb = sb + load(o)
    return sa, sb


def _count(pos, half, n):
    return (jnp.minimum(pos + half, n) - jnp.maximum(pos - half, 0)).astype(F32)


def _pool_halves():
    is_a = lax.broadcasted_iota(jnp.int32, (1, LANES), 1) < POOL_GROUP
    for half in range(D_POOL // LANES):
        wa, wb = POOL_WINDOWS[2 * half], POOL_WINDOWS[2 * half + 1]
        yield slice(half * LANES, (half + 1) * LANES), wa // 2, wb // 2, is_a


def _pool_ctx_kernel(v_ref, m_ref, cp):
    pad = POOL_PAD * BATCH
    tok = lax.broadcasted_iota(jnp.int32, (CTX_ROWS, 1), 0) // BATCH
    for lanes, ha, hb, is_a in _pool_halves():
        cp[...] = jnp.zeros_like(cp)
        cp[pad:pad + CTX_ROWS, :] = v_ref[:, lanes]
        ca, cb = _window_sums(lambda o: cp[pad + o * BATCH:pad + o * BATCH + CTX_ROWS, :], ha, hb)
        pooled = jnp.where(is_a, ca / _count(tok, ha, CTX_LEN), cb / _count(tok, hb, CTX_LEN))
        m_ref[:, lanes] = pooled - v_ref[:, lanes]


def _pool_ctx(vc):
    return pl.pallas_call(
        _pool_ctx_kernel,
        out_shape=jax.ShapeDtypeStruct((CTX_ROWS, D_POOL), F32),
        scratch_shapes=[pltpu.VMEM((CTX_ROWS + 2 * POOL_PAD * BATCH, LANES), F32)],
        name="pool_ctx",
    )(vc)


def _pool_kernel(vprev_ref, v_ref, vnext_ref, m_ref, vp, sp):
    i = pl.program_id(0)
    g0 = i * POOL_GROWS
    pad = POOL_PAD * BATCH
    col = lax.broadcasted_iota(jnp.int32, (GROW, 1), 0) // BATCH
    have_prev = i > 0
    have_next = i < pl.num_programs(0) - 1
    for lanes, ha, hb, is_a in _pool_halves():
        vp[...] = jnp.zeros_like(vp)

        def fill(src_ref, first, keep, lanes=lanes):
            def body(q, _):
                dst = pl.multiple_of((first + q) * PROW + pad, SUBLANES)
                src = pl.multiple_of(q * GROW, SUBLANES)
                vp[pl.ds(dst, GROW), :] = jnp.where(keep, src_ref[pl.ds(src, GROW), lanes], 0.0)
                return 0
            return body

        lax.fori_loop(0, POOL_PAD, fill(vprev_ref, 0, have_prev), 0)
        lax.fori_loop(0, POOL_GROWS, fill(v_ref, POOL_PAD, True), 0)
        lax.fori_loop(0, POOL_PAD, fill(vnext_ref, POOL_PAD + POOL_GROWS, have_next), 0)
        ccnt_a, ccnt_b = _count(col, ha, GRID_W), _count(col, hb, GRID_W)

        def body(r, _, lanes=lanes, ha=ha, hb=hb, is_a=is_a, ccnt_a=ccnt_a, ccnt_b=ccnt_b):
            base = (r + POOL_PAD) * PROW
            sa, sb = _window_sums(
                lambda o: vp[pl.ds(pl.multiple_of(base + o * PROW, SUBLANES), PROW), :], ha, hb)
            sp[...] = jnp.where(is_a, sa, sb)
            ba, bb = _window_sums(lambda o: sp[pad + o * BATCH:pad + o * BATCH + GROW, :], ha, hb)
            rcnt_a, rcnt_b = _count(g0 + r, ha, GRID_H), _count(g0 + r, hb, GRID_H)
            pooled = jnp.where(is_a, ba / (rcnt_a * ccnt_a), bb / (rcnt_b * ccnt_b))
            src = pl.multiple_of(r * GROW, SUBLANES)
            m_ref[pl.ds(src, GROW), lanes] = pooled - v_ref[pl.ds(src, GROW), lanes]
            return 0

        lax.fori_loop(0, POOL_GROWS, body, 0)


def _pool(v):
    n = SEQ_ROWS // POOL_TILE
    per = POOL_TILE // POOL_HALO
    nh = SEQ_ROWS // POOL_HALO
    return pl.pallas_call(
        _pool_kernel,
        grid=(n,),
        in_specs=[pl.BlockSpec((POOL_HALO, D_POOL), lambda i: (jnp.maximum(i * per - 1, 0), 0)),
                  pl.BlockSpec((POOL_TILE, D_POOL), lambda i: (i, 0)),
                  pl.BlockSpec((POOL_HALO, D_POOL), lambda i: (jnp.minimum((i + 1) * per, nh - 1), 0))],
        out_specs=pl.BlockSpec((POOL_TILE, D_POOL), lambda i: (i, 0)),
        out_shape=jax.ShapeDtypeStruct((SEQ_ROWS, D_POOL), F32),
        scratch_shapes=[pltpu.VMEM(((POOL_GROWS + 2 * POOL_PAD) * PROW, LANES), F32),
                        pltpu.VMEM((PROW, LANES), F32)],
        compiler_params=pltpu.CompilerParams(
            dimension_semantics=("parallel",), vmem_limit_bytes=VMEM_LIMIT),
        name="pool",
    )(v, v, v)


def _mix_kernel(skip_ctx, gate_ref, hf_ref, hb_ref, u_ref, ysf_ref, ysb_ref, m_ref, mc_ref, x_ref,
                mod_ref, dvec_ref, wglu_ref, bglu_ref, pw_ref, pb_ref, ps_ref, wo_ref, o_ref):
    dot = functools.partial(jnp.dot, preferred_element_type=F32)
    y_a = jax.nn.gelu(gate_ref[...]) * (hf_ref[...] + hb_ref[...])
    y_s = dvec_ref[...] * u_ref[...] + ysf_ref[...] + ysb_ref[...]
    z = jax.nn.gelu(y_s)
    y_b = z * jax.nn.sigmoid(dot(z.astype(BF16), wglu_ref[...]) + bglu_ref[...])
    m = m_ref[...] if skip_ctx else jnp.where(pl.program_id(0) == 0, mc_ref[...], m_ref[...])
    y_c = (dot(m.astype(BF16), pw_ref[...]) + pb_ref[...]) * ps_ref[...]
    o = dot(y_a.astype(BF16), wo_ref[0:D_RNN, :])
    o = o + dot(y_b.astype(BF16), wo_ref[D_RNN:D_RNN + D_SSM, :])
    o = o + dot(y_c.astype(BF16), wo_ref[D_RNN + D_SSM:, :])
    o_ref[...] = x_ref[...] + _rows(mod_ref[2], ROW_TILE) * o


def _mix(skip_ctx, gate, hf, hb, u, ysf, ysb, m, mc, xt, modtab, dvec, wglu, bglu, pw, pb, ps, wo):
    off = 1 if skip_ctx else 0
    n_tiles = N_ROW_TILES - off
    tok = lambda w: pl.BlockSpec((ROW_TILE, w), lambda i: (i + off, 0))
    return pl.pallas_call(
        functools.partial(_mix_kernel, skip_ctx),
        grid=(n_tiles,),
        in_specs=[tok(D_RNN), tok(D_RNN), tok(D_RNN), tok(D_SSM), tok(D_SSM), tok(D_SSM),
                  pl.BlockSpec((ROW_TILE, D_POOL), lambda i: (jnp.maximum(i + off - 1, 0), 0)),
                  pl.BlockSpec((CTX_ROWS, D_POOL), lambda i: (0, 0)),
                  tok(D_MODEL), _mod_spec(off),
                  _const_spec((1, D_SSM)), _const_spec((D_SSM, D_SSM)), _const_spec((1, D_SSM)),
                  _const_spec((D_POOL, D_POOL)), _const_spec((1, D_POOL)), _const_spec((1, D_POOL)),
                  _const_spec((D_MIX, D_MODEL))],
        out_specs=pl.BlockSpec((ROW_TILE, D_MODEL), lambda i: (i, 0)),
        out_shape=jax.ShapeDtypeStruct((n_tiles * ROW_TILE, D_MODEL), F32),
        compiler_params=pltpu.CompilerParams(
            dimension_semantics=("parallel",), vmem_limit_bytes=VMEM_LIMIT),
        name="mix_out",
    )(gate, hf, hb, u, ysf, ysb, m, mc, xt, modtab, dvec, wglu, bglu, pw, pb, ps, wo)


def _ffn_kernel(final, x_ref, mod_ref, g_ref, wgu_ref, wd_ref, fg_ref, o_ref, *scratch):
    x = x_ref[...]
    h = _rms(x) * g_ref[...]
    h = (h * (1.0 + _rows(mod_ref[4], ROW_TILE)) + _rows(mod_ref[3], ROW_TILE)).astype(BF16)
    gu = jnp.dot(h, wgu_ref[...], preferred_element_type=F32)
    act = (jax.nn.silu(gu[:, :D_FF]) * gu[:, D_FF:]).astype(BF16)
    y = x + _rows(mod_ref[5], ROW_TILE) * jnp.dot(act, wd_ref[...], preferred_element_type=F32)
    if not final:
        o_ref[...] = y
        return
    y = _rms(y) * fg_ref[...]
    slab, = scratch
    for s in range(D_MODEL // LANES):
        slab[s] = y[:, s * LANES:(s + 1) * LANES]
    for b in range(BATCH):
        for s in range(D_MODEL // LANES):
            o_ref[b, :, s * LANES:(s + 1) * LANES] = slab[s, pl.ds(b, ROW_TILE // BATCH, stride=BATCH), :]


def _ffn(final, has_ctx, x, modtab, g, wgu, wd, fg):
    n_tiles = x.shape[0] // ROW_TILE
    tok = pl.BlockSpec((ROW_TILE, D_MODEL), lambda i: (i, 0))
    if final:
        toks = ROW_TILE // BATCH
        out_spec = pl.BlockSpec((BATCH, toks, D_MODEL), lambda i: (0, i, 0))
        out_shape = jax.ShapeDtypeStruct((BATCH, n_tiles * toks, D_MODEL), F32)
        scratch = [pltpu.VMEM((D_MODEL // LANES, ROW_TILE, LANES), F32)]
    else:
        out_spec, out_shape, scratch = tok, jax.ShapeDtypeStruct(x.shape, F32), []
    return pl.pallas_call(
        functools.partial(_ffn_kernel, final),
        grid=(n_tiles,),
        in_specs=[tok, _mod_spec(0 if has_ctx else 1), _const_spec((1, D_MODEL)),
                  _const_spec((D_MODEL, 2 * D_FF)), _const_spec((D_FF, D_MODEL)),
                  _const_spec((1, D_MODEL))],
        out_specs=out_spec,
        out_shape=out_shape,
        scratch_shapes=scratch,
        compiler_params=pltpu.CompilerParams(
            dimension_semantics=("parallel",), vmem_limit_bytes=VMEM_LIMIT),
        name="ffn_final" if final else "ffn",
    )(x, modtab, g, wgu, wd, fg)


def _block_diag(w):
    n, di, dj = w.shape
    return jnp.einsum('nij,nm->nimj', w, jnp.eye(n, dtype=w.dtype)).reshape(n * di, n * dj)


def _s5_params(lam_re, lam_im, log_dt, b_re, b_im, c_re, c_im):
    gpb = SSM_GROUPS // S5_BLOCKS

    def in_blocks(w):
        w = jnp.swapaxes(w.astype(F32), 1, 2).reshape(S5_BLOCKS, gpb, SSM_GROUP, SSM_STATE)
        return jnp.stack([_block_diag(w[j]) for j in range(S5_BLOCKS)])

    def out_blocks(w):
        w = jnp.swapaxes(w, 1, 2).reshape(S5_BLOCKS, gpb, SSM_STATE, SSM_GROUP)
        return jnp.stack([_block_diag(w[j]) for j in range(S5_BLOCKS)])

    bmat = jnp.concatenate([in_blocks(b_re), in_blocks(b_im)], axis=2).astype(BF16)
    cmats, lams = [], []
    for d in range(2):
        lr, li = lam_re[d].astype(F32), lam_im[d].astype(F32)
        dt = jnp.exp(log_dt[d].astype(F32))[:, None]
        mag = jnp.exp(lr * dt)
        ang = li * dt
        bar_r, bar_i = mag * jnp.cos(ang), mag * jnp.sin(ang)
        den = lr * lr + li * li
        fr = ((bar_r - 1.0) * lr + bar_i * li) / den
        fi = (bar_i * lr - (bar_r - 1.0) * li) / den
        cr, ci = c_re[d].astype(F32), c_im[d].astype(F32)
        cfr = cr * fr[:, None, :] - ci * fi[:, None, :]
        cfi = cr * fi[:, None, :] + ci * fr[:, None, :]
        cmats.append(jnp.concatenate([out_blocks(cfr), out_blocks(-cfi)], axis=1))
        lams.append(jnp.stack([bar_r.reshape(-1), bar_i.reshape(-1)]))
    cmat = jnp.concatenate(cmats, axis=-1).astype(BF16)
    lam = jnp.concatenate([jnp.broadcast_to(lams[d][:, None, :], (2, HALF, N_STATE)) for d in range(2)],
                          axis=1)
    return bmat, cmat, lam


def kernel(x, c, ctx, c_ctx, w_mod, b_mod, norm1_g, norm2_g, w_in, w_out, lru_conv_w, lru_conv_b,
           lru_wa, lru_ba, lru_wi, lru_bi, lru_lambda, s5_lambda_re, s5_lambda_im, s5_log_dt,
           s5_b_re, s5_b_im, s5_c_re, s5_c_im, s5_d, s5_glu_w, s5_glu_b, pool_w, pool_b, pool_scale,
           ffn_w_gate, ffn_w_up, ffn_w_down, final_g):
    cc = jnp.zeros((SUBLANES, D_MODEL), F32).at[:BATCH].set(c).at[BATCH].set(c_ctx)
    mod = _modulation(cc, w_mod, b_mod)
    xt = (ctx, x)
    fg = final_g.reshape(1, D_MODEL)

    for l in range(DEPTH):
        last = l == DEPTH - 1
        lat = jnp.transpose(mod[l, :BATCH].reshape(BATCH, 6, D_MODEL), (1, 0, 2))
        lat = jnp.concatenate([lat, lat], axis=1)
        cm = jnp.broadcast_to(mod[l, BATCH].reshape(6, 1, D_MODEL), (6, SUBLANES, D_MODEL))
        modtab = jnp.stack([cm, lat])

        xt, ax, u, gate, v, vc = _inproj(xt, modtab, norm1_g[l].reshape(1, D_MODEL), w_in[l].astype(BF16))

        wg = jnp.stack([jnp.concatenate([_block_diag(lru_wa[l, d]), _block_diag(lru_wi[l, d])], axis=1)
                        for d in range(2)]).astype(BF16)
        bg = jnp.concatenate([lru_ba[l], lru_bi[l]], axis=-1).reshape(2, 1, 2 * D_RNN)
        spl = (8.0 * jax.nn.softplus(-lru_lambda[l].astype(F32))).reshape(2, 1, D_RNN)
        bmat, cmat, lam = _s5_params(s5_lambda_re[l], s5_lambda_im[l], s5_log_dt[l],
                                     s5_b_re[l], s5_b_im[l], s5_c_re[l], s5_c_im[l])
        hf, ysf, hb, ysb = _scan(ax, u, lru_conv_w[l], lru_conv_b[l].reshape(1, D_RNN),
                                 wg, bg, spl, bmat, cmat, lam)

        m = _pool(v)
        mc = vc if last else _pool_ctx(vc)
        x1 = _mix(last, gate, hf, hb, u, ysf, ysb, m, mc, xt, modtab,
                  s5_d[l].reshape(1, D_SSM), s5_glu_w[l].astype(BF16), s5_glu_b[l].reshape(1, D_SSM),
                  _block_diag(pool_w[l]).astype(BF16), pool_b[l].reshape(1, D_POOL),
                  pool_scale[l].reshape(1, D_POOL), w_out[l].astype(BF16))
        wgu = jnp.concatenate([ffn_w_gate[l], ffn_w_up[l]], axis=1).astype(BF16)
        xt = _ffn(last, not last, x1, modtab, norm2_g[l].reshape(1, D_MODEL), wgu,
                  ffn_w_down[l].astype(BF16), fg)
    return xt
```

```python
import functools

import jax
import jax.numpy as jnp
from jax import lax
from jax.experimental import pallas as pl
from jax.experimental.pallas import tpu as pltpu

F32 = jnp.float32
BF16 = jnp.bfloat16

D_MODEL = 1024
BATCH = 4
SEQ = 8192
DEPTH = 2
CTX_LEN = 256
GRID_W = 64
GRID_H = SEQ // GRID_W
EPS = 1e-6

D_RNN = 384
CONV_WIDTH = 4

D_SSM = 384
SSM_GROUP = 16
SSM_GROUPS = D_SSM // SSM_GROUP
SSM_STATE = 64
N_STATE = SSM_GROUPS * SSM_STATE

D_POOL = 256
POOL_WINDOWS = (2, 4, 8, 16)
POOL_GROUP = D_POOL // len(POOL_WINDOWS)
POOL_PAD = 8

D_IN = 2 * D_RNN + D_SSM + D_POOL
D_MIX = D_RNN + D_SSM + D_POOL
D_FF = 2816
FF_CHUNKS = 2
FF_CHUNK = D_FF // FF_CHUNKS

LANES = 128
SUBLANES = 8
HALF = SUBLANES // 2
assert BATCH == HALF

N_TOK = CTX_LEN + SEQ
N_ROWS = N_TOK * BATCH
CTX_ROWS = CTX_LEN * BATCH
SEQ_ROWS = SEQ * BATCH

ROW_TILE = 1024
N_ROW_TILES = N_ROWS // ROW_TILE
assert CTX_ROWS == ROW_TILE
TAIL_TILE = 512
TAIL_CTX_TILES = CTX_ROWS // TAIL_TILE

SCAN_TOK = 128
SCAN_ROWS = SCAN_TOK * BATCH
SCAN_TILES = N_TOK // SCAN_TOK
SCAN_CTX_TILES = CTX_LEN // SCAN_TOK
SCAN_BLKS = SCAN_ROWS // SUBLANES
S5_BLOCKS = D_SSM // LANES
S5_BLK_STATES = N_STATE // S5_BLOCKS
SCAN_CHUNK = 512

POOL_GROWS = 16
POOL_TILE = POOL_GROWS * GRID_W * BATCH
POOL_HALO = POOL_PAD * GRID_W * BATCH
GROW = GRID_W * BATCH
PROW = (GRID_W + 2 * POOL_PAD) * BATCH

VMEM_LIMIT = 56 * 1024 * 1024


def _const_spec(shape):
    nd = len(shape)
    return pl.BlockSpec(shape, lambda *_: (0,) * nd, pipeline_mode=pl.Buffered(1))


def _rms(x):
    return x * lax.rsqrt(jnp.mean(x * x, axis=-1, keepdims=True) + EPS)


def _rows(pat, n):
    return jnp.broadcast_to(pat[None], (n // SUBLANES,) + pat.shape).reshape(n, pat.shape[-1])


def _mod_spec(off=0, ctx_tiles=1):
    return pl.BlockSpec((None, 6, SUBLANES, D_MODEL),
                        lambda i: (jnp.minimum((i + off) // ctx_tiles, 1), 0, 0, 0))


def _mod_kernel(c_ref, w_ref, b_ref, o_ref):
    s = jax.nn.silu(c_ref[...])
    o_ref[...] = jnp.dot(s.astype(BF16), w_ref[...].astype(BF16),
                         preferred_element_type=F32) + b_ref[...]


def _modulation(cc, w_mod, b_mod):
    n_col = 6 * D_MODEL
    blk = 1536
    return pl.pallas_call(
        _mod_kernel,
        grid=(DEPTH, n_col // blk),
        in_specs=[pl.BlockSpec((SUBLANES, D_MODEL), lambda l, j: (0, 0)),
                  pl.BlockSpec((None, D_MODEL, blk), lambda l, j: (l, 0, j)),
                  pl.BlockSpec((None, 1, blk), lambda l, j: (l, 0, j))],
        out_specs=pl.BlockSpec((None, SUBLANES, blk), lambda l, j: (l, 0, j)),
        out_shape=jax.ShapeDtypeStruct((DEPTH, SUBLANES, n_col), F32),
        compiler_params=pltpu.CompilerParams(vmem_limit_bytes=VMEM_LIMIT),
        name="modulation",
    )(cc, w_mod, b_mod.reshape(DEPTH, 1, n_col))


def _interleave(src_ref, slab):
    for b in range(BATCH):
        for s in range(D_MODEL // LANES):
            slab[s, pl.ds(b, ROW_TILE // BATCH, stride=BATCH), :] = src_ref[b, :, s * LANES:(s + 1) * LANES]


def _inproj_body(x, mod_ref, g_ref, w_ref, ax_ref, u_ref, gate_ref, v_ref, vc_ref):
    h = _rms(x) * g_ref[...]
    h = h * (1.0 + _rows(mod_ref[1], ROW_TILE)) + _rows(mod_ref[0], ROW_TILE)
    p = jnp.dot(h.astype(BF16), w_ref[...], preferred_element_type=F32)
    ax_ref[...] = p[:, 0:D_RNN]
    u_ref[...] = p[:, D_RNN:D_RNN + D_SSM].astype(BF16)
    gate_ref[...] = p[:, D_RNN + D_SSM:2 * D_RNN + D_SSM].astype(BF16)
    v = p[:, 2 * D_RNN + D_SSM:]
    v_ref[...] = v

    @pl.when(pl.program_id(0) == 0)
    def _():
        vc_ref[...] = v


def _inproj_kernel(x_ref, *rest):
    _inproj_body(x_ref[...], *rest)


def _inproj_first_kernel(ctx_ref, x_ref, mod_ref, g_ref, w_ref, xt_ref, *rest):
    slab = rest[-1]
    i = pl.program_id(0)

    @pl.when(i == 0)
    def _():
        _interleave(ctx_ref, slab)

    @pl.when(i > 0)
    def _():
        _interleave(x_ref, slab)

    x = jnp.concatenate([slab[s] for s in range(D_MODEL // LANES)], axis=1)
    xt_ref[...] = x
    _inproj_body(x, mod_ref, g_ref, w_ref, *rest[:-1])


def _inproj(xt, modtab, g, w_in):
    first = isinstance(xt, tuple)
    tok = lambda w: pl.BlockSpec((ROW_TILE, w), lambda i: (i, 0))
    out = lambda w, dt=F32: jax.ShapeDtypeStruct((N_ROWS, w), dt)
    out_specs = [tok(D_RNN), tok(D_SSM), tok(D_RNN),
                 pl.BlockSpec((ROW_TILE, D_POOL), lambda i: (jnp.maximum(i - 1, 0), 0)),
                 pl.BlockSpec((CTX_ROWS, D_POOL), lambda i: (0, 0))]
    out_shape = [out(D_RNN), out(D_SSM, BF16), out(D_RNN, BF16),
                 jax.ShapeDtypeStruct((SEQ_ROWS, D_POOL), F32),
                 jax.ShapeDtypeStruct((CTX_ROWS, D_POOL), F32)]
    common = [_mod_spec(), _const_spec((1, D_MODEL)), _const_spec((D_MODEL, D_IN))]
    params = pltpu.CompilerParams(dimension_semantics=("arbitrary",), vmem_limit_bytes=VMEM_LIMIT)
    if not first:
        return (xt,) + tuple(pl.pallas_call(
            _inproj_kernel, grid=(N_ROW_TILES,), in_specs=[tok(D_MODEL)] + common,
            out_specs=out_specs, out_shape=out_shape, compiler_params=params, name="inproj",
        )(xt, modtab, g, w_in))
    toks = ROW_TILE // BATCH
    src = [pl.BlockSpec((BATCH, CTX_LEN, D_MODEL), lambda i: (0, 0, 0), pipeline_mode=pl.Buffered(1)),
           pl.BlockSpec((BATCH, toks, D_MODEL), lambda i: (0, jnp.maximum(i - 1, 0), 0))]
    return tuple(pl.pallas_call(
        _inproj_first_kernel, grid=(N_ROW_TILES,), in_specs=src + common,
        out_specs=[tok(D_MODEL)] + out_specs, out_shape=[out(D_MODEL)] + out_shape,
        scratch_shapes=[pltpu.VMEM((D_MODEL // LANES, ROW_TILE, LANES), F32)],
        compiler_params=params, name="inproj_first",
    )(xt[0], xt[1], modtab, g, w_in))


def _fwd_tile(i):
    return i


def _bwd_tile(i):
    return jnp.where(i < SCAN_CTX_TILES, SCAN_CTX_TILES - 1 - i, SCAN_TILES + SCAN_CTX_TILES - 1 - i)


def _scan_kernel(axp_f, ax_f, axn_f, u_f, axp_b, ax_b, axn_b, u_b,
                 cw_ref, cb_ref, wg_ref, bg_ref, spl_ref, bmat_ref, cmat_ref, lam_ref,
                 hf_ref, ysf_ref, hb_ref, ysb_ref,
                 axs, a_f, b_f, a_b, b_b, uf32, ub32, us1, us2, xs1, xs2, ys1, ys2,
                 hfs, hbs, yfs, ybs, lcar, scar):
    i = pl.program_id(0)
    uf32[...] = u_f[...].astype(F32)
    ub32[...] = u_b[...].astype(F32)

    @pl.when(i == 0)
    def _():
        lcar[...] = jnp.zeros_like(lcar)
        scar[...] = jnp.zeros_like(scar)

    def gates(d, tile, axp, ax, axn, a_out, b_out):
        prev_ok = jnp.logical_and(tile != 0, tile != SCAN_CTX_TILES)
        next_ok = jnp.logical_and(tile != SCAN_CTX_TILES - 1, tile != SCAN_TILES - 1)
        axs[0:SUBLANES, :] = jnp.where(prev_ok, axp[...], 0.0)
        axs[SUBLANES:SUBLANES + SCAN_ROWS, :] = ax[...]
        axs[SUBLANES + SCAN_ROWS:, :] = jnp.where(next_ok, axn[...], 0.0)
        xc = cb_ref[...] + axs[0:SCAN_ROWS, :] * cw_ref[0:1, :]
        for k in range(1, CONV_WIDTH):
            xc = xc + axs[k * BATCH:k * BATCH + SCAN_ROWS, :] * cw_ref[k:k + 1, :]
        g = jnp.dot(xc.astype(BF16), wg_ref[d], preferred_element_type=F32) + bg_ref[d]
        r = jax.nn.sigmoid(g[:, :D_RNN])
        ig = jax.nn.sigmoid(g[:, D_RNN:])
        a = jnp.exp(-(r * spl_ref[d]))
        a_out[...] = a
        b_out[...] = jnp.sqrt(1.0 - a * a) * (ig * xc)

    gates(0, _fwd_tile(i), axp_f, ax_f, axn_f, a_f, b_f)
    gates(1, _bwd_tile(i), axp_b, ax_b, axn_b, a_b, b_b)

    lo = lax.broadcasted_iota(jnp.int32, (SUBLANES, 1), 0) < HALF

    def swap(x):
        return pltpu.roll(x, HALF, 0)

    def rows_of(j):
        return (pl.multiple_of(j * SUBLANES, SUBLANES),
                pl.multiple_of((SCAN_BLKS - 1 - j) * SUBLANES, SUBLANES))

    def steps(vf, vb):
        return jnp.where(lo, vf, vb), swap(jnp.where(lo, vb, vf))

    def unsteps(h1, h2):
        h2s = swap(h2)
        return jnp.where(lo, h1, h2s), jnp.where(lo, h2s, h1)

    def lru_body(j, h):
        rf, rb = rows_of(j)
        a1, a2 = steps(a_f[pl.ds(rf, SUBLANES), :], a_b[pl.ds(rb, SUBLANES), :])
        b1, b2 = steps(b_f[pl.ds(rf, SUBLANES), :], b_b[pl.ds(rb, SUBLANES), :])
        h1 = a1 * h + b1
        h2 = a2 * h1 + b2
        of, ob = unsteps(h1, h2)
        hfs[pl.ds(rf, SUBLANES), :] = of
        hbs[pl.ds(rb, SUBLANES), :] = ob
        return h2

    lcar[...] = lax.fori_loop(0, SCAN_BLKS, lru_body, lcar[...], unroll=True)
    hf_ref[...] = hfs[...].astype(BF16)
    hb_ref[...] = hbs[...].astype(BF16)

    def to_steps(j, _):
        rf, rb = rows_of(j)
        s1, s2 = steps(uf32[pl.ds(rf, SUBLANES), :], ub32[pl.ds(rb, SUBLANES), :])
        us1[pl.ds(rf, SUBLANES), :] = s1
        us2[pl.ds(rf, SUBLANES), :] = s2
        return 0

    lax.fori_loop(0, SCAN_BLKS, to_steps, 0, unroll=True)

    nst = S5_BLK_STATES
    for jb in range(S5_BLOCKS):
        blk = slice(jb * LANES, (jb + 1) * LANES)
        x1, x2 = xs1.at[jb], xs2.at[jb]
        x1[...] = jnp.dot(us1[:, blk].astype(BF16), bmat_ref[jb], preferred_element_type=F32)
        x2[...] = jnp.dot(us2[:, blk].astype(BF16), bmat_ref[jb], preferred_element_type=F32)
        for c in range(nst // SCAN_CHUNK):
            re = slice(c * SCAN_CHUNK, (c + 1) * SCAN_CHUNK)
            im = slice(nst + c * SCAN_CHUNK, nst + (c + 1) * SCAN_CHUNK)
            gre = slice(jb * nst + c * SCAN_CHUNK, jb * nst + (c + 1) * SCAN_CHUNK)
            gim = slice(N_STATE + jb * nst + c * SCAN_CHUNK, N_STATE + jb * nst + (c + 1) * SCAN_CHUNK)
            lr, li = lam_ref[0, :, gre], lam_ref[1, :, gre]

            def s5_body(j, carry, re=re, im=im, lr=lr, li=li, x1=x1, x2=x2):
                sr, si = carry
                rows = pl.ds(pl.multiple_of(j * SUBLANES, SUBLANES), SUBLANES)
                r1 = lr * sr - li * si + x1[rows, re]
                i1 = lr * si + li * sr + x1[rows, im]
                r2 = lr * r1 - li * i1 + x2[rows, re]
                i2 = lr * i1 + li * r1 + x2[rows, im]
                x1[rows, re] = r1
                x1[rows, im] = i1
                x2[rows, re] = r2
                x2[rows, im] = i2
                return r2, i2

            sr, si = lax.fori_loop(0, SCAN_BLKS, s5_body, (scar[:, gre], scar[:, gim]), unroll=True)
            scar[:, gre] = sr
            scar[:, gim] = si

        for xs, ys in ((x1, ys1), (x2, ys2)):
            y = jnp.dot(xs[:, :nst].astype(BF16), cmat_ref[jb, 0:nst, :], preferred_element_type=F32)
            y = y + jnp.dot(xs[:, nst:].astype(BF16), cmat_ref[jb, nst:, :], preferred_element_type=F32)
            ys[:, 2 * jb * LANES:2 * (jb + 1) * LANES] = y

    def from_steps(j, _):
        rf, rb = rows_of(j)
        of, ob = unsteps(ys1[pl.ds(rf, SUBLANES), :], ys2[pl.ds(rf, SUBLANES), :])
        for jb in range(S5_BLOCKS):
            yfs[pl.ds(rf, SUBLANES), jb * LANES:(jb + 1) * LANES] = of[:, 2 * jb * LANES:(2 * jb + 1) * LANES]
            ybs[pl.ds(rb, SUBLANES), jb * LANES:(jb + 1) * LANES] = ob[:, (2 * jb + 1) * LANES:2 * (jb + 1) * LANES]
        return 0

    lax.fori_loop(0, SCAN_BLKS, from_steps, 0, unroll=True)
    ysf_ref[...] = yfs[...].astype(BF16)
    ysb_ref[...] = ybs[...].astype(BF16)


def _scan(ax, u, cw, cb, wg, bg, spl, bmat, cmat, lam):
    tpb = SCAN_ROWS // SUBLANES
    nb8 = N_ROWS // SUBLANES

    def specs(tile_of):
        tok = lambda w: pl.BlockSpec((SCAN_ROWS, w), lambda i: (tile_of(i), 0))
        prev = pl.BlockSpec((SUBLANES, D_RNN), lambda i: (jnp.maximum(tile_of(i) * tpb - 1, 0), 0))
        nxt = pl.BlockSpec((SUBLANES, D_RNN), lambda i: (jnp.minimum((tile_of(i) + 1) * tpb, nb8 - 1), 0))
        return tok, prev, nxt

    tok_f, prev_f, next_f = specs(_fwd_tile)
    tok_b, prev_b, next_b = specs(_bwd_tile)
    out = jax.ShapeDtypeStruct((N_ROWS, D_RNN), BF16)
    tile_f32 = lambda w: pltpu.VMEM((SCAN_ROWS, w), F32)
    return pl.pallas_call(
        _scan_kernel,
        grid=(SCAN_TILES,),
        in_specs=[prev_f, tok_f(D_RNN), next_f, tok_f(D_SSM),
                  prev_b, tok_b(D_RNN), next_b, tok_b(D_SSM),
                  _const_spec((CONV_WIDTH, D_RNN)), _const_spec((1, D_RNN)),
                  _const_spec((2, D_RNN, 2 * D_RNN)), _const_spec((2, 1, 2 * D_RNN)),
                  _const_spec((2, 1, D_RNN)),
                  _const_spec((S5_BLOCKS, LANES, 2 * S5_BLK_STATES)),
                  _const_spec((S5_BLOCKS, 2 * S5_BLK_STATES, 2 * LANES)),
                  _const_spec((2, SUBLANES, N_STATE))],
        out_specs=[tok_f(D_RNN), tok_f(D_SSM), tok_b(D_RNN), tok_b(D_SSM)],
        out_shape=[out, out, out, out],
        scratch_shapes=[pltpu.VMEM((SCAN_ROWS + 2 * SUBLANES, D_RNN), F32),
                        tile_f32(D_RNN), tile_f32(D_RNN), tile_f32(D_RNN), tile_f32(D_RNN),
                        tile_f32(D_SSM), tile_f32(D_SSM), tile_f32(D_SSM), tile_f32(D_SSM),
                        pltpu.VMEM((S5_BLOCKS, SCAN_ROWS, 2 * S5_BLK_STATES), F32),
                        pltpu.VMEM((S5_BLOCKS, SCAN_ROWS, 2 * S5_BLK_STATES), F32),
                        tile_f32(2 * D_SSM), tile_f32(2 * D_SSM),
                        tile_f32(D_RNN), tile_f32(D_RNN), tile_f32(D_SSM), tile_f32(D_SSM),
                        pltpu.VMEM((SUBLANES, D_RNN), F32),
                        pltpu.VMEM((SUBLANES, 2 * N_STATE), F32)],
        compiler_params=pltpu.CompilerParams(
            dimension_semantics=("arbitrary",), vmem_limit_bytes=VMEM_LIMIT),
        name="scan",
    )(ax, ax, ax, u, ax, ax, ax, u, cw, cb, wg, bg, spl, bmat, cmat, lam)


def _window_sums(load, ha, hb):
    sa = load(-ha)
    for o in range(-ha + 1, ha):
        sa = sa + load(o)
    sb = sa
    for o in list(range(-hb, -ha)) + list(range(ha, hb)):
        sb = sb + load(o)
    return sa, sb


def _count(pos, half, n):
    return (jnp.minimum(pos + half, n) - jnp.maximum(pos - half, 0)).astype(F32)


def _pool_halves():
    is_a = lax.broadcasted_iota(jnp.int32, (1, LANES), 1) < POOL_GROUP
    for half in range(D_POOL // LANES):
        wa, wb = POOL_WINDOWS[2 * half], POOL_WINDOWS[2 * half + 1]
        yield slice(half * LANES, (half + 1) * LANES), wa // 2, wb // 2, is_a


def _pool_ctx_kernel(v_ref, m_ref, cp):
    pad = POOL_PAD * BATCH
    tok = lax.broadcasted_iota(jnp.int32, (CTX_ROWS, 1), 0) // BATCH
    for lanes, ha, hb, is_a in _pool_halves():
        cp[...] = jnp.zeros_like(cp)
        cp[pad:pad + CTX_ROWS, :] = v_ref[:, lanes]
        ca, cb = _window_sums(lambda o: cp[pad + o * BATCH:pad + o * BATCH + CTX_ROWS, :], ha, hb)
        pooled = jnp.where(is_a, ca / _count(tok, ha, CTX_LEN), cb / _count(tok, hb, CTX_LEN))
        m_ref[:, lanes] = pooled - v_ref[:, lanes]


def _pool_ctx(vc):
    return pl.pallas_call(
        _pool_ctx_kernel,
        out_shape=jax.ShapeDtypeStruct((CTX_ROWS, D_POOL), F32),
        scratch_shapes=[pltpu.VMEM((CTX_ROWS + 2 * POOL_PAD * BATCH, LANES), F32)],
        name="pool_ctx",
    )(vc)


def _pool_kernel(vprev_ref, v_ref, vnext_ref, m_ref, vp, sp):
    i = pl.program_id(0)
    g0 = i * POOL_GROWS
    pad = POOL_PAD * BATCH
    col = lax.broadcasted_iota(jnp.int32, (GROW, 1), 0) // BATCH
    have_prev = i > 0
    have_next = i < pl.num_programs(0) - 1
    for lanes, ha, hb, is_a in _pool_halves():
        vp[...] = jnp.zeros_like(vp)

        def fill(src_ref, first, keep, lanes=lanes):
            def body(q, _):
                dst = pl.multiple_of((first + q) * PROW + pad, SUBLANES)
                src = pl.multiple_of(q * GROW, SUBLANES)
                vp[pl.ds(dst, GROW), :] = jnp.where(keep, src_ref[pl.ds(src, GROW), lanes], 0.0)
                return 0
            return body

        lax.fori_loop(0, POOL_PAD, fill(vprev_ref, 0, have_prev), 0)
        lax.fori_loop(0, POOL_GROWS, fill(v_ref, POOL_PAD, True), 0)
        lax.fori_loop(0, POOL_PAD, fill(vnext_ref, POOL_PAD + POOL_GROWS, have_next), 0)
        ccnt_a, ccnt_b = _count(col, ha, GRID_W), _count(col, hb, GRID_W)

        def body(r, _, lanes=lanes, ha=ha, hb=hb, is_a=is_a, ccnt_a=ccnt_a, ccnt_b=ccnt_b):
            base = (r + POOL_PAD) * PROW
            sa, sb = _window_sums(
                lambda o: vp[pl.ds(pl.multiple_of(base + o * PROW, SUBLANES), PROW), :], ha, hb)
            sp[...] = jnp.where(is_a, sa, sb)
            ba, bb = _window_sums(lambda o: sp[pad + o * BATCH:pad + o * BATCH + GROW, :], ha, hb)
            rcnt_a, rcnt_b = _count(g0 + r, ha, GRID_H), _count(g0 + r, hb, GRID_H)
            pooled = jnp.where(is_a, ba / (rcnt_a * ccnt_a), bb / (rcnt_b * ccnt_b))
            src = pl.multiple_of(r * GROW, SUBLANES)
            m_ref[pl.ds(src, GROW), lanes] = pooled - v_ref[pl.ds(src, GROW), lanes]
            return 0

        lax.fori_loop(0, POOL_GROWS, body, 0)


def _pool(v):
    n = SEQ_ROWS // POOL_TILE
    per = POOL_TILE // POOL_HALO
    nh = SEQ_ROWS // POOL_HALO
    return pl.pallas_call(
        _pool_kernel,
        grid=(n,),
        in_specs=[pl.BlockSpec((POOL_HALO, D_POOL), lambda i: (jnp.maximum(i * per - 1, 0), 0)),
                  pl.BlockSpec((POOL_TILE, D_POOL), lambda i: (i, 0)),
                  pl.BlockSpec((POOL_HALO, D_POOL), lambda i: (jnp.minimum((i + 1) * per, nh - 1), 0))],
        out_specs=pl.BlockSpec((POOL_TILE, D_POOL), lambda i: (i, 0)),
        out_shape=jax.ShapeDtypeStruct((SEQ_ROWS, D_POOL), F32),
        scratch_shapes=[pltpu.VMEM(((POOL_GROWS + 2 * POOL_PAD) * PROW, LANES), F32),
                        pltpu.VMEM((PROW, LANES), F32)],
        compiler_params=pltpu.CompilerParams(
            dimension_semantics=("parallel",), vmem_limit_bytes=VMEM_LIMIT),
        name="pool",
    )(v, v, v)


def _tail_kernel(final, *refs):
    if final:
        gate_ref, hf_ref, hb_ref, u_ref, ysf_ref, ysb_ref, m_ref, x_ref, mod_ref = refs[:9]
        rest = refs[9:]
    else:
        gate_ref, hf_ref, hb_ref, u_ref, ysf_ref, ysb_ref, m_ref, mc_ref, x_ref, mod_ref = refs[:10]
        rest = refs[10:]
    (dvec_ref, wglu_ref, bglu_ref, pw_ref, pb_ref, ps_ref, wo_ref,
     g2_ref, wgu_ref, wd_ref, fg_ref, o_ref) = rest[:12]
    dot = functools.partial(jnp.dot, preferred_element_type=F32)
    f32 = lambda ref: ref[...].astype(F32)

    y_a = jax.nn.gelu(f32(gate_ref)) * (f32(hf_ref) + f32(hb_ref))
    y_s = dvec_ref[...] * f32(u_ref) + f32(ysf_ref) + f32(ysb_ref)
    z = jax.nn.gelu(y_s)
    y_b = z * jax.nn.sigmoid(dot(z.astype(BF16), wglu_ref[...]) + bglu_ref[...])
    m = m_ref[...] if final else jnp.where(pl.program_id(0) < TAIL_CTX_TILES, mc_ref[...], m_ref[...])
    y_c = (dot(m.astype(BF16), pw_ref[...]) + pb_ref[...]) * ps_ref[...]
    o = dot(y_a.astype(BF16), wo_ref[0:D_RNN, :])
    o = o + dot(y_b.astype(BF16), wo_ref[D_RNN:D_RNN + D_SSM, :])
    o = o + dot(y_c.astype(BF16), wo_ref[D_RNN + D_SSM:, :])
    x = x_ref[...] + _rows(mod_ref[2], TAIL_TILE) * o

    h = _rms(x) * g2_ref[...]
    h = (h * (1.0 + _rows(mod_ref[4], TAIL_TILE)) + _rows(mod_ref[3], TAIL_TILE)).astype(BF16)
    ff = None
    for c in range(FF_CHUNKS):
        cols = slice(c * FF_CHUNK, (c + 1) * FF_CHUNK)
        ucols = slice(D_FF + c * FF_CHUNK, D_FF + (c + 1) * FF_CHUNK)
        act = (jax.nn.silu(dot(h, wgu_ref[:, cols])) * dot(h, wgu_ref[:, ucols])).astype(BF16)
        part = dot(act, wd_ref[cols, :])
        ff = part if ff is None else ff + part
    y = x + _rows(mod_ref[5], TAIL_TILE) * ff
    if not final:
        o_ref[...] = y
        return
    y = _rms(y) * fg_ref[...]
    slab = rest[12]
    for s in range(D_MODEL // LANES):
        slab[s] = y[:, s * LANES:(s + 1) * LANES]
    for b in range(BATCH):
        for s in range(D_MODEL // LANES):
            o_ref[b, :, s * LANES:(s + 1) * LANES] = slab[s, pl.ds(b, TAIL_TILE // BATCH, stride=BATCH), :]


def _tail(final, gate, hf, hb, u, ysf, ysb, m, mc, xt, modtab, weights):
    off = TAIL_CTX_TILES if final else 0
    n_tiles = N_ROWS // TAIL_TILE - off
    tok = lambda w: pl.BlockSpec((TAIL_TILE, w), lambda i: (i + off, 0))
    acts = [gate, hf, hb, u, ysf, ysb, m]
    act_specs = [tok(D_RNN), tok(D_RNN), tok(D_RNN), tok(D_SSM), tok(D_SSM), tok(D_SSM),
                 pl.BlockSpec((TAIL_TILE, D_POOL), lambda i: (jnp.maximum(i + off - TAIL_CTX_TILES, 0), 0))]
    if final:
        toks = TAIL_TILE // BATCH
        out_spec = pl.BlockSpec((BATCH, toks, D_MODEL), lambda i: (0, i, 0))
        out_shape = jax.ShapeDtypeStruct((BATCH, n_tiles * toks, D_MODEL), F32)
        scratch = [pltpu.VMEM((D_MODEL // LANES, TAIL_TILE, LANES), F32)]
    else:
        acts.append(mc)
        act_specs.append(pl.BlockSpec((TAIL_TILE, D_POOL), lambda i: (jnp.minimum(i, TAIL_CTX_TILES - 1), 0)))
        out_spec = pl.BlockSpec((TAIL_TILE, D_MODEL), lambda i: (i, 0))
        out_shape = jax.ShapeDtypeStruct((N_ROWS, D_MODEL), F32)
        scratch = []
    return pl.pallas_call(
        functools.partial(_tail_kernel, final),
        grid=(n_tiles,),
        in_specs=(act_specs + [tok(D_MODEL), _mod_spec(off, TAIL_CTX_TILES)]
                  + [_const_spec(w.shape) for w in weights]),
        out_specs=out_spec,
        out_shape=out_shape,
        scratch_shapes=scratch,
        compiler_params=pltpu.CompilerParams(
            dimension_semantics=("parallel",), vmem_limit_bytes=VMEM_LIMIT),
        name="tail_final" if final else "tail",
    )(*acts, xt, modtab, *weights)


def _block_diag(w):
    n, di, dj = w.shape
    return jnp.einsum('nij,nm->nimj', w, jnp.eye(n, dtype=w.dtype)).reshape(n * di, n * dj)


def _s5_params(lam_re, lam_im, log_dt, b_re, b_im, c_re, c_im):
    gpb = SSM_GROUPS // S5_BLOCKS

    def in_blocks(w):
        w = jnp.swapaxes(w.astype(F32), 1, 2).reshape(S5_BLOCKS, gpb, SSM_GROUP, SSM_STATE)
        return jnp.stack([_block_diag(w[j]) for j in range(S5_BLOCKS)])

    def out_blocks(w):
        w = jnp.swapaxes(w, 1, 2).reshape(S5_BLOCKS, gpb, SSM_STATE, SSM_GROUP)
        return jnp.stack([_block_diag(w[j]) for j in range(S5_BLOCKS)])

    bmat = jnp.concatenate([in_blocks(b_re), in_blocks(b_im)], axis=2).astype(BF16)
    cmats, lams = [], []
    for d in range(2):
        lr, li = lam_re[d].astype(F32), lam_im[d].astype(F32)
        dt = jnp.exp(log_dt[d].astype(F32))[:, None]
        mag = jnp.exp(lr * dt)
        ang = li * dt
        bar_r, bar_i = mag * jnp.cos(ang), mag * jnp.sin(ang)
        den = lr * lr + li * li
        fr = ((bar_r - 1.0) * lr + bar_i * li) / den
        fi = (bar_i * lr - (bar_r - 1.0) * li) / den
        cr, ci = c_re[d].astype(F32), c_im[d].astype(F32)
        cfr = cr * fr[:, None, :] - ci * fi[:, None, :]
        cfi = cr * fi[:, None, :] + ci * fr[:, None, :]
        cmats.append(jnp.concatenate([out_blocks(cfr), out_blocks(-cfi)], axis=1))
        lams.append(jnp.stack([bar_r.reshape(-1), bar_i.reshape(-1)]))
    cmat = jnp.concatenate(cmats, axis=-1).astype(BF16)
    lam = jnp.concatenate([jnp.broadcast_to(lams[d][:, None, :], (2, HALF, N_STATE)) for d in range(2)],
                          axis=1)
    return bmat, cmat, lam


def kernel(x, c, ctx, c_ctx, w_mod, b_mod, norm1_g, norm2_g, w_in, w_out, lru_conv_w, lru_conv_b,
           lru_wa, lru_ba, lru_wi, lru_bi, lru_lambda, s5_lambda_re, s5_lambda_im, s5_log_dt,
           s5_b_re, s5_b_im, s5_c_re, s5_c_im, s5_d, s5_glu_w, s5_glu_b, pool_w, pool_b, pool_scale,
           ffn_w_gate, ffn_w_up, ffn_w_down, final_g):
    cc = jnp.zeros((SUBLANES, D_MODEL), F32).at[:BATCH].set(c).at[BATCH].set(c_ctx)
    mod = _modulation(cc, w_mod, b_mod)
    xt = (ctx, x)
    fg = final_g.reshape(1, D_MODEL)

    for l in range(DEPTH):
        last = l == DEPTH - 1
        lat = jnp.transpose(mod[l, :BATCH].reshape(BATCH, 6, D_MODEL), (1, 0, 2))
        lat = jnp.concatenate([lat, lat], axis=1)
        cm = jnp.broadcast_to(mod[l, BATCH].reshape(6, 1, D_MODEL), (6, SUBLANES, D_MODEL))
        modtab = jnp.stack([cm, lat])

        xt, ax, u, gate, v, vc = _inproj(xt, modtab, norm1_g[l].reshape(1, D_MODEL), w_in[l].astype(BF16))

        wg = jnp.stack([jnp.concatenate([_block_diag(lru_wa[l, d]), _block_diag(lru_wi[l, d])], axis=1)
                        for d in range(2)]).astype(BF16)
        bg = jnp.concatenate([lru_ba[l], lru_bi[l]], axis=-1).reshape(2, 1, 2 * D_RNN)
        spl = (8.0 * jax.nn.softplus(-lru_lambda[l].astype(F32))).reshape(2, 1, D_RNN)
        bmat, cmat, lam = _s5_params(s5_lambda_re[l], s5_lambda_im[l], s5_log_dt[l],
                                     s5_b_re[l], s5_b_im[l], s5_c_re[l], s5_c_im[l])
        hf, ysf, hb, ysb = _scan(ax, u, lru_conv_w[l], lru_conv_b[l].reshape(1, D_RNN),
                                 wg, bg, spl, bmat, cmat, lam)

        m = _pool(v)
        mc = None if last else _pool_ctx(vc)
        wgu = jnp.concatenate([ffn_w_gate[l], ffn_w_up[l]], axis=1).astype(BF16)
        weights = (s5_d[l].reshape(1, D_SSM), s5_glu_w[l].astype(BF16), s5_glu_b[l].reshape(1, D_SSM),
                   _block_diag(pool_w[l]).astype(BF16), pool_b[l].reshape(1, D_POOL),
                   pool_scale[l].reshape(1, D_POOL), w_out[l].astype(BF16),
                   norm2_g[l].reshape(1, D_MODEL), wgu, ffn_w_down[l].astype(BF16), fg)
        xt = _tail(last, gate, hf, hb, u, ysf, ysb, m, mc, xt, modtab, weights)
    return xt
```

```python
import functools

import jax
import jax.numpy as jnp
from jax import lax
from jax.experimental import pallas as pl
from jax.experimental.pallas import tpu as pltpu

F32 = jnp.float32
BF16 = jnp.bfloat16

D_MODEL = 1024
BATCH = 4
SEQ = 8192
DEPTH = 2
CTX_LEN = 256
GRID_W = 64
GRID_H = SEQ // GRID_W
EPS = 1e-6

D_RNN = 384
CONV_WIDTH = 4

D_SSM = 384
SSM_GROUP = 16
SSM_GROUPS = D_SSM // SSM_GROUP
SSM_STATE = 64
N_STATE = SSM_GROUPS * SSM_STATE

D_POOL = 256
POOL_WINDOWS = (2, 4, 8, 16)
POOL_GROUP = D_POOL // len(POOL_WINDOWS)
POOL_PAD = 8

D_IN = 2 * D_RNN + D_SSM + D_POOL
D_MIX = D_RNN + D_SSM + D_POOL
D_FF = 2816
MXU_DIM = 256
FF_EDGES = (0, 6 * MXU_DIM, D_FF)

LANES = 128
SUBLANES = 8
HALF = SUBLANES // 2
assert BATCH == HALF

N_TOK = CTX_LEN + SEQ
N_ROWS = N_TOK * BATCH
CTX_ROWS = CTX_LEN * BATCH
SEQ_ROWS = SEQ * BATCH

ROW_TILE = 1024
N_ROW_TILES = N_ROWS // ROW_TILE
assert CTX_ROWS == ROW_TILE
TAIL_TILE = 512
TAIL_CTX_TILES = CTX_ROWS // TAIL_TILE

SCAN_TOK = 128
SCAN_ROWS = SCAN_TOK * BATCH
SCAN_TILES = N_TOK // SCAN_TOK
SCAN_CTX_TILES = CTX_LEN // SCAN_TOK
SCAN_BLKS = SCAN_ROWS // SUBLANES
S5_BLOCKS = D_SSM // LANES
S5_BLK_STATES = N_STATE // S5_BLOCKS
SCAN_CHUNK = 512

POOL_GROWS = 16
POOL_TILE = POOL_GROWS * GRID_W * BATCH
POOL_HALO = POOL_PAD * GRID_W * BATCH
GROW = GRID_W * BATCH
PROW = (GRID_W + 2 * POOL_PAD) * BATCH

VMEM_LIMIT = 56 * 1024 * 1024


def _const_spec(shape):
    nd = len(shape)
    return pl.BlockSpec(shape, lambda *_: (0,) * nd, pipeline_mode=pl.Buffered(1))


def _rms(x):
    return x * lax.rsqrt(jnp.mean(x * x, axis=-1, keepdims=True) + EPS)


def _rows(pat, n):
    return jnp.broadcast_to(pat[None], (n // SUBLANES,) + pat.shape).reshape(n, pat.shape[-1])


def _mod_spec(off=0, ctx_tiles=1):
    return pl.BlockSpec((None, 6, SUBLANES, D_MODEL),
                        lambda i: (jnp.minimum((i + off) // ctx_tiles, 1), 0, 0, 0))


def _mod_kernel(c_ref, w_ref, b_ref, o_ref):
    s = jax.nn.silu(c_ref[...])
    o_ref[...] = jnp.dot(s.astype(BF16), w_ref[...].astype(BF16),
                         preferred_element_type=F32) + b_ref[...]


def _modulation(cc, w_mod, b_mod):
    n_col = 6 * D_MODEL
    blk = 1536
    return pl.pallas_call(
        _mod_kernel,
        grid=(DEPTH, n_col // blk),
        in_specs=[pl.BlockSpec((SUBLANES, D_MODEL), lambda l, j: (0, 0)),
                  pl.BlockSpec((None, D_MODEL, blk), lambda l, j: (l, 0, j)),
                  pl.BlockSpec((None, 1, blk), lambda l, j: (l, 0, j))],
        out_specs=pl.BlockSpec((None, SUBLANES, blk), lambda l, j: (l, 0, j)),
        out_shape=jax.ShapeDtypeStruct((DEPTH, SUBLANES, n_col), F32),
        compiler_params=pltpu.CompilerParams(vmem_limit_bytes=VMEM_LIMIT),
        name="modulation",
    )(cc, w_mod, b_mod.reshape(DEPTH, 1, n_col))


def _interleave(src_ref, slab):
    for b in range(BATCH):
        for s in range(D_MODEL // LANES):
            slab[s, pl.ds(b, ROW_TILE // BATCH, stride=BATCH), :] = src_ref[b, :, s * LANES:(s + 1) * LANES]


def _inproj_body(x, mod_ref, g_ref, w_ref, ax_ref, u_ref, gate_ref, v_ref, vc_ref):
    h = _rms(x) * g_ref[...]
    h = h * (1.0 + _rows(mod_ref[1], ROW_TILE)) + _rows(mod_ref[0], ROW_TILE)
    p = jnp.dot(h.astype(BF16), w_ref[...], preferred_element_type=F32)
    ax_ref[...] = p[:, 0:D_RNN]
    u_ref[...] = p[:, D_RNN:D_RNN + D_SSM].astype(BF16)
    gate_ref[...] = p[:, D_RNN + D_SSM:2 * D_RNN + D_SSM].astype(BF16)
    v = p[:, 2 * D_RNN + D_SSM:]
    v_ref[...] = v

    @pl.when(pl.program_id(0) == 0)
    def _():
        vc_ref[...] = v


def _inproj_kernel(x_ref, *rest):
    _inproj_body(x_ref[...], *rest)


def _inproj_first_kernel(ctx_ref, x_ref, mod_ref, g_ref, w_ref, xt_ref, *rest):
    slab = rest[-1]
    i = pl.program_id(0)

    @pl.when(i == 0)
    def _():
        _interleave(ctx_ref, slab)

    @pl.when(i > 0)
    def _():
        _interleave(x_ref, slab)

    x = jnp.concatenate([slab[s] for s in range(D_MODEL // LANES)], axis=1)
    xt_ref[...] = x
    _inproj_body(x, mod_ref, g_ref, w_ref, *rest[:-1])


def _inproj(xt, modtab, g, w_in):
    first = isinstance(xt, tuple)
    tok = lambda w: pl.BlockSpec((ROW_TILE, w), lambda i: (i, 0))
    out = lambda w, dt=F32: jax.ShapeDtypeStruct((N_ROWS, w), dt)
    out_specs = [tok(D_RNN), tok(D_SSM), tok(D_RNN),
                 pl.BlockSpec((ROW_TILE, D_POOL), lambda i: (jnp.maximum(i - 1, 0), 0)),
                 pl.BlockSpec((CTX_ROWS, D_POOL), lambda i: (0, 0))]
    out_shape = [out(D_RNN), out(D_SSM, BF16), out(D_RNN, BF16),
                 jax.ShapeDtypeStruct((SEQ_ROWS, D_POOL), F32),
                 jax.ShapeDtypeStruct((CTX_ROWS, D_POOL), F32)]
    common = [_mod_spec(), _const_spec((1, D_MODEL)), _const_spec((D_MODEL, D_IN))]
    params = pltpu.CompilerParams(dimension_semantics=("arbitrary",), vmem_limit_bytes=VMEM_LIMIT)
    if not first:
        return (xt,) + tuple(pl.pallas_call(
            _inproj_kernel, grid=(N_ROW_TILES,), in_specs=[tok(D_MODEL)] + common,
            out_specs=out_specs, out_shape=out_shape, compiler_params=params, name="inproj",
        )(xt, modtab, g, w_in))
    toks = ROW_TILE // BATCH
    src = [pl.BlockSpec((BATCH, CTX_LEN, D_MODEL), lambda i: (0, 0, 0), pipeline_mode=pl.Buffered(1)),
           pl.BlockSpec((BATCH, toks, D_MODEL), lambda i: (0, jnp.maximum(i - 1, 0), 0))]
    return tuple(pl.pallas_call(
        _inproj_first_kernel, grid=(N_ROW_TILES,), in_specs=src + common,
        out_specs=[tok(D_MODEL)] + out_specs, out_shape=[out(D_MODEL)] + out_shape,
        scratch_shapes=[pltpu.VMEM((D_MODEL // LANES, ROW_TILE, LANES), F32)],
        compiler_params=params, name="inproj_first",
    )(xt[0], xt[1], modtab, g, w_in))


def _fwd_tile(i):
    return i


def _bwd_tile(i):
    return jnp.where(i < SCAN_CTX_TILES, SCAN_CTX_TILES - 1 - i, SCAN_TILES + SCAN_CTX_TILES - 1 - i)


def _scan_kernel(axp_f, ax_f, axn_f, u_f, axp_b, ax_b, axn_b, u_b,
                 cw_ref, cb_ref, wg_ref, bg_ref, spl_ref, bmat_ref, cmat_ref, lam_ref,
                 hf_ref, ysf_ref, hb_ref, ysb_ref,
                 axs, a_f, b_f, a_b, b_b, uf32, ub32, us1, us2, xs1, xs2, ys1, ys2,
                 hfs, hbs, yfs, ybs, lcar, scar):
    i = pl.program_id(0)
    uf32[...] = u_f[...].astype(F32)
    ub32[...] = u_b[...].astype(F32)

    @pl.when(i == 0)
    def _():
        lcar[...] = jnp.zeros_like(lcar)
        scar[...] = jnp.zeros_like(scar)

    def gates(d, tile, axp, ax, axn, a_out, b_out):
        prev_ok = jnp.logical_and(tile != 0, tile != SCAN_CTX_TILES)
        next_ok = jnp.logical_and(tile != SCAN_CTX_TILES - 1, tile != SCAN_TILES - 1)
        axs[0:SUBLANES, :] = jnp.where(prev_ok, axp[...], 0.0)
        axs[SUBLANES:SUBLANES + SCAN_ROWS, :] = ax[...]
        axs[SUBLANES + SCAN_ROWS:, :] = jnp.where(next_ok, axn[...], 0.0)
        xc = cb_ref[...] + axs[0:SCAN_ROWS, :] * cw_ref[0:1, :]
        for k in range(1, CONV_WIDTH):
            xc = xc + axs[k * BATCH:k * BATCH + SCAN_ROWS, :] * cw_ref[k:k + 1, :]
        xcb = xc.astype(BF16)
        g = [jnp.dot(xcb[:, k * LANES:(k + 1) * LANES], wg_ref[d, k], preferred_element_type=F32)
             for k in range(D_RNN // LANES)]
        r = jax.nn.sigmoid(jnp.concatenate([gk[:, :LANES] for gk in g], axis=1) + bg_ref[d, :, :D_RNN])
        ig = jax.nn.sigmoid(jnp.concatenate([gk[:, LANES:] for gk in g], axis=1) + bg_ref[d, :, D_RNN:])
        a = jnp.exp(-(r * spl_ref[d]))
        a_out[...] = a
        b_out[...] = jnp.sqrt(1.0 - a * a) * (ig * xc)

    lo = lax.broadcasted_iota(jnp.int32, (SUBLANES, 1), 0) < HALF

    def swap(x):
        return pltpu.roll(x, HALF, 0)

    def rows_of(j):
        return (pl.multiple_of(j * SUBLANES, SUBLANES),
                pl.multiple_of((SCAN_BLKS - 1 - j) * SUBLANES, SUBLANES))

    def steps(vf, vb):
        return jnp.where(lo, vf, vb), swap(jnp.where(lo, vb, vf))

    def unsteps(h1, h2):
        h2s = swap(h2)
        return jnp.where(lo, h1, h2s), jnp.where(lo, h2s, h1)

    def lru_body(j, h):
        rf, rb = rows_of(j)
        a1, a2 = steps(a_f[pl.ds(rf, SUBLANES), :], a_b[pl.ds(rb, SUBLANES), :])
        b1, b2 = steps(b_f[pl.ds(rf, SUBLANES), :], b_b[pl.ds(rb, SUBLANES), :])
        h1 = a1 * h + b1
        h2 = a2 * h1 + b2
        of, ob = unsteps(h1, h2)
        hfs[pl.ds(rf, SUBLANES), :] = of
        hbs[pl.ds(rb, SUBLANES), :] = ob
        return h2

    def to_steps(j, _):
        rf, rb = rows_of(j)
        s1, s2 = steps(uf32[pl.ds(rf, SUBLANES), :], ub32[pl.ds(rb, SUBLANES), :])
        us1[pl.ds(rf, SUBLANES), :] = s1
        us2[pl.ds(rf, SUBLANES), :] = s2
        return 0

    gates(0, _fwd_tile(i), axp_f, ax_f, axn_f, a_f, b_f)
    gates(1, _bwd_tile(i), axp_b, ax_b, axn_b, a_b, b_b)
    lcar[...] = lax.fori_loop(0, SCAN_BLKS, lru_body, lcar[...], unroll=True)
    hf_ref[...] = hfs[...].astype(BF16)
    hb_ref[...] = hbs[...].astype(BF16)
    lax.fori_loop(0, SCAN_BLKS, to_steps, 0, unroll=True)

    nst = S5_BLK_STATES
    for jb in range(S5_BLOCKS):
        blk = slice(jb * LANES, (jb + 1) * LANES)
        x1, x2 = xs1.at[jb], xs2.at[jb]
        x1[...] = jnp.dot(us1[:, blk].astype(BF16), bmat_ref[jb], preferred_element_type=F32)
        x2[...] = jnp.dot(us2[:, blk].astype(BF16), bmat_ref[jb], preferred_element_type=F32)
        for c in range(nst // SCAN_CHUNK):
            re = slice(c * SCAN_CHUNK, (c + 1) * SCAN_CHUNK)
            im = slice(nst + c * SCAN_CHUNK, nst + (c + 1) * SCAN_CHUNK)
            gre = slice(jb * nst + c * SCAN_CHUNK, jb * nst + (c + 1) * SCAN_CHUNK)
            gim = slice(N_STATE + jb * nst + c * SCAN_CHUNK, N_STATE + jb * nst + (c + 1) * SCAN_CHUNK)
            lr, li = lam_ref[0, :, gre], lam_ref[1, :, gre]

            def s5_body(j, carry, re=re, im=im, lr=lr, li=li, x1=x1, x2=x2):
                sr, si = carry
                rows = pl.ds(pl.multiple_of(j * SUBLANES, SUBLANES), SUBLANES)
                r1 = lr * sr - li * si + x1[rows, re]
                i1 = lr * si + li * sr + x1[rows, im]
                r2 = lr * r1 - li * i1 + x2[rows, re]
                i2 = lr * i1 + li * r1 + x2[rows, im]
                x1[rows, re] = r1
                x1[rows, im] = i1
                x2[rows, re] = r2
                x2[rows, im] = i2
                return r2, i2

            sr, si = lax.fori_loop(0, SCAN_BLKS, s5_body, (scar[:, gre], scar[:, gim]), unroll=True)
            scar[:, gre] = sr
            scar[:, gim] = si
        for xs, ys in ((x1, ys1), (x2, ys2)):
            y = jnp.dot(xs[:, :nst].astype(BF16), cmat_ref[jb, 0:nst, :], preferred_element_type=F32)
            y = y + jnp.dot(xs[:, nst:].astype(BF16), cmat_ref[jb, nst:, :], preferred_element_type=F32)
            ys[:, 2 * jb * LANES:2 * (jb + 1) * LANES] = y

    def from_steps(j, _):
        rf, rb = rows_of(j)
        of, ob = unsteps(ys1[pl.ds(rf, SUBLANES), :], ys2[pl.ds(rf, SUBLANES), :])
        for jb in range(S5_BLOCKS):
            yfs[pl.ds(rf, SUBLANES), jb * LANES:(jb + 1) * LANES] = of[:, 2 * jb * LANES:(2 * jb + 1) * LANES]
            ybs[pl.ds(rb, SUBLANES), jb * LANES:(jb + 1) * LANES] = ob[:, (2 * jb + 1) * LANES:2 * (jb + 1) * LANES]
        return 0

    lax.fori_loop(0, SCAN_BLKS, from_steps, 0, unroll=True)
    ysf_ref[...] = yfs[...].astype(BF16)
    ysb_ref[...] = ybs[...].astype(BF16)


def _scan(ax, u, cw, cb, wg, bg, spl, bmat, cmat, lam):
    tpb = SCAN_ROWS // SUBLANES
    nb8 = N_ROWS // SUBLANES

    def specs(tile_of):
        tok = lambda w: pl.BlockSpec((SCAN_ROWS, w), lambda i: (tile_of(i), 0))
        prev = pl.BlockSpec((SUBLANES, D_RNN), lambda i: (jnp.maximum(tile_of(i) * tpb - 1, 0), 0))
        nxt = pl.BlockSpec((SUBLANES, D_RNN), lambda i: (jnp.minimum((tile_of(i) + 1) * tpb, nb8 - 1), 0))
        return tok, prev, nxt

    tok_f, prev_f, next_f = specs(_fwd_tile)
    tok_b, prev_b, next_b = specs(_bwd_tile)
    out = jax.ShapeDtypeStruct((N_ROWS, D_RNN), BF16)
    tile_f32 = lambda w: pltpu.VMEM((SCAN_ROWS, w), F32)
    return pl.pallas_call(
        _scan_kernel,
        grid=(SCAN_TILES,),
        in_specs=[prev_f, tok_f(D_RNN), next_f, tok_f(D_SSM),
                  prev_b, tok_b(D_RNN), next_b, tok_b(D_SSM),
                  _const_spec((CONV_WIDTH, D_RNN)), _const_spec((1, D_RNN)),
                  _const_spec((2, D_RNN // LANES, LANES, 2 * LANES)), _const_spec((2, 1, 2 * D_RNN)),
                  _const_spec((2, 1, D_RNN)),
                  _const_spec((S5_BLOCKS, LANES, 2 * S5_BLK_STATES)),
                  _const_spec((S5_BLOCKS, 2 * S5_BLK_STATES, 2 * LANES)),
                  _const_spec((2, SUBLANES, N_STATE))],
        out_specs=[tok_f(D_RNN), tok_f(D_SSM), tok_b(D_RNN), tok_b(D_SSM)],
        out_shape=[out, out, out, out],
        scratch_shapes=[pltpu.VMEM((SCAN_ROWS + 2 * SUBLANES, D_RNN), F32),
                        tile_f32(D_RNN), tile_f32(D_RNN), tile_f32(D_RNN), tile_f32(D_RNN),
                        tile_f32(D_SSM), tile_f32(D_SSM), tile_f32(D_SSM), tile_f32(D_SSM),
                        pltpu.VMEM((S5_BLOCKS, SCAN_ROWS, 2 * S5_BLK_STATES), F32),
                        pltpu.VMEM((S5_BLOCKS, SCAN_ROWS, 2 * S5_BLK_STATES), F32),
                        tile_f32(2 * D_SSM), tile_f32(2 * D_SSM),
                        tile_f32(D_RNN), tile_f32(D_RNN), tile_f32(D_SSM), tile_f32(D_SSM),
                        pltpu.VMEM((SUBLANES, D_RNN), F32),
                        pltpu.VMEM((SUBLANES, 2 * N_STATE), F32)],
        compiler_params=pltpu.CompilerParams(
            dimension_semantics=("arbitrary",), vmem_limit_bytes=VMEM_LIMIT),
        name="scan",
    )(ax, ax, ax, u, ax, ax, ax, u, cw, cb, wg, bg, spl, bmat, cmat, lam)


def _window_sums(load, ha, hb):
    sa = load(-ha)
    for o in range(-ha + 1, ha):
        sa = sa + load(o)
    sb = sa
    for o in list(range(-hb, -ha)) + list(range(ha, hb)):
        sb = sb + load(o)
    return sa, sb


def _count(pos, half, n):
    return (jnp.minimum(pos + half, n) - jnp.maximum(pos - half, 0)).astype(F32)


def _pool_halves():
    is_a = lax.broadcasted_iota(jnp.int32, (1, LANES), 1) < POOL_GROUP
    for half in range(D_POOL // LANES):
        wa, wb = POOL_WINDOWS[2 * half], POOL_WINDOWS[2 * half + 1]
        yield slice(half * LANES, (half + 1) * LANES), wa // 2, wb // 2, is_a


def _pool_ctx_kernel(v_ref, m_ref, cp):
    pad = POOL_PAD * BATCH
    tok = lax.broadcasted_iota(jnp.int32, (CTX_ROWS, 1), 0) // BATCH
    for lanes, ha, hb, is_a in _pool_halves():
        cp[...] = jnp.zeros_like(cp)
        cp[pad:pad + CTX_ROWS, :] = v_ref[:, lanes]
        ca, cb = _window_sums(lambda o: cp[pad + o * BATCH:pad + o * BATCH + CTX_ROWS, :], ha, hb)
        pooled = jnp.where(is_a, ca / _count(tok, ha, CTX_LEN), cb / _count(tok, hb, CTX_LEN))
        m_ref[:, lanes] = pooled - v_ref[:, lanes]


def _pool_ctx(vc):
    return pl.pallas_call(
        _pool_ctx_kernel,
        out_shape=jax.ShapeDtypeStruct((CTX_ROWS, D_POOL), F32),
        scratch_shapes=[pltpu.VMEM((CTX_ROWS + 2 * POOL_PAD * BATCH, LANES), F32)],
        name="pool_ctx",
    )(vc)


def _pool_kernel(vprev_ref, v_ref, vnext_ref, m_ref, vp, sp):
    i = pl.program_id(0)
    g0 = i * POOL_GROWS
    pad = POOL_PAD * BATCH
    col = lax.broadcasted_iota(jnp.int32, (GROW, 1), 0) // BATCH
    have_prev = i > 0
    have_next = i < pl.num_programs(0) - 1
    for lanes, ha, hb, is_a in _pool_halves():
        vp[...] = jnp.zeros_like(vp)

        def fill(src_ref, first, keep, lanes=lanes):
            def body(q, _):
                dst = pl.multiple_of((first + q) * PROW + pad, SUBLANES)
                src = pl.multiple_of(q * GROW, SUBLANES)
                vp[pl.ds(dst, GROW), :] = jnp.where(keep, src_ref[pl.ds(src, GROW), lanes], 0.0)
                return 0
            return body

        lax.fori_loop(0, POOL_PAD, fill(vprev_ref, 0, have_prev), 0)
        lax.fori_loop(0, POOL_GROWS, fill(v_ref, POOL_PAD, True), 0)
        lax.fori_loop(0, POOL_PAD, fill(vnext_ref, POOL_PAD + POOL_GROWS, have_next), 0)
        ccnt_a, ccnt_b = _count(col, ha, GRID_W), _count(col, hb, GRID_W)

        def body(r, _, lanes=lanes, ha=ha, hb=hb, is_a=is_a, ccnt_a=ccnt_a, ccnt_b=ccnt_b):
            base = (r + POOL_PAD) * PROW
            sa, sb = _window_sums(
                lambda o: vp[pl.ds(pl.multiple_of(base + o * PROW, SUBLANES), PROW), :], ha, hb)
            sp[...] = jnp.where(is_a, sa, sb)
            ba, bb = _window_sums(lambda o: sp[pad + o * BATCH:pad + o * BATCH + GROW, :], ha, hb)
            rcnt_a, rcnt_b = _count(g0 + r, ha, GRID_H), _count(g0 + r, hb, GRID_H)
            pooled = jnp.where(is_a, ba / (rcnt_a * ccnt_a), bb / (rcnt_b * ccnt_b))
            src = pl.multiple_of(r * GROW, SUBLANES)
            m_ref[pl.ds(src, GROW), lanes] = pooled - v_ref[pl.ds(src, GROW), lanes]
            return 0

        lax.fori_loop(0, POOL_GROWS, body, 0)


def _pool(v):
    n = SEQ_ROWS // POOL_TILE
    per = POOL_TILE // POOL_HALO
    nh = SEQ_ROWS // POOL_HALO
    return pl.pallas_call(
        _pool_kernel,
        grid=(n,),
        in_specs=[pl.BlockSpec((POOL_HALO, D_POOL), lambda i: (jnp.maximum(i * per - 1, 0), 0)),
                  pl.BlockSpec((POOL_TILE, D_POOL), lambda i: (i, 0)),
                  pl.BlockSpec((POOL_HALO, D_POOL), lambda i: (jnp.minimum((i + 1) * per, nh - 1), 0))],
        out_specs=pl.BlockSpec((POOL_TILE, D_POOL), lambda i: (i, 0)),
        out_shape=jax.ShapeDtypeStruct((SEQ_ROWS, D_POOL), F32),
        scratch_shapes=[pltpu.VMEM(((POOL_GROWS + 2 * POOL_PAD) * PROW, LANES), F32),
                        pltpu.VMEM((PROW, LANES), F32)],
        compiler_params=pltpu.CompilerParams(
            dimension_semantics=("parallel",), vmem_limit_bytes=VMEM_LIMIT),
        name="pool",
    )(v, v, v)


def _tail_kernel(final, *refs):
    if final:
        gate_ref, hf_ref, hb_ref, u_ref, ysf_ref, ysb_ref, m_ref, x_ref, mod_ref = refs[:9]
        rest = refs[9:]
    else:
        gate_ref, hf_ref, hb_ref, u_ref, ysf_ref, ysb_ref, m_ref, mc_ref, x_ref, mod_ref = refs[:10]
        rest = refs[10:]
    (dvec_ref, wglu_ref, bglu_ref, pw_ref, pb_ref, ps_ref, wo_ref,
     g2_ref, wgu_ref, wd_ref, fg_ref, o_ref) = rest[:12]
    dot = functools.partial(jnp.dot, preferred_element_type=F32)
    f32 = lambda ref: ref[...].astype(F32)

    y_a = jax.nn.gelu(f32(gate_ref)) * (f32(hf_ref) + f32(hb_ref))
    y_s = dvec_ref[...] * f32(u_ref) + f32(ysf_ref) + f32(ysb_ref)
    z = jax.nn.gelu(y_s)
    y_b = z * jax.nn.sigmoid(dot(z.astype(BF16), wglu_ref[...]) + bglu_ref[...])
    m = m_ref[...] if final else jnp.where(pl.program_id(0) < TAIL_CTX_TILES, mc_ref[...], m_ref[...])
    y_c = (dot(m.astype(BF16), pw_ref[...]) + pb_ref[...]) * ps_ref[...]
    y = jnp.concatenate([y_a.astype(BF16), y_b.astype(BF16), y_c.astype(BF16)], axis=1)
    x = x_ref[...] + _rows(mod_ref[2], TAIL_TILE) * dot(y, wo_ref[...])

    h = _rms(x) * g2_ref[...]
    h = (h * (1.0 + _rows(mod_ref[4], TAIL_TILE)) + _rows(mod_ref[3], TAIL_TILE)).astype(BF16)
    ff = None
    for c0, c1 in zip(FF_EDGES[:-1], FF_EDGES[1:]):
        cols = slice(c0, c1)
        ucols = slice(D_FF + c0, D_FF + c1)
        act = (jax.nn.silu(dot(h, wgu_ref[:, cols])) * dot(h, wgu_ref[:, ucols])).astype(BF16)
        part = dot(act, wd_ref[cols, :])
        ff = part if ff is None else ff + part
    y = x + _rows(mod_ref[5], TAIL_TILE) * ff
    if not final:
        o_ref[...] = y
        return
    y = _rms(y) * fg_ref[...]
    slab = rest[12]
    for s in range(D_MODEL // LANES):
        slab[s] = y[:, s * LANES:(s + 1) * LANES]
    for b in range(BATCH):
        for s in range(D_MODEL // LANES):
            o_ref[b, :, s * LANES:(s + 1) * LANES] = slab[s, pl.ds(b, TAIL_TILE // BATCH, stride=BATCH), :]


def _tail(final, gate, hf, hb, u, ysf, ysb, m, mc, xt, modtab, weights):
    off = TAIL_CTX_TILES if final else 0
    n_tiles = N_ROWS // TAIL_TILE - off
    tok = lambda w: pl.BlockSpec((TAIL_TILE, w), lambda i: (i + off, 0))
    acts = [gate, hf, hb, u, ysf, ysb, m]
    act_specs = [tok(D_RNN), tok(D_RNN), tok(D_RNN), tok(D_SSM), tok(D_SSM), tok(D_SSM),
                 pl.BlockSpec((TAIL_TILE, D_POOL), lambda i: (jnp.maximum(i + off - TAIL_CTX_TILES, 0), 0))]
    if final:
        toks = TAIL_TILE // BATCH
        out_spec = pl.BlockSpec((BATCH, toks, D_MODEL), lambda i: (0, i, 0))
        out_shape = jax.ShapeDtypeStruct((BATCH, n_tiles * toks, D_MODEL), F32)
        scratch = [pltpu.VMEM((D_MODEL // LANES, TAIL_TILE, LANES), F32)]
    else:
        acts.append(mc)
        act_specs.append(pl.BlockSpec((TAIL_TILE, D_POOL), lambda i: (jnp.minimum(i, TAIL_CTX_TILES - 1), 0)))
        out_spec = pl.BlockSpec((TAIL_TILE, D_MODEL), lambda i: (i, 0))
        out_shape = jax.ShapeDtypeStruct((N_ROWS, D_MODEL), F32)
        scratch = []
    return pl.pallas_call(
        functools.partial(_tail_kernel, final),
        grid=(n_tiles,),
        in_specs=(act_specs + [tok(D_MODEL), _mod_spec(off, TAIL_CTX_TILES)]
                  + [_const_spec(w.shape) for w in weights]),
        out_specs=out_spec,
        out_shape=out_shape,
        scratch_shapes=scratch,
        compiler_params=pltpu.CompilerParams(
            dimension_semantics=("parallel",), vmem_limit_bytes=VMEM_LIMIT),
        name="tail_final" if final else "tail",
    )(*acts, xt, modtab, *weights)


def _block_diag(w):
    n, di, dj = w.shape
    return jnp.einsum('nij,nm->nimj', w, jnp.eye(n, dtype=w.dtype)).reshape(n * di, n * dj)


def _s5_params(lam_re, lam_im, log_dt, b_re, b_im, c_re, c_im):
    gpb = SSM_GROUPS // S5_BLOCKS

    def in_blocks(w):
        w = jnp.swapaxes(w.astype(F32), 1, 2).reshape(S5_BLOCKS, gpb, SSM_GROUP, SSM_STATE)
        return jnp.stack([_block_diag(w[j]) for j in range(S5_BLOCKS)])

    def out_blocks(w):
        w = jnp.swapaxes(w, 1, 2).reshape(S5_BLOCKS, gpb, SSM_STATE, SSM_GROUP)
        return jnp.stack([_block_diag(w[j]) for j in range(S5_BLOCKS)])

    bmat = jnp.concatenate([in_blocks(b_re), in_blocks(b_im)], axis=2).astype(BF16)
    cmats, lams = [], []
    for d in range(2):
        lr, li = lam_re[d].astype(F32), lam_im[d].astype(F32)
        dt = jnp.exp(log_dt[d].astype(F32))[:, None]
        mag = jnp.exp(lr * dt)
        ang = li * dt
        bar_r, bar_i = mag * jnp.cos(ang), mag * jnp.sin(ang)
        den = lr * lr + li * li
        fr = ((bar_r - 1.0) * lr + bar_i * li) / den
        fi = (bar_i * lr - (bar_r - 1.0) * li) / den
        cr, ci = c_re[d].astype(F32), c_im[d].astype(F32)
        cfr = cr * fr[:, None, :] - ci * fi[:, None, :]
        cfi = cr * fi[:, None, :] + ci * fr[:, None, :]
        cmats.append(jnp.concatenate([out_blocks(cfr), out_blocks(-cfi)], axis=1))
        lams.append(jnp.stack([bar_r.reshape(-1), bar_i.reshape(-1)]))
    cmat = jnp.concatenate(cmats, axis=-1).astype(BF16)
    lam = jnp.concatenate([jnp.broadcast_to(lams[d][:, None, :], (2, HALF, N_STATE)) for d in range(2)],
                          axis=1)
    return bmat, cmat, lam


def kernel(x, c, ctx, c_ctx, w_mod, b_mod, norm1_g, norm2_g, w_in, w_out, lru_conv_w, lru_conv_b,
           lru_wa, lru_ba, lru_wi, lru_bi, lru_lambda, s5_lambda_re, s5_lambda_im, s5_log_dt,
           s5_b_re, s5_b_im, s5_c_re, s5_c_im, s5_d, s5_glu_w, s5_glu_b, pool_w, pool_b, pool_scale,
           ffn_w_gate, ffn_w_up, ffn_w_down, final_g):
    cc = jnp.zeros((SUBLANES, D_MODEL), F32).at[:BATCH].set(c).at[BATCH].set(c_ctx)
    mod = _modulation(cc, w_mod, b_mod)
    xt = (ctx, x)
    fg = final_g.reshape(1, D_MODEL)

    for l in range(DEPTH):
        last = l == DEPTH - 1
        lat = jnp.transpose(mod[l, :BATCH].reshape(BATCH, 6, D_MODEL), (1, 0, 2))
        lat = jnp.concatenate([lat, lat], axis=1)
        cm = jnp.broadcast_to(mod[l, BATCH].reshape(6, 1, D_MODEL), (6, SUBLANES, D_MODEL))
        modtab = jnp.stack([cm, lat])

        xt, ax, u, gate, v, vc = _inproj(xt, modtab, norm1_g[l].reshape(1, D_MODEL), w_in[l].astype(BF16))

        hpb = LANES // (D_RNN // lru_wa.shape[2])
        wg = jnp.stack([jnp.stack([
            jnp.concatenate([_block_diag(lru_wa[l, d, k * hpb:(k + 1) * hpb]),
                             _block_diag(lru_wi[l, d, k * hpb:(k + 1) * hpb])], axis=1)
            for k in range(D_RNN // LANES)]) for d in range(2)]).astype(BF16)
        bg = jnp.concatenate([lru_ba[l], lru_bi[l]], axis=-1).reshape(2, 1, 2 * D_RNN)
        spl = (8.0 * jax.nn.softplus(-lru_lambda[l].astype(F32))).reshape(2, 1, D_RNN)
        bmat, cmat, lam = _s5_params(s5_lambda_re[l], s5_lambda_im[l], s5_log_dt[l],
                                     s5_b_re[l], s5_b_im[l], s5_c_re[l], s5_c_im[l])
        hf, ysf, hb, ysb = _scan(ax, u, lru_conv_w[l], lru_conv_b[l].reshape(1, D_RNN),
                                 wg, bg, spl, bmat, cmat, lam)

        m = _pool(v)
        mc = None if last else _pool_ctx(vc)
        wgu = jnp.concatenate([ffn_w_gate[l], ffn_w_up[l]], axis=1).astype(BF16)
        weights = (s5_d[l].reshape(1, D_SSM), s5_glu_w[l].astype(BF16), s5_glu_b[l].reshape(1, D_SSM),
                   _block_diag(pool_w[l]).astype(BF16), pool_b[l].reshape(1, D_POOL),
                   pool_scale[l].reshape(1, D_POOL), w_out[l].astype(BF16),
                   norm2_g[l].reshape(1, D_MODEL), wgu, ffn_w_down[l].astype(BF16), fg)
        xt = _tail(last, gate, hf, hb, u, ysf, ysb, m, mc, xt, modtab, weights)
    return xt
```

```python
import functools

import jax
import jax.numpy as jnp
from jax import lax
from jax.experimental import pallas as pl
from jax.experimental.pallas import tpu as pltpu

F32 = jnp.float32
BF16 = jnp.bfloat16

D_MODEL = 1024
BATCH = 4
SEQ = 8192
DEPTH = 2
CTX_LEN = 256
GRID_W = 64
GRID_H = SEQ // GRID_W
EPS = 1e-6

D_RNN = 384
CONV_WIDTH = 4

D_SSM = 384
SSM_GROUP = 16
SSM_GROUPS = D_SSM // SSM_GROUP
SSM_STATE = 64
N_STATE = SSM_GROUPS * SSM_STATE

D_POOL = 256
POOL_WINDOWS = (2, 4, 8, 16)
POOL_GROUP = D_POOL // len(POOL_WINDOWS)
POOL_PAD = 8

D_IN = 2 * D_RNN + D_SSM + D_POOL
D_MIX = D_RNN + D_SSM + D_POOL
D_FF = 2816
MXU_DIM = 256
FF_EDGES = (0, 6 * MXU_DIM, D_FF)

LANES = 128
SUBLANES = 8
HALF = SUBLANES // 2
assert BATCH == HALF

N_TOK = CTX_LEN + SEQ
N_ROWS = N_TOK * BATCH
CTX_ROWS = CTX_LEN * BATCH
SEQ_ROWS = SEQ * BATCH

ROW_TILE = 1024
N_ROW_TILES = N_ROWS // ROW_TILE
assert CTX_ROWS == ROW_TILE
TAIL_TILE = 512
TAIL_CTX_TILES = CTX_ROWS // TAIL_TILE

SCAN_TOK = 128
SCAN_ROWS = SCAN_TOK * BATCH
SCAN_TILES = N_TOK // SCAN_TOK
SCAN_CTX_TILES = CTX_LEN // SCAN_TOK
SCAN_BLKS = SCAN_ROWS // SUBLANES
S5_BLOCKS = D_SSM // LANES
S5_BLK_STATES = N_STATE // S5_BLOCKS
SCAN_SEGS = 4

POOL_GROWS = 16
POOL_TILE = POOL_GROWS * GRID_W * BATCH
POOL_HALO = POOL_PAD * GRID_W * BATCH
GROW = GRID_W * BATCH
PROW = (GRID_W + 2 * POOL_PAD) * BATCH

VMEM_LIMIT = 56 * 1024 * 1024


def _const_spec(shape):
    nd = len(shape)
    return pl.BlockSpec(shape, lambda *_: (0,) * nd, pipeline_mode=pl.Buffered(1))


def _layer_spec(shape, layer):
    nd = len(shape) - 1
    return pl.BlockSpec((None,) + tuple(shape[1:]), lambda *_: (layer,) + (0,) * nd,
                        pipeline_mode=pl.Buffered(1))


def _rms(x):
    return x * lax.rsqrt(jnp.mean(x * x, axis=-1, keepdims=True) + EPS)


def _rows(pat, n):
    return jnp.broadcast_to(pat[None], (n // SUBLANES,) + pat.shape).reshape(n, pat.shape[-1])


def _mod_spec(off=0, ctx_tiles=1):
    return pl.BlockSpec((None, 6, SUBLANES, D_MODEL),
                        lambda i: (jnp.minimum((i + off) // ctx_tiles, 1), 0, 0, 0))


def _mod_kernel(c_ref, w_ref, b_ref, o_ref):
    s = jax.nn.silu(c_ref[...])
    o_ref[...] = jnp.dot(s.astype(BF16), w_ref[...].astype(BF16),
                         preferred_element_type=F32) + b_ref[...]


def _modulation(cc, w_mod, b_mod):
    n_col = 6 * D_MODEL
    blk = 1536
    return pl.pallas_call(
        _mod_kernel,
        grid=(DEPTH, n_col // blk),
        in_specs=[pl.BlockSpec((SUBLANES, D_MODEL), lambda l, j: (0, 0)),
                  pl.BlockSpec((None, D_MODEL, blk), lambda l, j: (l, 0, j)),
                  pl.BlockSpec((None, 1, blk), lambda l, j: (l, 0, j))],
        out_specs=pl.BlockSpec((None, SUBLANES, blk), lambda l, j: (l, 0, j)),
        out_shape=jax.ShapeDtypeStruct((DEPTH, SUBLANES, n_col), F32),
        compiler_params=pltpu.CompilerParams(vmem_limit_bytes=VMEM_LIMIT),
        name="modulation",
    )(cc, w_mod, b_mod.reshape(DEPTH, 1, n_col))


def _interleave(src_ref, slab):
    for b in range(BATCH):
        for s in range(D_MODEL // LANES):
            slab[s, pl.ds(b, ROW_TILE // BATCH, stride=BATCH), :] = src_ref[b, :, s * LANES:(s + 1) * LANES]


def _inproj_body(x, mod_ref, g_ref, w_ref, ax_ref, u_ref, gate_ref, v_ref, vc_ref):
    h = _rms(x) * g_ref[...]
    h = h * (1.0 + _rows(mod_ref[1], ROW_TILE)) + _rows(mod_ref[0], ROW_TILE)
    p = jnp.dot(h.astype(BF16), w_ref[...], preferred_element_type=F32)
    ax_ref[...] = p[:, 0:D_RNN]
    u_ref[...] = p[:, D_RNN:D_RNN + D_SSM].astype(BF16)
    gate_ref[...] = p[:, D_RNN + D_SSM:2 * D_RNN + D_SSM].astype(BF16)
    v = p[:, 2 * D_RNN + D_SSM:]
    v_ref[...] = v

    @pl.when(pl.program_id(0) == 0)
    def _():
        vc_ref[...] = v


def _inproj_kernel(x_ref, *rest):
    _inproj_body(x_ref[...], *rest)


def _inproj_first_kernel(ctx_ref, x_ref, mod_ref, g_ref, w_ref, xt_ref, *rest):
    slab = rest[-1]
    i = pl.program_id(0)

    @pl.when(i == 0)
    def _():
        _interleave(ctx_ref, slab)

    @pl.when(i > 0)
    def _():
        _interleave(x_ref, slab)

    x = jnp.concatenate([slab[s] for s in range(D_MODEL // LANES)], axis=1)
    xt_ref[...] = x
    _inproj_body(x, mod_ref, g_ref, w_ref, *rest[:-1])


def _inproj(xt, modtab, g, w_in, layer):
    first = isinstance(xt, tuple)
    tok = lambda w: pl.BlockSpec((ROW_TILE, w), lambda i: (i, 0))
    out = lambda w, dt=F32: jax.ShapeDtypeStruct((N_ROWS, w), dt)
    out_specs = [tok(D_RNN), tok(D_SSM), tok(D_RNN),
                 pl.BlockSpec((ROW_TILE, D_POOL), lambda i: (jnp.maximum(i - 1, 0), 0)),
                 pl.BlockSpec((CTX_ROWS, D_POOL), lambda i: (0, 0))]
    out_shape = [out(D_RNN), out(D_SSM, BF16), out(D_RNN, BF16),
                 jax.ShapeDtypeStruct((SEQ_ROWS, D_POOL), F32),
                 jax.ShapeDtypeStruct((CTX_ROWS, D_POOL), F32)]
    common = [_mod_spec(), _const_spec((1, D_MODEL)), _layer_spec(w_in.shape, layer)]
    params = pltpu.CompilerParams(dimension_semantics=("arbitrary",), vmem_limit_bytes=VMEM_LIMIT)
    if not first:
        return (xt,) + tuple(pl.pallas_call(
            _inproj_kernel, grid=(N_ROW_TILES,), in_specs=[tok(D_MODEL)] + common,
            out_specs=out_specs, out_shape=out_shape, compiler_params=params, name="inproj",
        )(xt, modtab, g, w_in))
    toks = ROW_TILE // BATCH
    src = [pl.BlockSpec((BATCH, CTX_LEN, D_MODEL), lambda i: (0, 0, 0), pipeline_mode=pl.Buffered(1)),
           pl.BlockSpec((BATCH, toks, D_MODEL), lambda i: (0, jnp.maximum(i - 1, 0), 0))]
    return tuple(pl.pallas_call(
        _inproj_first_kernel, grid=(N_ROW_TILES,), in_specs=src + common,
        out_specs=[tok(D_MODEL)] + out_specs, out_shape=[out(D_MODEL)] + out_shape,
        scratch_shapes=[pltpu.VMEM((D_MODEL // LANES, ROW_TILE, LANES), F32)],
        compiler_params=params, name="inproj_first",
    )(xt[0], xt[1], modtab, g, w_in))


def _fwd_tile(i):
    return i


def _bwd_tile(i):
    return jnp.where(i < SCAN_CTX_TILES, SCAN_CTX_TILES - 1 - i, SCAN_TILES + SCAN_CTX_TILES - 1 - i)


def _scan_kernel(axp_f, ax_f, axn_f, u_f, axp_b, ax_b, axn_b, u_b,
                 cw_ref, cb_ref, wg_ref, bg_ref, spl_ref, bmat_ref, cmat_ref, lam_ref,
                 hf_ref, ysf_ref, hb_ref, ysb_ref,
                 axs_f, axs_b, a_f, b_f, a_b, b_b, uf32, ub32, us1, us2, xs1, xs2, ys1, ys2,
                 hfs, hbs, yfs, ybs, lcar, scar):
    i = pl.program_id(0)
    uf32[...] = u_f[...].astype(F32)
    ub32[...] = u_b[...].astype(F32)

    @pl.when(i == 0)
    def _():
        lcar[...] = jnp.zeros_like(lcar)
        scar[...] = jnp.zeros_like(scar)

    def stage_conv(tile, axp, ax, axn, axs):
        prev_ok = jnp.logical_and(tile != 0, tile != SCAN_CTX_TILES)
        next_ok = jnp.logical_and(tile != SCAN_CTX_TILES - 1, tile != SCAN_TILES - 1)
        axs[0:SUBLANES, :] = jnp.where(prev_ok, axp[...], 0.0)
        axs[SUBLANES:SUBLANES + SCAN_ROWS, :] = ax[...]
        axs[SUBLANES + SCAN_ROWS:, :] = jnp.where(next_ok, axn[...], 0.0)

    def gates(d, k, axs, a_out, b_out):
        lanes = slice(k * LANES, (k + 1) * LANES)
        xc = cb_ref[:, lanes] + axs[0:SCAN_ROWS, lanes] * cw_ref[0:1, lanes]
        for t in range(1, CONV_WIDTH):
            xc = xc + axs[t * BATCH:t * BATCH + SCAN_ROWS, lanes] * cw_ref[t:t + 1, lanes]
        g = jnp.dot(xc.astype(BF16), wg_ref[d, k], preferred_element_type=F32)
        r = jax.nn.sigmoid(g[:, :LANES] + bg_ref[d, :, lanes])
        ig = jax.nn.sigmoid(g[:, LANES:] + bg_ref[d, :, D_RNN + k * LANES:D_RNN + (k + 1) * LANES])
        a = jnp.exp(-(r * spl_ref[d, :, lanes]))
        a_out[:, lanes] = a
        b_out[:, lanes] = jnp.sqrt(1.0 - a * a) * (ig * xc)

    lo = lax.broadcasted_iota(jnp.int32, (SUBLANES, 1), 0) < HALF

    def swap(x):
        return pltpu.roll(x, HALF, 0)

    def rows_of(j):
        jb = SCAN_BLKS - 1 - j
        return slice(j * SUBLANES, (j + 1) * SUBLANES), slice(jb * SUBLANES, (jb + 1) * SUBLANES)

    def steps(vf, vb):
        return jnp.where(lo, vf, vb), swap(jnp.where(lo, vb, vf))

    def unsteps(h1, h2):
        h2s = swap(h2)
        return jnp.where(lo, h1, h2s), jnp.where(lo, h2s, h1)

    def lru_scan(h, j0, j1):
        for j in range(j0, j1):
            rf, rb = rows_of(j)
            a1, a2 = steps(a_f[rf, :], a_b[rb, :])
            b1, b2 = steps(b_f[rf, :], b_b[rb, :])
            h1 = a1 * h + b1
            h = a2 * h1 + b2
            hfs[rf, :], hbs[rb, :] = unsteps(h1, h)
        return h

    for j in range(SCAN_BLKS):
        rf, rb = rows_of(j)
        us1[rf, :], us2[rf, :] = steps(uf32[rf, :], ub32[rb, :])
    stage_conv(_fwd_tile(i), axp_f, ax_f, axn_f, axs_f)
    stage_conv(_bwd_tile(i), axp_b, ax_b, axn_b, axs_b)

    nst = S5_BLK_STATES

    def s5_scan(jb, carry, j0, j1):
        x1, x2 = xs1.at[jb], xs2.at[jb]
        re, im = slice(0, nst), slice(nst, 2 * nst)
        lr, li = lam_ref[0, :, jb * nst:(jb + 1) * nst], lam_ref[1, :, jb * nst:(jb + 1) * nst]
        sr, si = carry
        for j in range(j0, j1):
            rows = slice(j * SUBLANES, (j + 1) * SUBLANES)
            r1 = lr * sr - li * si + x1[rows, re]
            i1 = lr * si + li * sr + x1[rows, im]
            sr = lr * r1 - li * i1 + x2[rows, re]
            si = lr * i1 + li * r1 + x2[rows, im]
            x1[rows, re] = r1
            x1[rows, im] = i1
            x2[rows, re] = sr
            x2[rows, im] = si
        return sr, si

    def out_proj(jb, piece):
        xs, ys = ((xs1, ys1), (xs2, ys2))[piece // 2]
        half = slice(0, nst) if piece % 2 == 0 else slice(nst, 2 * nst)
        cols = slice(2 * jb * LANES, 2 * (jb + 1) * LANES)
        y = jnp.dot(xs[jb, :, half].astype(BF16), cmat_ref[jb, half, :], preferred_element_type=F32)
        ys[:, cols] = y if piece % 2 == 0 else ys[:, cols] + y

    for jb in range(S5_BLOCKS):
        blk = slice(jb * LANES, (jb + 1) * LANES)
        gates(0, jb, axs_f, a_f, b_f)
        xs1[jb] = jnp.dot(us1[:, blk].astype(BF16), bmat_ref[jb], preferred_element_type=F32)
        gates(1, jb, axs_b, a_b, b_b)
        xs2[jb] = jnp.dot(us2[:, blk].astype(BF16), bmat_ref[jb], preferred_element_type=F32)
    seg = SCAN_BLKS // SCAN_SEGS
    for jb in range(S5_BLOCKS):
        carry = (scar[:, jb * nst:(jb + 1) * nst], scar[:, N_STATE + jb * nst:N_STATE + (jb + 1) * nst])
        for q in range(SCAN_SEGS):
            carry = s5_scan(jb, carry, q * seg, (q + 1) * seg)
            if jb > 0:
                out_proj(jb - 1, q)
        scar[:, jb * nst:(jb + 1) * nst] = carry[0]
        scar[:, N_STATE + jb * nst:N_STATE + (jb + 1) * nst] = carry[1]
    h = lcar[...]
    for q in range(SCAN_SEGS):
        h = lru_scan(h, q * seg, (q + 1) * seg)
        out_proj(S5_BLOCKS - 1, q)
    lcar[...] = h
    hf_ref[...] = hfs[...].astype(BF16)
    hb_ref[...] = hbs[...].astype(BF16)

    for j in range(SCAN_BLKS):
        rf, rb = rows_of(j)
        of, ob = unsteps(ys1[rf, :], ys2[rf, :])
        for jb in range(S5_BLOCKS):
            yfs[rf, jb * LANES:(jb + 1) * LANES] = of[:, 2 * jb * LANES:(2 * jb + 1) * LANES]
            ybs[rb, jb * LANES:(jb + 1) * LANES] = ob[:, (2 * jb + 1) * LANES:2 * (jb + 1) * LANES]
    ysf_ref[...] = yfs[...].astype(BF16)
    ysb_ref[...] = ybs[...].astype(BF16)


def _scan(ax, u, cw, cb, wg, bg, spl, bmat, cmat, lam):
    tpb = SCAN_ROWS // SUBLANES
    nb8 = N_ROWS // SUBLANES

    def specs(tile_of):
        tok = lambda w: pl.BlockSpec((SCAN_ROWS, w), lambda i: (tile_of(i), 0))
        prev = pl.BlockSpec((SUBLANES, D_RNN), lambda i: (jnp.maximum(tile_of(i) * tpb - 1, 0), 0))
        nxt = pl.BlockSpec((SUBLANES, D_RNN), lambda i: (jnp.minimum((tile_of(i) + 1) * tpb, nb8 - 1), 0))
        return tok, prev, nxt

    tok_f, prev_f, next_f = specs(_fwd_tile)
    tok_b, prev_b, next_b = specs(_bwd_tile)
    out = jax.ShapeDtypeStruct((N_ROWS, D_RNN), BF16)
    tile_f32 = lambda w: pltpu.VMEM((SCAN_ROWS, w), F32)
    return pl.pallas_call(
        _scan_kernel,
        grid=(SCAN_TILES,),
        in_specs=[prev_f, tok_f(D_RNN), next_f, tok_f(D_SSM),
                  prev_b, tok_b(D_RNN), next_b, tok_b(D_SSM),
                  _const_spec((CONV_WIDTH, D_RNN)), _const_spec((1, D_RNN)),
                  _const_spec((2, D_RNN // LANES, LANES, 2 * LANES)), _const_spec((2, 1, 2 * D_RNN)),
                  _const_spec((2, 1, D_RNN)),
                  _const_spec((S5_BLOCKS, LANES, 2 * S5_BLK_STATES)),
                  _const_spec((S5_BLOCKS, 2 * S5_BLK_STATES, 2 * LANES)),
                  _const_spec((2, SUBLANES, N_STATE))],
        out_specs=[tok_f(D_RNN), tok_f(D_SSM), tok_b(D_RNN), tok_b(D_SSM)],
        out_shape=[out, out, out, out],
        scratch_shapes=[pltpu.VMEM((SCAN_ROWS + 2 * SUBLANES, D_RNN), F32),
                        pltpu.VMEM((SCAN_ROWS + 2 * SUBLANES, D_RNN), F32),
                        tile_f32(D_RNN), tile_f32(D_RNN), tile_f32(D_RNN), tile_f32(D_RNN),
                        tile_f32(D_SSM), tile_f32(D_SSM), tile_f32(D_SSM), tile_f32(D_SSM),
                        pltpu.VMEM((S5_BLOCKS, SCAN_ROWS, 2 * S5_BLK_STATES), F32),
                        pltpu.VMEM((S5_BLOCKS, SCAN_ROWS, 2 * S5_BLK_STATES), F32),
                        tile_f32(2 * D_SSM), tile_f32(2 * D_SSM),
                        tile_f32(D_RNN), tile_f32(D_RNN), tile_f32(D_SSM), tile_f32(D_SSM),
                        pltpu.VMEM((SUBLANES, D_RNN), F32),
                        pltpu.VMEM((SUBLANES, 2 * N_STATE), F32)],
        compiler_params=pltpu.CompilerParams(
            dimension_semantics=("arbitrary",), vmem_limit_bytes=VMEM_LIMIT),
        name="scan",
    )(ax, ax, ax, u, ax, ax, ax, u, cw, cb, wg, bg, spl, bmat, cmat, lam)


def _window_sums(load, ha, hb):
    sa = load(-ha)
    for o in range(-ha + 1, ha):
        sa = sa + load(o)
    sb = sa
    for o in list(range(-hb, -ha)) + list(range(ha, hb)):
        sb = sb + load(o)
    return sa, sb


def _count(pos, half, n):
    return (jnp.minimum(pos + half, n) - jnp.maximum(pos - half, 0)).astype(F32)


def _pool_halves():
    is_a = lax.broadcasted_iota(jnp.int32, (1, LANES), 1) < POOL_GROUP
    for half in range(D_POOL // LANES):
        wa, wb = POOL_WINDOWS[2 * half], POOL_WINDOWS[2 * half + 1]
        yield slice(half * LANES, (half + 1) * LANES), wa // 2, wb // 2, is_a


def _pool_ctx_kernel(v_ref, m_ref, cp):
    pad = POOL_PAD * BATCH
    tok = lax.broadcasted_iota(jnp.int32, (CTX_ROWS, 1), 0) // BATCH
    for lanes, ha, hb, is_a in _pool_halves():
        cp[...] = jnp.zeros_like(cp)
        cp[pad:pad + CTX_ROWS, :] = v_ref[:, lanes]
        ca, cb = _window_sums(lambda o: cp[pad + o * BATCH:pad + o * BATCH + CTX_ROWS, :], ha, hb)
        pooled = jnp.where(is_a, ca / _count(tok, ha, CTX_LEN), cb / _count(tok, hb, CTX_LEN))
        m_ref[:, lanes] = pooled - v_ref[:, lanes]


def _pool_ctx(vc):
    return pl.pallas_call(
        _pool_ctx_kernel,
        out_shape=jax.ShapeDtypeStruct((CTX_ROWS, D_POOL), F32),
        scratch_shapes=[pltpu.VMEM((CTX_ROWS + 2 * POOL_PAD * BATCH, LANES), F32)],
        name="pool_ctx",
    )(vc)


def _pool_kernel(vprev_ref, v_ref, vnext_ref, m_ref, vp, sp):
    i = pl.program_id(0)
    g0 = i * POOL_GROWS
    pad = POOL_PAD * BATCH
    col = lax.broadcasted_iota(jnp.int32, (GROW, 1), 0) // BATCH
    have_prev = i > 0
    have_next = i < pl.num_programs(0) - 1
    for lanes, ha, hb, is_a in _pool_halves():
        vp[...] = jnp.zeros_like(vp)

        def fill(src_ref, first, keep, lanes=lanes):
            def body(q, _):
                dst = pl.multiple_of((first + q) * PROW + pad, SUBLANES)
                src = pl.multiple_of(q * GROW, SUBLANES)
                vp[pl.ds(dst, GROW), :] = jnp.where(keep, src_ref[pl.ds(src, GROW), lanes], 0.0)
                return 0
            return body

        lax.fori_loop(0, POOL_PAD, fill(vprev_ref, 0, have_prev), 0)
        lax.fori_loop(0, POOL_GROWS, fill(v_ref, POOL_PAD, True), 0)
        lax.fori_loop(0, POOL_PAD, fill(vnext_ref, POOL_PAD + POOL_GROWS, have_next), 0)
        ccnt_a, ccnt_b = _count(col, ha, GRID_W), _count(col, hb, GRID_W)

        def body(r, _, lanes=lanes, ha=ha, hb=hb, is_a=is_a, ccnt_a=ccnt_a, ccnt_b=ccnt_b):
            base = (r + POOL_PAD) * PROW
            sa, sb = _window_sums(
                lambda o: vp[pl.ds(pl.multiple_of(base + o * PROW, SUBLANES), PROW), :], ha, hb)
            sp[...] = jnp.where(is_a, sa, sb)
            ba, bb = _window_sums(lambda o: sp[pad + o * BATCH:pad + o * BATCH + GROW, :], ha, hb)
            rcnt_a, rcnt_b = _count(g0 + r, ha, GRID_H), _count(g0 + r, hb, GRID_H)
            pooled = jnp.where(is_a, ba / (rcnt_a * ccnt_a), bb / (rcnt_b * ccnt_b))
            src = pl.multiple_of(r * GROW, SUBLANES)
            m_ref[pl.ds(src, GROW), lanes] = pooled - v_ref[pl.ds(src, GROW), lanes]
            return 0

        lax.fori_loop(0, POOL_GROWS, body, 0)


def _pool(v):
    n = SEQ_ROWS // POOL_TILE
    per = POOL_TILE // POOL_HALO
    nh = SEQ_ROWS // POOL_HALO
    return pl.pallas_call(
        _pool_kernel,
        grid=(n,),
        in_specs=[pl.BlockSpec((POOL_HALO, D_POOL), lambda i: (jnp.maximum(i * per - 1, 0), 0)),
                  pl.BlockSpec((POOL_TILE, D_POOL), lambda i: (i, 0)),
                  pl.BlockSpec((POOL_HALO, D_POOL), lambda i: (jnp.minimum((i + 1) * per, nh - 1), 0))],
        out_specs=pl.BlockSpec((POOL_TILE, D_POOL), lambda i: (i, 0)),
        out_shape=jax.ShapeDtypeStruct((SEQ_ROWS, D_POOL), F32),
        scratch_shapes=[pltpu.VMEM(((POOL_GROWS + 2 * POOL_PAD) * PROW, LANES), F32),
                        pltpu.VMEM((PROW, LANES), F32)],
        compiler_params=pltpu.CompilerParams(
            dimension_semantics=("parallel",), vmem_limit_bytes=VMEM_LIMIT),
        name="pool",
    )(v, v, v)


def _tail_kernel(final, *refs):
    if final:
        gate_ref, hf_ref, hb_ref, u_ref, ysf_ref, ysb_ref, m_ref, x_ref, mod_ref = refs[:9]
        rest = refs[9:]
    else:
        gate_ref, hf_ref, hb_ref, u_ref, ysf_ref, ysb_ref, m_ref, mc_ref, x_ref, mod_ref = refs[:10]
        rest = refs[10:]
    (dvec_ref, wglu_ref, bglu_ref, pw_ref, pb_ref, ps_ref, wo_ref,
     g2_ref, wg_ref, wu_ref, wd_ref, fg_ref, o_ref) = rest[:13]
    dot = functools.partial(jnp.dot, preferred_element_type=F32)
    f32 = lambda ref: ref[...].astype(F32)

    y_a = jax.nn.gelu(f32(gate_ref)) * (f32(hf_ref) + f32(hb_ref))
    y_s = dvec_ref[...] * f32(u_ref) + f32(ysf_ref) + f32(ysb_ref)
    z = jax.nn.gelu(y_s)
    y_b = z * jax.nn.sigmoid(dot(z.astype(BF16), wglu_ref[...]) + bglu_ref[...])
    m = m_ref[...] if final else jnp.where(pl.program_id(0) < TAIL_CTX_TILES, mc_ref[...], m_ref[...])
    y_c = (dot(m.astype(BF16), pw_ref[...]) + pb_ref[...]) * ps_ref[...]
    y = jnp.concatenate([y_a.astype(BF16), y_b.astype(BF16), y_c.astype(BF16)], axis=1)
    x = x_ref[...] + _rows(mod_ref[2], TAIL_TILE) * dot(y, wo_ref[...])

    h = _rms(x) * g2_ref[...]
    h = (h * (1.0 + _rows(mod_ref[4], TAIL_TILE)) + _rows(mod_ref[3], TAIL_TILE)).astype(BF16)
    ff = None
    for c0, c1 in zip(FF_EDGES[:-1], FF_EDGES[1:]):
        cols = slice(c0, c1)
        act = (jax.nn.silu(dot(h, wg_ref[:, cols])) * dot(h, wu_ref[:, cols])).astype(BF16)
        part = dot(act, wd_ref[cols, :])
        ff = part if ff is None else ff + part
    y = x + _rows(mod_ref[5], TAIL_TILE) * ff
    if not final:
        o_ref[...] = y
        return
    y = _rms(y) * fg_ref[...]
    slab = rest[13]
    for s in range(D_MODEL // LANES):
        slab[s] = y[:, s * LANES:(s + 1) * LANES]
    for b in range(BATCH):
        for s in range(D_MODEL // LANES):
            o_ref[b, :, s * LANES:(s + 1) * LANES] = slab[s, pl.ds(b, TAIL_TILE // BATCH, stride=BATCH), :]


def _tail(final, layer, gate, hf, hb, u, ysf, ysb, m, mc, xt, modtab, weights):
    off = TAIL_CTX_TILES if final else 0
    n_tiles = N_ROWS // TAIL_TILE - off
    tok = lambda w: pl.BlockSpec((TAIL_TILE, w), lambda i: (i + off, 0))
    acts = [gate, hf, hb, u, ysf, ysb, m]
    act_specs = [tok(D_RNN), tok(D_RNN), tok(D_RNN), tok(D_SSM), tok(D_SSM), tok(D_SSM),
                 pl.BlockSpec((TAIL_TILE, D_POOL), lambda i: (jnp.maximum(i + off - TAIL_CTX_TILES, 0), 0))]
    if final:
        toks = TAIL_TILE // BATCH
        out_spec = pl.BlockSpec((BATCH, toks, D_MODEL), lambda i: (0, i, 0))
        out_shape = jax.ShapeDtypeStruct((BATCH, n_tiles * toks, D_MODEL), F32)
        scratch = [pltpu.VMEM((D_MODEL // LANES, TAIL_TILE, LANES), F32)]
    else:
        acts.append(mc)
        act_specs.append(pl.BlockSpec((TAIL_TILE, D_POOL), lambda i: (jnp.minimum(i, TAIL_CTX_TILES - 1), 0)))
        out_spec = pl.BlockSpec((TAIL_TILE, D_MODEL), lambda i: (i, 0))
        out_shape = jax.ShapeDtypeStruct((N_ROWS, D_MODEL), F32)
        scratch = []
    return pl.pallas_call(
        functools.partial(_tail_kernel, final),
        grid=(n_tiles,),
        in_specs=(act_specs + [tok(D_MODEL), _mod_spec(off, TAIL_CTX_TILES)]
                  + [_const_spec(w.shape) if w.ndim == 2 else _layer_spec(w.shape, layer) for w in weights]),
        out_specs=out_spec,
        out_shape=out_shape,
        scratch_shapes=scratch,
        compiler_params=pltpu.CompilerParams(
            dimension_semantics=("parallel",), vmem_limit_bytes=VMEM_LIMIT),
        name="tail_final" if final else "tail",
    )(*acts, xt, modtab, *weights)


def _block_diag(w):
    n, di, dj = w.shape
    return jnp.einsum('nij,nm->nimj', w, jnp.eye(n, dtype=w.dtype)).reshape(n * di, n * dj)


def _s5_params(lam_re, lam_im, log_dt, b_re, b_im, c_re, c_im):
    gpb = SSM_GROUPS // S5_BLOCKS

    def in_blocks(w):
        w = jnp.swapaxes(w.astype(F32), 1, 2).reshape(S5_BLOCKS, gpb, SSM_GROUP, SSM_STATE)
        return jnp.stack([_block_diag(w[j]) for j in range(S5_BLOCKS)])

    def out_blocks(w):
        w = jnp.swapaxes(w, 1, 2).reshape(S5_BLOCKS, gpb, SSM_STATE, SSM_GROUP)
        return jnp.stack([_block_diag(w[j]) for j in range(S5_BLOCKS)])

    bmat = jnp.concatenate([in_blocks(b_re), in_blocks(b_im)], axis=2).astype(BF16)
    cmats, lams = [], []
    for d in range(2):
        lr, li = lam_re[d].astype(F32), lam_im[d].astype(F32)
        dt = jnp.exp(log_dt[d].astype(F32))[:, None]
        mag = jnp.exp(lr * dt)
        ang = li * dt
        bar_r, bar_i = mag * jnp.cos(ang), mag * jnp.sin(ang)
        den = lr * lr + li * li
        fr = ((bar_r - 1.0) * lr + bar_i * li) / den
        fi = (bar_i * lr - (bar_r - 1.0) * li) / den
        cr, ci = c_re[d].astype(F32), c_im[d].astype(F32)
        cfr = cr * fr[:, None, :] - ci * fi[:, None, :]
        cfi = cr * fi[:, None, :] + ci * fr[:, None, :]
        cmats.append(jnp.concatenate([out_blocks(cfr), out_blocks(-cfi)], axis=1))
        lams.append(jnp.stack([bar_r.reshape(-1), bar_i.reshape(-1)]))
    cmat = jnp.concatenate(cmats, axis=-1).astype(BF16)
    lam = jnp.concatenate([jnp.broadcast_to(lams[d][:, None, :], (2, HALF, N_STATE)) for d in range(2)],
                          axis=1)
    return bmat, cmat, lam


def kernel(x, c, ctx, c_ctx, w_mod, b_mod, norm1_g, norm2_g, w_in, w_out, lru_conv_w, lru_conv_b,
           lru_wa, lru_ba, lru_wi, lru_bi, lru_lambda, s5_lambda_re, s5_lambda_im, s5_log_dt,
           s5_b_re, s5_b_im, s5_c_re, s5_c_im, s5_d, s5_glu_w, s5_glu_b, pool_w, pool_b, pool_scale,
           ffn_w_gate, ffn_w_up, ffn_w_down, final_g):
    cc = jnp.zeros((SUBLANES, D_MODEL), F32).at[:BATCH].set(c).at[BATCH].set(c_ctx)
    mod = _modulation(cc, w_mod, b_mod)
    xt = (ctx, x)
    fg = final_g.reshape(1, D_MODEL)
    w_in_b, w_out_b, glu_b = w_in.astype(BF16), w_out.astype(BF16), s5_glu_w.astype(BF16)
    ffn_gate_b, ffn_up_b, ffn_down_b = (w.astype(BF16) for w in (ffn_w_gate, ffn_w_up, ffn_w_down))

    for l in range(DEPTH):
        last = l == DEPTH - 1
        lat = jnp.transpose(mod[l, :BATCH].reshape(BATCH, 6, D_MODEL), (1, 0, 2))
        lat = jnp.concatenate([lat, lat], axis=1)
        cm = jnp.broadcast_to(mod[l, BATCH].reshape(6, 1, D_MODEL), (6, SUBLANES, D_MODEL))
        modtab = jnp.stack([cm, lat])

        xt, ax, u, gate, v, vc = _inproj(xt, modtab, norm1_g[l].reshape(1, D_MODEL), w_in_b, l)

        hpb = LANES // (D_RNN // lru_wa.shape[2])
        wg = jnp.stack([jnp.stack([
            jnp.concatenate([_block_diag(lru_wa[l, d, k * hpb:(k + 1) * hpb]),
                             _block_diag(lru_wi[l, d, k * hpb:(k + 1) * hpb])], axis=1)
            for k in range(D_RNN // LANES)]) for d in range(2)]).astype(BF16)
        bg = jnp.concatenate([lru_ba[l], lru_bi[l]], axis=-1).reshape(2, 1, 2 * D_RNN)
        spl = (8.0 * jax.nn.softplus(-lru_lambda[l].astype(F32))).reshape(2, 1, D_RNN)
        bmat, cmat, lam = _s5_params(s5_lambda_re[l], s5_lambda_im[l], s5_log_dt[l],
                                     s5_b_re[l], s5_b_im[l], s5_c_re[l], s5_c_im[l])
        hf, ysf, hb, ysb = _scan(ax, u, lru_conv_w[l], lru_conv_b[l].reshape(1, D_RNN),
                                 wg, bg, spl, bmat, cmat, lam)

        m = _pool(v)
        mc = None if last else _pool_ctx(vc)
        weights = (s5_d[l].reshape(1, D_SSM), glu_b, s5_glu_b[l].reshape(1, D_SSM),
                   _block_diag(pool_w[l]).astype(BF16), pool_b[l].reshape(1, D_POOL),
                   pool_scale[l].reshape(1, D_POOL), w_out_b,
                   norm2_g[l].reshape(1, D_MODEL), ffn_gate_b, ffn_up_b, ffn_down_b, fg)
        xt = _tail(last, l, gate, hf, hb, u, ysf, ysb, m, mc, xt, modtab, weights)
    return xt
```

```python
import functools

import jax
import jax.numpy as jnp
from jax import lax
from jax.experimental import pallas as pl
from jax.experimental.pallas import tpu as pltpu

F32 = jnp.float32
BF16 = jnp.bfloat16

D_MODEL = 1024
BATCH = 4
SEQ = 8192
DEPTH = 2
CTX_LEN = 256
GRID_W = 64
GRID_H = SEQ // GRID_W
EPS = 1e-6

D_RNN = 384
CONV_WIDTH = 4

D_SSM = 384
SSM_GROUP = 16
SSM_GROUPS = D_SSM // SSM_GROUP
SSM_STATE = 64
N_STATE = SSM_GROUPS * SSM_STATE

D_POOL = 256
POOL_WINDOWS = (2, 4, 8, 16)
POOL_GROUP = D_POOL // len(POOL_WINDOWS)
POOL_PAD = 8

D_IN = 2 * D_RNN + D_SSM + D_POOL
D_MIX = D_RNN + D_SSM + D_POOL
D_FF = 2816
MXU_DIM = 256
FF_EDGES = (0, 6 * MXU_DIM, D_FF)

LANES = 128
SUBLANES = 8
HALF = SUBLANES // 2
assert BATCH == HALF

N_TOK = CTX_LEN + SEQ
N_ROWS = N_TOK * BATCH
CTX_ROWS = CTX_LEN * BATCH
SEQ_ROWS = SEQ * BATCH

ROW_TILE = 1024
N_ROW_TILES = N_ROWS // ROW_TILE
assert CTX_ROWS == ROW_TILE
TAIL_TILE = 512
TAIL_CTX_TILES = CTX_ROWS // TAIL_TILE

SCAN_TOK = 128
SCAN_ROWS = SCAN_TOK * BATCH
SCAN_TILES = N_TOK // SCAN_TOK
SCAN_CTX_TILES = CTX_LEN // SCAN_TOK
SCAN_BLKS = SCAN_ROWS // SUBLANES
S5_BLOCKS = D_SSM // LANES
S5_BLK_STATES = N_STATE // S5_BLOCKS
SCAN_SEGS = 4

POOL_GROWS = 16
POOL_TILE = POOL_GROWS * GRID_W * BATCH
POOL_HALO = POOL_PAD * GRID_W * BATCH
GROW = GRID_W * BATCH
PROW = (GRID_W + 2 * POOL_PAD) * BATCH

VMEM_LIMIT = 56 * 1024 * 1024


def _const_spec(shape):
    nd = len(shape)
    return pl.BlockSpec(shape, lambda *_: (0,) * nd, pipeline_mode=pl.Buffered(1))


def _layer_spec(shape, layer):
    nd = len(shape) - 1
    return pl.BlockSpec((None,) + tuple(shape[1:]), lambda *_: (layer,) + (0,) * nd,
                        pipeline_mode=pl.Buffered(1))


def _rms(x):
    return x * lax.rsqrt(jnp.mean(x * x, axis=-1, keepdims=True) + EPS)


def _rows(pat, n):
    return jnp.broadcast_to(pat[None], (n // SUBLANES,) + pat.shape).reshape(n, pat.shape[-1])


def _mod_spec(off=0, ctx_tiles=1):
    return pl.BlockSpec((None, 6, SUBLANES, D_MODEL),
                        lambda i: (jnp.minimum((i + off) // ctx_tiles, 1), 0, 0, 0))


def _mod_kernel(c_ref, w_ref, b_ref, o_ref):
    s = jax.nn.silu(c_ref[...])
    o_ref[...] = jnp.dot(s.astype(BF16), w_ref[...].astype(BF16),
                         preferred_element_type=F32) + b_ref[...]


def _modulation(cc, w_mod, b_mod):
    n_col = 6 * D_MODEL
    blk = 1536
    return pl.pallas_call(
        _mod_kernel,
        grid=(DEPTH, n_col // blk),
        in_specs=[pl.BlockSpec((SUBLANES, D_MODEL), lambda l, j: (0, 0)),
                  pl.BlockSpec((None, D_MODEL, blk), lambda l, j: (l, 0, j)),
                  pl.BlockSpec((None, 1, blk), lambda l, j: (l, 0, j))],
        out_specs=pl.BlockSpec((None, SUBLANES, blk), lambda l, j: (l, 0, j)),
        out_shape=jax.ShapeDtypeStruct((DEPTH, SUBLANES, n_col), F32),
        compiler_params=pltpu.CompilerParams(vmem_limit_bytes=VMEM_LIMIT),
        name="modulation",
    )(cc, w_mod, b_mod.reshape(DEPTH, 1, n_col))


def _interleave(src_ref, slab):
    toks = src_ref.shape[1]
    for b in range(BATCH):
        for s in range(D_MODEL // LANES):
            slab[s, pl.ds(b, toks, stride=BATCH), :] = src_ref[b, :, s * LANES:(s + 1) * LANES]


def _time_major(ctx_ref, x_ref, slab, ctx_tiles):
    i = pl.program_id(0)

    @pl.when(i < ctx_tiles)
    def _():
        _interleave(ctx_ref, slab)

    @pl.when(i >= ctx_tiles)
    def _():
        _interleave(x_ref, slab)

    return jnp.concatenate([slab[s] for s in range(D_MODEL // LANES)], axis=1)


def _batch_major_specs(rows, ctx_tiles, off=0):
    toks = rows // BATCH
    return [pl.BlockSpec((BATCH, toks, D_MODEL), lambda i: (0, jnp.minimum(i + off, ctx_tiles - 1), 0)),
            pl.BlockSpec((BATCH, toks, D_MODEL), lambda i: (0, jnp.maximum(i + off - ctx_tiles, 0), 0))]


def _inproj_body(x, mod_ref, g_ref, w_ref, ax_ref, u_ref, gate_ref, v_ref, vc_ref):
    h = _rms(x) * g_ref[...]
    h = h * (1.0 + _rows(mod_ref[1], ROW_TILE)) + _rows(mod_ref[0], ROW_TILE)
    p = jnp.dot(h.astype(BF16), w_ref[...], preferred_element_type=F32)
    ax_ref[...] = p[:, 0:D_RNN]
    u_ref[...] = p[:, D_RNN:D_RNN + D_SSM].astype(BF16)
    gate_ref[...] = p[:, D_RNN + D_SSM:2 * D_RNN + D_SSM].astype(BF16)
    v = p[:, 2 * D_RNN + D_SSM:]
    v_ref[...] = v

    @pl.when(pl.program_id(0) == 0)
    def _():
        vc_ref[...] = v


def _inproj_kernel(x_ref, *rest):
    _inproj_body(x_ref[...], *rest)


def _inproj_first_kernel(ctx_ref, x_ref, mod_ref, g_ref, w_ref, *rest):
    x = _time_major(ctx_ref, x_ref, rest[-1], 1)
    _inproj_body(x, mod_ref, g_ref, w_ref, *rest[:-1])


def _inproj(xt, modtab, g, w_in, layer):
    first = isinstance(xt, tuple)
    tok = lambda w: pl.BlockSpec((ROW_TILE, w), lambda i: (i, 0))
    out = lambda w, dt=F32: jax.ShapeDtypeStruct((N_ROWS, w), dt)
    out_specs = [tok(D_RNN), tok(D_SSM), tok(D_RNN),
                 pl.BlockSpec((ROW_TILE, D_POOL), lambda i: (jnp.maximum(i - 1, 0), 0)),
                 pl.BlockSpec((CTX_ROWS, D_POOL), lambda i: (0, 0))]
    out_shape = [out(D_RNN), out(D_SSM, BF16), out(D_RNN, BF16),
                 jax.ShapeDtypeStruct((SEQ_ROWS, D_POOL), F32),
                 jax.ShapeDtypeStruct((CTX_ROWS, D_POOL), F32)]
    common = [_mod_spec(), _const_spec((1, D_MODEL)), _layer_spec(w_in.shape, layer)]
    params = pltpu.CompilerParams(dimension_semantics=("arbitrary",), vmem_limit_bytes=VMEM_LIMIT)
    if not first:
        return (xt,) + tuple(pl.pallas_call(
            _inproj_kernel, grid=(N_ROW_TILES,), in_specs=[tok(D_MODEL)] + common,
            out_specs=out_specs, out_shape=out_shape, compiler_params=params, name="inproj",
        )(xt, modtab, g, w_in))
    return (xt,) + tuple(pl.pallas_call(
        _inproj_first_kernel, grid=(N_ROW_TILES,), in_specs=_batch_major_specs(ROW_TILE, 1) + common,
        out_specs=out_specs, out_shape=out_shape,
        scratch_shapes=[pltpu.VMEM((D_MODEL // LANES, ROW_TILE, LANES), F32)],
        compiler_params=params, name="inproj_first",
    )(xt[0], xt[1], modtab, g, w_in))


def _fwd_tile(i):
    return i


def _bwd_tile(i):
    return jnp.where(i < SCAN_CTX_TILES, SCAN_CTX_TILES - 1 - i, SCAN_TILES + SCAN_CTX_TILES - 1 - i)


def _scan_kernel(axp_f, ax_f, axn_f, u_f, axp_b, ax_b, axn_b, u_b,
                 cw_ref, cb_ref, wg_ref, bg_ref, spl_ref, bmat_ref, cmat_ref, lam_ref,
                 hf_ref, ysf_ref, hb_ref, ysb_ref,
                 axs_f, axs_b, a_f, b_f, a_b, b_b, us1, us2, xs1, xs2, hq1, hq2, ys1, ys2, lcar, scar):
    i = pl.program_id(0)

    @pl.when(i == 0)
    def _():
        lcar[...] = jnp.zeros_like(lcar)
        scar[...] = jnp.zeros_like(scar)

    def stage_conv(tile, axp, ax, axn, axs):
        prev_ok = jnp.logical_and(tile != 0, tile != SCAN_CTX_TILES)
        next_ok = jnp.logical_and(tile != SCAN_CTX_TILES - 1, tile != SCAN_TILES - 1)
        axs[0:SUBLANES, :] = jnp.where(prev_ok, axp[...], 0.0)
        axs[SUBLANES:SUBLANES + SCAN_ROWS, :] = ax[...]
        axs[SUBLANES + SCAN_ROWS:, :] = jnp.where(next_ok, axn[...], 0.0)

    def gates(d, k, axs, a_out, b_out):
        lanes = slice(k * LANES, (k + 1) * LANES)
        xc = cb_ref[:, lanes] + axs[0:SCAN_ROWS, lanes] * cw_ref[0:1, lanes]
        for t in range(1, CONV_WIDTH):
            xc = xc + axs[t * BATCH:t * BATCH + SCAN_ROWS, lanes] * cw_ref[t:t + 1, lanes]
        g = jnp.dot(xc.astype(BF16), wg_ref[d, k], preferred_element_type=F32)
        r = jax.nn.sigmoid(g[:, :LANES] + bg_ref[d, :, lanes])
        ig = jax.nn.sigmoid(g[:, LANES:] + bg_ref[d, :, D_RNN + k * LANES:D_RNN + (k + 1) * LANES])
        a = jnp.exp(-(r * spl_ref[d, :, lanes]))
        a_out[:, lanes] = a
        b_out[:, lanes] = jnp.sqrt(1.0 - a * a) * (ig * xc)

    lo = lax.broadcasted_iota(jnp.int32, (SUBLANES, 1), 0) < HALF

    def swap(x):
        return pltpu.roll(x, HALF, 0)

    def rows_of(j):
        jb = SCAN_BLKS - 1 - j
        return slice(j * SUBLANES, (j + 1) * SUBLANES), slice(jb * SUBLANES, (jb + 1) * SUBLANES)

    def steps(vf, vb):
        return jnp.where(lo, vf, vb), swap(jnp.where(lo, vb, vf))

    def unsteps(h1, h2):
        h2s = swap(h2)
        return jnp.where(lo, h1, h2s), jnp.where(lo, h2s, h1)

    def pair_rows(j):
        jb = SCAN_BLKS - 2 - j
        return slice(j * SUBLANES, (j + 2) * SUBLANES), slice(jb * SUBLANES, (jb + 2) * SUBLANES)

    def pack2(blk0, blk1):
        return jnp.concatenate([blk0, blk1], axis=0).astype(BF16)

    def lru_scan(h, j0, j1):
        for j in range(j0, j1, 2):
            outs = []
            for jj in (j, j + 1):
                rf, rb = rows_of(jj)
                a1, a2 = steps(a_f[rf, :], a_b[rb, :])
                b1, b2 = steps(b_f[rf, :], b_b[rb, :])
                h1 = a1 * h + b1
                h = a2 * h1 + b2
                outs.append(unsteps(h1, h))
            pf, pb = pair_rows(j)
            hf_ref[pf, :] = pack2(outs[0][0], outs[1][0])
            hb_ref[pb, :] = pack2(outs[1][1], outs[0][1])
        return h

    for j in range(0, SCAN_BLKS, 2):
        pf, pb = pair_rows(j)
        uf, ub = u_f[pf, :].astype(F32), u_b[pb, :].astype(F32)
        first = steps(uf[:SUBLANES], ub[SUBLANES:])
        second = steps(uf[SUBLANES:], ub[:SUBLANES])
        us1[pf, :] = pack2(first[0], second[0])
        us2[pf, :] = pack2(first[1], second[1])
    stage_conv(_fwd_tile(i), axp_f, ax_f, axn_f, axs_f)
    stage_conv(_bwd_tile(i), axp_b, ax_b, axn_b, axs_b)

    nst = S5_BLK_STATES

    def s5_scan(jb, carry, j0, j1):
        x1, x2 = xs1.at[jb], xs2.at[jb]
        re, im = slice(0, nst), slice(nst, 2 * nst)
        lr, li = lam_ref[0, :, jb * nst:(jb + 1) * nst], lam_ref[1, :, jb * nst:(jb + 1) * nst]
        sr, si = carry
        held = None
        for j in range(j0, j1):
            rows = slice(j * SUBLANES, (j + 1) * SUBLANES)
            r1 = lr * sr - li * si + x1[rows, re]
            i1 = lr * si + li * sr + x1[rows, im]
            sr = lr * r1 - li * i1 + x2[rows, re]
            si = lr * i1 + li * r1 + x2[rows, im]
            if held is None:
                held = (r1, i1, sr, si)
                continue
            pair = slice((j - 1) * SUBLANES, (j + 1) * SUBLANES)
            for hq, cols, lo_blk, hi_blk in ((hq1, re, held[0], r1), (hq1, im, held[1], i1),
                                             (hq2, re, held[2], sr), (hq2, im, held[3], si)):
                hq[jb, pair, cols] = jnp.concatenate([lo_blk, hi_blk], axis=0).astype(BF16)
            held = None
        return sr, si

    def out_proj(jb, arr):
        hq, ys = ((hq1, ys1), (hq2, ys2))[arr]
        y = jnp.dot(hq[jb, :, :nst], cmat_ref[jb, :nst, :], preferred_element_type=F32)
        y = y + jnp.dot(hq[jb, :, nst:], cmat_ref[jb, nst:, :], preferred_element_type=F32)
        ys[:, 2 * jb * LANES:2 * (jb + 1) * LANES] = y

    for jb in range(S5_BLOCKS):
        blk = slice(jb * LANES, (jb + 1) * LANES)
        gates(0, jb, axs_f, a_f, b_f)
        xs1[jb] = jnp.dot(us1[:, blk], bmat_ref[jb], preferred_element_type=F32)
        gates(1, jb, axs_b, a_b, b_b)
        xs2[jb] = jnp.dot(us2[:, blk], bmat_ref[jb], preferred_element_type=F32)

    seg = SCAN_BLKS // SCAN_SEGS
    half = SCAN_SEGS // 2
    for jb in range(S5_BLOCKS):
        carry = (scar[:, jb * nst:(jb + 1) * nst], scar[:, N_STATE + jb * nst:N_STATE + (jb + 1) * nst])
        for q in range(SCAN_SEGS):
            carry = s5_scan(jb, carry, q * seg, (q + 1) * seg)
            if jb > 0 and q % half == half - 1:
                out_proj(jb - 1, q // half)
        scar[:, jb * nst:(jb + 1) * nst] = carry[0]
        scar[:, N_STATE + jb * nst:N_STATE + (jb + 1) * nst] = carry[1]
    h = lcar[...]
    for q in range(SCAN_SEGS):
        h = lru_scan(h, q * seg, (q + 1) * seg)
        if q % half == half - 1:
            out_proj(S5_BLOCKS - 1, q // half)
    lcar[...] = h

    for j in range(0, SCAN_BLKS, 2):
        pf, pb = pair_rows(j)
        rows = [slice(jj * SUBLANES, (jj + 1) * SUBLANES) for jj in (j, j + 1)]
        outs = [unsteps(ys1[r, :], ys2[r, :]) for r in rows]
        for jb in range(S5_BLOCKS):
            fcols = slice(2 * jb * LANES, (2 * jb + 1) * LANES)
            bcols = slice((2 * jb + 1) * LANES, 2 * (jb + 1) * LANES)
            ysf_ref[pf, jb * LANES:(jb + 1) * LANES] = pack2(outs[0][0][:, fcols], outs[1][0][:, fcols])
            ysb_ref[pb, jb * LANES:(jb + 1) * LANES] = pack2(outs[1][1][:, bcols], outs[0][1][:, bcols])


def _scan(ax, u, cw, cb, wg, bg, spl, bmat, cmat, lam):
    tpb = SCAN_ROWS // SUBLANES
    nb8 = N_ROWS // SUBLANES

    def specs(tile_of):
        tok = lambda w: pl.BlockSpec((SCAN_ROWS, w), lambda i: (tile_of(i), 0))
        prev = pl.BlockSpec((SUBLANES, D_RNN), lambda i: (jnp.maximum(tile_of(i) * tpb - 1, 0), 0))
        nxt = pl.BlockSpec((SUBLANES, D_RNN), lambda i: (jnp.minimum((tile_of(i) + 1) * tpb, nb8 - 1), 0))
        return tok, prev, nxt

    tok_f, prev_f, next_f = specs(_fwd_tile)
    tok_b, prev_b, next_b = specs(_bwd_tile)
    out = jax.ShapeDtypeStruct((N_ROWS, D_RNN), BF16)
    tile_f32 = lambda w: pltpu.VMEM((SCAN_ROWS, w), F32)
    states = lambda dt: pltpu.VMEM((S5_BLOCKS, SCAN_ROWS, 2 * S5_BLK_STATES), dt)
    return pl.pallas_call(
        _scan_kernel,
        grid=(SCAN_TILES,),
        in_specs=[prev_f, tok_f(D_RNN), next_f, tok_f(D_SSM),
                  prev_b, tok_b(D_RNN), next_b, tok_b(D_SSM),
                  _const_spec((CONV_WIDTH, D_RNN)), _const_spec((1, D_RNN)),
                  _const_spec((2, D_RNN // LANES, LANES, 2 * LANES)), _const_spec((2, 1, 2 * D_RNN)),
                  _const_spec((2, 1, D_RNN)),
                  _const_spec((S5_BLOCKS, LANES, 2 * S5_BLK_STATES)),
                  _const_spec((S5_BLOCKS, 2 * S5_BLK_STATES, 2 * LANES)),
                  _const_spec((2, SUBLANES, N_STATE))],
        out_specs=[tok_f(D_RNN), tok_f(D_SSM), tok_b(D_RNN), tok_b(D_SSM)],
        out_shape=[out, out, out, out],
        scratch_shapes=[pltpu.VMEM((SCAN_ROWS + 2 * SUBLANES, D_RNN), F32),
                        pltpu.VMEM((SCAN_ROWS + 2 * SUBLANES, D_RNN), F32),
                        tile_f32(D_RNN), tile_f32(D_RNN), tile_f32(D_RNN), tile_f32(D_RNN),
                        pltpu.VMEM((SCAN_ROWS, D_SSM), BF16), pltpu.VMEM((SCAN_ROWS, D_SSM), BF16),
                        states(F32), states(F32), states(BF16), states(BF16),
                        tile_f32(2 * D_SSM), tile_f32(2 * D_SSM),
                        pltpu.VMEM((SUBLANES, D_RNN), F32),
                        pltpu.VMEM((SUBLANES, 2 * N_STATE), F32)],
        compiler_params=pltpu.CompilerParams(
            dimension_semantics=("arbitrary",), vmem_limit_bytes=VMEM_LIMIT),
        name="scan",
    )(ax, ax, ax, u, ax, ax, ax, u, cw, cb, wg, bg, spl, bmat, cmat, lam)


def _window_sums(load, ha, hb):
    sa = load(-ha)
    for o in range(-ha + 1, ha):
        sa = sa + load(o)
    sb = sa
    for o in list(range(-hb, -ha)) + list(range(ha, hb)):
        sb = sb + load(o)
    return sa, sb


def _count(pos, half, n):
    return (jnp.minimum(pos + half, n) - jnp.maximum(pos - half, 0)).astype(F32)


def _pool_halves():
    is_a = lax.broadcasted_iota(jnp.int32, (1, LANES), 1) < POOL_GROUP
    for half in range(D_POOL // LANES):
        wa, wb = POOL_WINDOWS[2 * half], POOL_WINDOWS[2 * half + 1]
        yield slice(half * LANES, (half + 1) * LANES), wa // 2, wb // 2, is_a


def _pool_ctx_kernel(v_ref, m_ref, cp):
    pad = POOL_PAD * BATCH
    tok = lax.broadcasted_iota(jnp.int32, (CTX_ROWS, 1), 0) // BATCH
    for lanes, ha, hb, is_a in _pool_halves():
        cp[...] = jnp.zeros_like(cp)
        cp[pad:pad + CTX_ROWS, :] = v_ref[:, lanes]
        ca, cb = _window_sums(lambda o: cp[pad + o * BATCH:pad + o * BATCH + CTX_ROWS, :], ha, hb)
        pooled = jnp.where(is_a, ca / _count(tok, ha, CTX_LEN), cb / _count(tok, hb, CTX_LEN))
        m_ref[:, lanes] = pooled - v_ref[:, lanes]


def _pool_ctx(vc):
    return pl.pallas_call(
        _pool_ctx_kernel,
        out_shape=jax.ShapeDtypeStruct((CTX_ROWS, D_POOL), F32),
        scratch_shapes=[pltpu.VMEM((CTX_ROWS + 2 * POOL_PAD * BATCH, LANES), F32)],
        name="pool_ctx",
    )(vc)


def _pool_kernel(vprev_ref, v_ref, vnext_ref, m_ref, vp, sp):
    i = pl.program_id(0)
    g0 = i * POOL_GROWS
    pad = POOL_PAD * BATCH
    col = lax.broadcasted_iota(jnp.int32, (GROW, 1), 0) // BATCH
    have_prev = i > 0
    have_next = i < pl.num_programs(0) - 1
    for lanes, ha, hb, is_a in _pool_halves():
        vp[...] = jnp.zeros_like(vp)

        def fill(src_ref, first, keep, lanes=lanes):
            def body(q, _):
                dst = pl.multiple_of((first + q) * PROW + pad, SUBLANES)
                src = pl.multiple_of(q * GROW, SUBLANES)
                vp[pl.ds(dst, GROW), :] = jnp.where(keep, src_ref[pl.ds(src, GROW), lanes], 0.0)
                return 0
            return body

        lax.fori_loop(0, POOL_PAD, fill(vprev_ref, 0, have_prev), 0)
        lax.fori_loop(0, POOL_GROWS, fill(v_ref, POOL_PAD, True), 0)
        lax.fori_loop(0, POOL_PAD, fill(vnext_ref, POOL_PAD + POOL_GROWS, have_next), 0)
        ccnt_a, ccnt_b = _count(col, ha, GRID_W), _count(col, hb, GRID_W)

        def body(r, _, lanes=lanes, ha=ha, hb=hb, is_a=is_a, ccnt_a=ccnt_a, ccnt_b=ccnt_b):
            base = (r + POOL_PAD) * PROW
            sa, sb = _window_sums(
                lambda o: vp[pl.ds(pl.multiple_of(base + o * PROW, SUBLANES), PROW), :], ha, hb)
            sp[...] = jnp.where(is_a, sa, sb)
            ba, bb = _window_sums(lambda o: sp[pad + o * BATCH:pad + o * BATCH + GROW, :], ha, hb)
            rcnt_a, rcnt_b = _count(g0 + r, ha, GRID_H), _count(g0 + r, hb, GRID_H)
            pooled = jnp.where(is_a, ba / (rcnt_a * ccnt_a), bb / (rcnt_b * ccnt_b))
            src = pl.multiple_of(r * GROW, SUBLANES)
            m_ref[pl.ds(src, GROW), lanes] = pooled - v_ref[pl.ds(src, GROW), lanes]
            return 0

        lax.fori_loop(0, POOL_GROWS, body, 0)


def _pool(v):
    n = SEQ_ROWS // POOL_TILE
    per = POOL_TILE // POOL_HALO
    nh = SEQ_ROWS // POOL_HALO
    return pl.pallas_call(
        _pool_kernel,
        grid=(n,),
        in_specs=[pl.BlockSpec((POOL_HALO, D_POOL), lambda i: (jnp.maximum(i * per - 1, 0), 0)),
                  pl.BlockSpec((POOL_TILE, D_POOL), lambda i: (i, 0)),
                  pl.BlockSpec((POOL_HALO, D_POOL), lambda i: (jnp.minimum((i + 1) * per, nh - 1), 0))],
        out_specs=pl.BlockSpec((POOL_TILE, D_POOL), lambda i: (i, 0)),
        out_shape=jax.ShapeDtypeStruct((SEQ_ROWS, D_POOL), F32),
        scratch_shapes=[pltpu.VMEM(((POOL_GROWS + 2 * POOL_PAD) * PROW, LANES), F32),
                        pltpu.VMEM((PROW, LANES), F32)],
        compiler_params=pltpu.CompilerParams(
            dimension_semantics=("parallel",), vmem_limit_bytes=VMEM_LIMIT),
        name="pool",
    )(v, v, v)


def _tail_kernel(first, final, *refs):
    refs = list(refs)
    gate_ref, hf_ref, hb_ref, u_ref, ysf_ref, ysb_ref, m_ref = refs[:7]
    del refs[:7]
    mc_ref = None if final else refs.pop(0)
    x_refs = [refs.pop(0) for _ in range(2 if first else 1)]
    (mod_ref, dvec_ref, wglu_ref, bglu_ref, pw_ref, pb_ref, ps_ref, wo_ref,
     g2_ref, wg_ref, wu_ref, wd_ref, fg_ref, o_ref) = refs[:14]
    slab = refs[14] if (first or final) else None
    dot = functools.partial(jnp.dot, preferred_element_type=F32)
    f32 = lambda ref: ref[...].astype(F32)
    ctx_tiles = 0 if final else TAIL_CTX_TILES
    x_in = _time_major(x_refs[0], x_refs[1], slab, ctx_tiles) if first else x_refs[0][...]

    y_a = jax.nn.gelu(f32(gate_ref)) * (f32(hf_ref) + f32(hb_ref))
    y_s = dvec_ref[...] * f32(u_ref) + f32(ysf_ref) + f32(ysb_ref)
    z = jax.nn.gelu(y_s)
    y_b = z * jax.nn.sigmoid(dot(z.astype(BF16), wglu_ref[...]) + bglu_ref[...])
    m = m_ref[...] if final else jnp.where(pl.program_id(0) < TAIL_CTX_TILES, mc_ref[...], m_ref[...])
    y_c = (dot(m.astype(BF16), pw_ref[...]) + pb_ref[...]) * ps_ref[...]
    y = jnp.concatenate([y_a.astype(BF16), y_b.astype(BF16), y_c.astype(BF16)], axis=1)
    x = x_in + _rows(mod_ref[2], TAIL_TILE) * dot(y, wo_ref[...])

    h = _rms(x) * g2_ref[...]
    h = (h * (1.0 + _rows(mod_ref[4], TAIL_TILE)) + _rows(mod_ref[3], TAIL_TILE)).astype(BF16)
    ff = None
    for c0, c1 in zip(FF_EDGES[:-1], FF_EDGES[1:]):
        cols = slice(c0, c1)
        act = (jax.nn.silu(dot(h, wg_ref[:, cols])) * dot(h, wu_ref[:, cols])).astype(BF16)
        part = dot(act, wd_ref[cols, :])
        ff = part if ff is None else ff + part
    y = x + _rows(mod_ref[5], TAIL_TILE) * ff
    if not final:
        o_ref[...] = y
        return
    y = _rms(y) * fg_ref[...]
    for s in range(D_MODEL // LANES):
        slab[s] = y[:, s * LANES:(s + 1) * LANES]
    for b in range(BATCH):
        for s in range(D_MODEL // LANES):
            o_ref[b, :, s * LANES:(s + 1) * LANES] = slab[s, pl.ds(b, TAIL_TILE // BATCH, stride=BATCH), :]


def _tail(final, layer, gate, hf, hb, u, ysf, ysb, m, mc, xt, modtab, weights):
    off = TAIL_CTX_TILES if final else 0
    n_tiles = N_ROWS // TAIL_TILE - off
    tok = lambda w: pl.BlockSpec((TAIL_TILE, w), lambda i: (i + off, 0))
    acts = [gate, hf, hb, u, ysf, ysb, m]
    act_specs = [tok(D_RNN), tok(D_RNN), tok(D_RNN), tok(D_SSM), tok(D_SSM), tok(D_SSM),
                 pl.BlockSpec((TAIL_TILE, D_POOL), lambda i: (jnp.maximum(i + off - TAIL_CTX_TILES, 0), 0))]
    first = isinstance(xt, tuple)
    if final:
        toks = TAIL_TILE // BATCH
        out_spec = pl.BlockSpec((BATCH, toks, D_MODEL), lambda i: (0, i, 0))
        out_shape = jax.ShapeDtypeStruct((BATCH, n_tiles * toks, D_MODEL), F32)
    else:
        acts.append(mc)
        act_specs.append(pl.BlockSpec((TAIL_TILE, D_POOL), lambda i: (jnp.minimum(i, TAIL_CTX_TILES - 1), 0)))
        out_spec = pl.BlockSpec((TAIL_TILE, D_MODEL), lambda i: (i, 0))
        out_shape = jax.ShapeDtypeStruct((N_ROWS, D_MODEL), F32)
    if first:
        acts += list(xt)
        act_specs += _batch_major_specs(TAIL_TILE, TAIL_CTX_TILES, off)
    else:
        acts.append(xt)
        act_specs.append(tok(D_MODEL))
    scratch = [pltpu.VMEM((D_MODEL // LANES, TAIL_TILE, LANES), F32)] if (first or final) else []
    return pl.pallas_call(
        functools.partial(_tail_kernel, first, final),
        grid=(n_tiles,),
        in_specs=(act_specs + [_mod_spec(off, TAIL_CTX_TILES)]
                  + [_const_spec(w.shape) if w.ndim == 2 else _layer_spec(w.shape, layer) for w in weights]),
        out_specs=out_spec,
        out_shape=out_shape,
        scratch_shapes=scratch,
        compiler_params=pltpu.CompilerParams(
            dimension_semantics=("parallel",), vmem_limit_bytes=VMEM_LIMIT),
        name="tail_final" if final else "tail",
    )(*acts, modtab, *weights)


def _block_diag(w):
    n, di, dj = w.shape
    return jnp.einsum('nij,nm->nimj', w, jnp.eye(n, dtype=w.dtype)).reshape(n * di, n * dj)


def _s5_params(lam_re, lam_im, log_dt, b_re, b_im, c_re, c_im):
    gpb = SSM_GROUPS // S5_BLOCKS

    def in_blocks(w):
        w = jnp.swapaxes(w.astype(F32), 1, 2).reshape(S5_BLOCKS, gpb, SSM_GROUP, SSM_STATE)
        return jnp.stack([_block_diag(w[j]) for j in range(S5_BLOCKS)])

    def out_blocks(w):
        w = jnp.swapaxes(w, 1, 2).reshape(S5_BLOCKS, gpb, SSM_STATE, SSM_GROUP)
        return jnp.stack([_block_diag(w[j]) for j in range(S5_BLOCKS)])

    bmat = jnp.concatenate([in_blocks(b_re), in_blocks(b_im)], axis=2).astype(BF16)
    cmats, lams = [], []
    for d in range(2):
        lr, li = lam_re[d].astype(F32), lam_im[d].astype(F32)
        dt = jnp.exp(log_dt[d].astype(F32))[:, None]
        mag = jnp.exp(lr * dt)
        ang = li * dt
        bar_r, bar_i = mag * jnp.cos(ang), mag * jnp.sin(ang)
        den = lr * lr + li * li
        fr = ((bar_r - 1.0) * lr + bar_i * li) / den
        fi = (bar_i * lr - (bar_r - 1.0) * li) / den
        cr, ci = c_re[d].astype(F32), c_im[d].astype(F32)
        cfr = cr * fr[:, None, :] - ci * fi[:, None, :]
        cfi = cr * fi[:, None, :] + ci * fr[:, None, :]
        cmats.append(jnp.concatenate([out_blocks(cfr), out_blocks(-cfi)], axis=1))
        lams.append(jnp.stack([bar_r.reshape(-1), bar_i.reshape(-1)]))
    cmat = jnp.concatenate(cmats, axis=-1).astype(BF16)
    lam = jnp.concatenate([jnp.broadcast_to(lams[d][:, None, :], (2, HALF, N_STATE)) for d in range(2)],
                          axis=1)
    return bmat, cmat, lam


def kernel(x, c, ctx, c_ctx, w_mod, b_mod, norm1_g, norm2_g, w_in, w_out, lru_conv_w, lru_conv_b,
           lru_wa, lru_ba, lru_wi, lru_bi, lru_lambda, s5_lambda_re, s5_lambda_im, s5_log_dt,
           s5_b_re, s5_b_im, s5_c_re, s5_c_im, s5_d, s5_glu_w, s5_glu_b, pool_w, pool_b, pool_scale,
           ffn_w_gate, ffn_w_up, ffn_w_down, final_g):
    cc = jnp.zeros((SUBLANES, D_MODEL), F32).at[:BATCH].set(c).at[BATCH].set(c_ctx)
    mod = _modulation(cc, w_mod, b_mod)
    xt = (ctx, x)
    fg = final_g.reshape(1, D_MODEL)
    w_in_b, w_out_b, glu_b = w_in.astype(BF16), w_out.astype(BF16), s5_glu_w.astype(BF16)
    ffn_gate_b, ffn_up_b, ffn_down_b = (w.astype(BF16) for w in (ffn_w_gate, ffn_w_up, ffn_w_down))

    for l in range(DEPTH):
        last = l == DEPTH - 1
        lat = jnp.transpose(mod[l, :BATCH].reshape(BATCH, 6, D_MODEL), (1, 0, 2))
        lat = jnp.concatenate([lat, lat], axis=1)
        cm = jnp.broadcast_to(mod[l, BATCH].reshape(6, 1, D_MODEL), (6, SUBLANES, D_MODEL))
        modtab = jnp.stack([cm, lat])

        xt, ax, u, gate, v, vc = _inproj(xt, modtab, norm1_g[l].reshape(1, D_MODEL), w_in_b, l)

        hpb = LANES // (D_RNN // lru_wa.shape[2])
        wg = jnp.stack([jnp.stack([
            jnp.concatenate([_block_diag(lru_wa[l, d, k * hpb:(k + 1) * hpb]),
                             _block_diag(lru_wi[l, d, k * hpb:(k + 1) * hpb])], axis=1)
            for k in range(D_RNN // LANES)]) for d in range(2)]).astype(BF16)
        bg = jnp.concatenate([lru_ba[l], lru_bi[l]], axis=-1).reshape(2, 1, 2 * D_RNN)
        spl = (8.0 * jax.nn.softplus(-lru_lambda[l].astype(F32))).reshape(2, 1, D_RNN)
        bmat, cmat, lam = _s5_params(s5_lambda_re[l], s5_lambda_im[l], s5_log_dt[l],
                                     s5_b_re[l], s5_b_im[l], s5_c_re[l], s5_c_im[l])
        hf, ysf, hb, ysb = _scan(ax, u, lru_conv_w[l], lru_conv_b[l].reshape(1, D_RNN),
                                 wg, bg, spl, bmat, cmat, lam)

        m = _pool(v)
        mc = None if last else _pool_ctx(vc)
        weights = (s5_d[l].reshape(1, D_SSM), glu_b, s5_glu_b[l].reshape(1, D_SSM),
                   _block_diag(pool_w[l]).astype(BF16), pool_b[l].reshape(1, D_POOL),
                   pool_scale[l].reshape(1, D_POOL), w_out_b,
                   norm2_g[l].reshape(1, D_MODEL), ffn_gate_b, ffn_up_b, ffn_down_b, fg)
        xt = _tail(last, l, gate, hf, hb, u, ysf, ysb, m, mc, xt, modtab, weights)
    return xt
```

```python
import functools

import jax
import jax.numpy as jnp
from jax import lax
from jax.experimental import pallas as pl
from jax.experimental.pallas import tpu as pltpu

F32 = jnp.float32
BF16 = jnp.bfloat16

D_MODEL = 1024
BATCH = 4
SEQ = 8192
DEPTH = 2
CTX_LEN = 256
GRID_W = 64
GRID_H = SEQ // GRID_W
EPS = 1e-6

D_RNN = 384
CONV_WIDTH = 4
LRU_C = 8.0
LOG2_E = 1.4426950408889634

D_SSM = 384
SSM_GROUP = 16
SSM_GROUPS = D_SSM // SSM_GROUP
SSM_STATE = 64
N_STATE = SSM_GROUPS * SSM_STATE

D_POOL = 256
POOL_WINDOWS = (2, 4, 8, 16)
POOL_GROUP = D_POOL // len(POOL_WINDOWS)
POOL_PAD = 8

D_IN = 2 * D_RNN + D_SSM + D_POOL
D_MIX = D_RNN + D_SSM + D_POOL
D_FF = 2816
MXU_DIM = 256
FF_EDGES = (0, 6 * MXU_DIM, D_FF)

LANES = 128
SUBLANES = 8
HALF = SUBLANES // 2
assert BATCH == HALF

N_TOK = CTX_LEN + SEQ
N_ROWS = N_TOK * BATCH
CTX_ROWS = CTX_LEN * BATCH
SEQ_ROWS = SEQ * BATCH

ROW_TILE = 1024
N_ROW_TILES = N_ROWS // ROW_TILE
assert CTX_ROWS == ROW_TILE
TAIL_TILE = 512
TAIL_CTX_TILES = CTX_ROWS // TAIL_TILE

SCAN_TOK = 128
SCAN_ROWS = SCAN_TOK * BATCH
SCAN_TILES = N_TOK // SCAN_TOK
SCAN_CTX_TILES = CTX_LEN // SCAN_TOK
SCAN_BLKS = SCAN_ROWS // SUBLANES
S5_BLOCKS = D_SSM // LANES
S5_BLK_STATES = N_STATE // S5_BLOCKS
SCAN_SEGS = 4

POOL_GROWS = 16
POOL_TILE = POOL_GROWS * GRID_W * BATCH
POOL_HALO = POOL_PAD * GRID_W * BATCH
GROW = GRID_W * BATCH
PROW = (GRID_W + 2 * POOL_PAD) * BATCH

VMEM_LIMIT = 56 * 1024 * 1024


def _const_spec(shape):
    nd = len(shape)
    return pl.BlockSpec(shape, lambda *_: (0,) * nd, pipeline_mode=pl.Buffered(1))


def _layer_spec(shape, layer):
    nd = len(shape) - 1
    return pl.BlockSpec((None,) + tuple(shape[1:]), lambda *_: (layer,) + (0,) * nd,
                        pipeline_mode=pl.Buffered(1))


def _rms(x):
    return x * lax.rsqrt(jnp.mean(x * x, axis=-1, keepdims=True) + EPS)


def _rows(pat, n):
    return jnp.broadcast_to(pat[None], (n // SUBLANES,) + pat.shape).reshape(n, pat.shape[-1])


def _mod_spec(layer, off=0, ctx_tiles=1):
    return pl.BlockSpec((None, None, 6, SUBLANES, D_MODEL),
                        lambda i: (layer, jnp.minimum((i + off) // ctx_tiles, 1), 0, 0, 0))


def _mod_kernel(c_ref, w_ref, b_ref, o_ref):
    s = jax.nn.silu(c_ref[...])
    o_ref[...] = jnp.dot(s.astype(BF16), w_ref[...].astype(BF16),
                         preferred_element_type=F32) + b_ref[...]


def _modulation(cc, w_mod, b_mod):
    n_col = 6 * D_MODEL
    blk = 1536
    return pl.pallas_call(
        _mod_kernel,
        grid=(DEPTH, n_col // blk),
        in_specs=[pl.BlockSpec((SUBLANES, D_MODEL), lambda l, j: (0, 0)),
                  pl.BlockSpec((None, D_MODEL, blk), lambda l, j: (l, 0, j)),
                  pl.BlockSpec((None, 1, blk), lambda l, j: (l, 0, j))],
        out_specs=pl.BlockSpec((None, SUBLANES, blk), lambda l, j: (l, 0, j)),
        out_shape=jax.ShapeDtypeStruct((DEPTH, SUBLANES, n_col), F32),
        compiler_params=pltpu.CompilerParams(vmem_limit_bytes=VMEM_LIMIT),
        name="modulation",
    )(cc, w_mod, b_mod.reshape(DEPTH, 1, n_col))


def _interleave(src_ref, slab):
    for b in range(BATCH):
        for s in range(D_MODEL // LANES):
            slab[s, pl.ds(b, ROW_TILE // BATCH, stride=BATCH), :] = src_ref[b, :, s * LANES:(s + 1) * LANES]


def _inproj_body(x, mod_ref, g_ref, w_ref, ax_ref, u_ref, gate_ref, v_ref, vc_ref):
    h = _rms(x) * g_ref[...]
    h = h * (1.0 + _rows(mod_ref[1], ROW_TILE)) + _rows(mod_ref[0], ROW_TILE)
    p = jnp.dot(h.astype(BF16), w_ref[...], preferred_element_type=F32)
    ax_ref[...] = p[:, 0:D_RNN]
    u_ref[...] = p[:, D_RNN:D_RNN + D_SSM].astype(BF16)
    gate_ref[...] = p[:, D_RNN + D_SSM:2 * D_RNN + D_SSM].astype(BF16)
    v = p[:, 2 * D_RNN + D_SSM:]
    v_ref[...] = v

    @pl.when(pl.program_id(0) == 0)
    def _():
        vc_ref[...] = v


def _inproj_kernel(x_ref, *rest):
    _inproj_body(x_ref[...], *rest)


def _inproj_first_kernel(ctx_ref, x_ref, mod_ref, g_ref, w_ref, xt_ref, *rest):
    slab = rest[-1]
    i = pl.program_id(0)

    @pl.when(i == 0)
    def _():
        _interleave(ctx_ref, slab)

    @pl.when(i > 0)
    def _():
        _interleave(x_ref, slab)

    x = jnp.concatenate([slab[s] for s in range(D_MODEL // LANES)], axis=1)
    xt_ref[...] = x
    _inproj_body(x, mod_ref, g_ref, w_ref, *rest[:-1])


def _inproj(xt, modtab, g, w_in, layer):
    first = isinstance(xt, tuple)
    tok = lambda w: pl.BlockSpec((ROW_TILE, w), lambda i: (i, 0))
    out = lambda w, dt=F32: jax.ShapeDtypeStruct((N_ROWS, w), dt)
    out_specs = [tok(D_RNN), tok(D_SSM), tok(D_RNN),
                 pl.BlockSpec((ROW_TILE, D_POOL), lambda i: (jnp.maximum(i - 1, 0), 0)),
                 pl.BlockSpec((CTX_ROWS, D_POOL), lambda i: (0, 0))]
    out_shape = [out(D_RNN), out(D_SSM, BF16), out(D_RNN, BF16),
                 jax.ShapeDtypeStruct((SEQ_ROWS, D_POOL), F32),
                 jax.ShapeDtypeStruct((CTX_ROWS, D_POOL), F32)]
    common = [_mod_spec(layer), _layer_spec(g.shape, layer), _layer_spec(w_in.shape, layer)]
    params = pltpu.CompilerParams(dimension_semantics=("arbitrary",), vmem_limit_bytes=VMEM_LIMIT)
    if not first:
        return (xt,) + tuple(pl.pallas_call(
            _inproj_kernel, grid=(N_ROW_TILES,), in_specs=[tok(D_MODEL)] + common,
            out_specs=out_specs, out_shape=out_shape, compiler_params=params, name="inproj",
        )(xt, modtab, g, w_in))
    toks = ROW_TILE // BATCH
    src = [pl.BlockSpec((BATCH, CTX_LEN, D_MODEL), lambda i: (0, 0, 0), pipeline_mode=pl.Buffered(1)),
           pl.BlockSpec((BATCH, toks, D_MODEL), lambda i: (0, jnp.maximum(i - 1, 0), 0))]
    return tuple(pl.pallas_call(
        _inproj_first_kernel, grid=(N_ROW_TILES,), in_specs=src + common,
        out_specs=[tok(D_MODEL)] + out_specs, out_shape=[out(D_MODEL)] + out_shape,
        scratch_shapes=[pltpu.VMEM((D_MODEL // LANES, ROW_TILE, LANES), F32)],
        compiler_params=params, name="inproj_first",
    )(xt[0], xt[1], modtab, g, w_in))


def _fwd_tile(i):
    return i


def _bwd_tile(i):
    return jnp.where(i < SCAN_CTX_TILES, SCAN_CTX_TILES - 1 - i, SCAN_TILES + SCAN_CTX_TILES - 1 - i)


def _scan_kernel(axp_f, ax_f, axn_f, u_f, axp_b, ax_b, axn_b, u_b,
                 cw_ref, cb_ref, wg_ref, bg_ref, spl_ref, bmat_ref, cmat_ref, lam_ref,
                 hf_ref, ysf_ref, hb_ref, ysb_ref,
                 axs_f, axs_b, a_f, b_f, a_b, b_b, uf32, ub32, us1, us2, xs1, xs2, ys1, ys2,
                 hfs, hbs, yfs, ybs, lcar, scar):
    i = pl.program_id(0)
    uf32[...] = u_f[...].astype(F32)
    ub32[...] = u_b[...].astype(F32)

    @pl.when(i == 0)
    def _():
        lcar[...] = jnp.zeros_like(lcar)
        scar[...] = jnp.zeros_like(scar)

    def stage_conv(tile, axp, ax, axn, axs):
        prev_ok = jnp.logical_and(tile != 0, tile != SCAN_CTX_TILES)
        next_ok = jnp.logical_and(tile != SCAN_CTX_TILES - 1, tile != SCAN_TILES - 1)
        axs[0:SUBLANES, :] = jnp.where(prev_ok, axp[...], 0.0)
        axs[SUBLANES:SUBLANES + SCAN_ROWS, :] = ax[...]
        axs[SUBLANES + SCAN_ROWS:, :] = jnp.where(next_ok, axn[...], 0.0)

    def gates(d, k, axs, a_out, b_out):
        lanes = slice(k * LANES, (k + 1) * LANES)
        xc = cb_ref[:, lanes] + axs[0:SCAN_ROWS, lanes] * cw_ref[0:1, lanes]
        for t in range(1, CONV_WIDTH):
            xc = xc + axs[t * BATCH:t * BATCH + SCAN_ROWS, lanes] * cw_ref[t:t + 1, lanes]
        g = jnp.dot(xc.astype(BF16), wg_ref[d, k], preferred_element_type=F32)
        r = 1.0 / (1.0 + jnp.exp2(g[:, :LANES] + bg_ref[d, :, lanes]))
        ig = 1.0 / (1.0 + jnp.exp2(g[:, LANES:] + bg_ref[d, :, D_RNN + k * LANES:D_RNN + (k + 1) * LANES]))
        a = jnp.exp2(r * spl_ref[d, :, lanes])
        a_out[:, lanes] = a
        b_out[:, lanes] = jnp.sqrt(1.0 - a * a) * (ig * xc)

    lo = lax.broadcasted_iota(jnp.int32, (SUBLANES, 1), 0) < HALF

    def swap(x):
        return pltpu.roll(x, HALF, 0)

    def rows_of(j):
        jb = SCAN_BLKS - 1 - j
        return slice(j * SUBLANES, (j + 1) * SUBLANES), slice(jb * SUBLANES, (jb + 1) * SUBLANES)

    def steps(vf, vb):
        return jnp.where(lo, vf, vb), swap(jnp.where(lo, vb, vf))

    def unsteps(h1, h2):
        h2s = swap(h2)
        return jnp.where(lo, h1, h2s), jnp.where(lo, h2s, h1)

    def lru_scan(h, j0, j1):
        for j in range(j0, j1):
            rf, rb = rows_of(j)
            a1, a2 = steps(a_f[rf, :], a_b[rb, :])
            b1, b2 = steps(b_f[rf, :], b_b[rb, :])
            h1 = a1 * h + b1
            h = a2 * h1 + b2
            hfs[rf, :], hbs[rb, :] = unsteps(h1, h)
        return h

    for j in range(SCAN_BLKS):
        rf, rb = rows_of(j)
        us1[rf, :], us2[rf, :] = steps(uf32[rf, :], ub32[rb, :])
    stage_conv(_fwd_tile(i), axp_f, ax_f, axn_f, axs_f)
    stage_conv(_bwd_tile(i), axp_b, ax_b, axn_b, axs_b)

    nst = S5_BLK_STATES

    def s5_scan(jb, carry, j0, j1):
        x1, x2 = xs1.at[jb], xs2.at[jb]
        re, im = slice(0, nst), slice(nst, 2 * nst)
        lr, li = lam_ref[0, :, jb * nst:(jb + 1) * nst], lam_ref[1, :, jb * nst:(jb + 1) * nst]
        sr, si = carry
        for j in range(j0, j1):
            rows = slice(j * SUBLANES, (j + 1) * SUBLANES)
            r1 = lr * sr - li * si + x1[rows, re]
            i1 = lr * si + li * sr + x1[rows, im]
            sr = lr * r1 - li * i1 + x2[rows, re]
            si = lr * i1 + li * r1 + x2[rows, im]
            x1[rows, re] = r1
            x1[rows, im] = i1
            x2[rows, re] = sr
            x2[rows, im] = si
        return sr, si

    def out_proj(jb, piece):
        xs, ys = ((xs1, ys1), (xs2, ys2))[piece // 2]
        half = slice(0, nst) if piece % 2 == 0 else slice(nst, 2 * nst)
        cols = slice(2 * jb * LANES, 2 * (jb + 1) * LANES)
        y = jnp.dot(xs[jb, :, half].astype(BF16), cmat_ref[jb, half, :], preferred_element_type=F32)
        ys[:, cols] = y if piece % 2 == 0 else ys[:, cols] + y

    for jb in range(S5_BLOCKS):
        blk = slice(jb * LANES, (jb + 1) * LANES)
        gates(0, jb, axs_f, a_f, b_f)
        xs1[jb] = jnp.dot(us1[:, blk].astype(BF16), bmat_ref[jb], preferred_element_type=F32)
        gates(1, jb, axs_b, a_b, b_b)
        xs2[jb] = jnp.dot(us2[:, blk].astype(BF16), bmat_ref[jb], preferred_element_type=F32)
    seg = SCAN_BLKS // SCAN_SEGS
    for jb in range(S5_BLOCKS):
        carry = (scar[:, jb * nst:(jb + 1) * nst], scar[:, N_STATE + jb * nst:N_STATE + (jb + 1) * nst])
        for q in range(SCAN_SEGS):
            carry = s5_scan(jb, carry, q * seg, (q + 1) * seg)
            if jb > 0:
                out_proj(jb - 1, q)
        scar[:, jb * nst:(jb + 1) * nst] = carry[0]
        scar[:, N_STATE + jb * nst:N_STATE + (jb + 1) * nst] = carry[1]
    h = lcar[...]
    for q in range(SCAN_SEGS):
        h = lru_scan(h, q * seg, (q + 1) * seg)
        out_proj(S5_BLOCKS - 1, q)
    lcar[...] = h
    hf_ref[...] = hfs[...].astype(BF16)
    hb_ref[...] = hbs[...].astype(BF16)

    for j in range(SCAN_BLKS):
        rf, rb = rows_of(j)
        of, ob = unsteps(ys1[rf, :], ys2[rf, :])
        for jb in range(S5_BLOCKS):
            yfs[rf, jb * LANES:(jb + 1) * LANES] = of[:, 2 * jb * LANES:(2 * jb + 1) * LANES]
            ybs[rb, jb * LANES:(jb + 1) * LANES] = ob[:, (2 * jb + 1) * LANES:2 * (jb + 1) * LANES]
    ysf_ref[...] = yfs[...].astype(BF16)
    ysb_ref[...] = ybs[...].astype(BF16)


def _scan(layer, ax, u, *params):
    tpb = SCAN_ROWS // SUBLANES
    nb8 = N_ROWS // SUBLANES

    def specs(tile_of):
        tok = lambda w: pl.BlockSpec((SCAN_ROWS, w), lambda i: (tile_of(i), 0))
        prev = pl.BlockSpec((SUBLANES, D_RNN), lambda i: (jnp.maximum(tile_of(i) * tpb - 1, 0), 0))
        nxt = pl.BlockSpec((SUBLANES, D_RNN), lambda i: (jnp.minimum((tile_of(i) + 1) * tpb, nb8 - 1), 0))
        return tok, prev, nxt

    tok_f, prev_f, next_f = specs(_fwd_tile)
    tok_b, prev_b, next_b = specs(_bwd_tile)
    out = jax.ShapeDtypeStruct((N_ROWS, D_RNN), BF16)
    tile_f32 = lambda w: pltpu.VMEM((SCAN_ROWS, w), F32)
    return pl.pallas_call(
        _scan_kernel,
        grid=(SCAN_TILES,),
        in_specs=[prev_f, tok_f(D_RNN), next_f, tok_f(D_SSM),
                  prev_b, tok_b(D_RNN), next_b, tok_b(D_SSM)] + [_layer_spec(p.shape, layer) for p in params],
        out_specs=[tok_f(D_RNN), tok_f(D_SSM), tok_b(D_RNN), tok_b(D_SSM)],
        out_shape=[out, out, out, out],
        scratch_shapes=[pltpu.VMEM((SCAN_ROWS + 2 * SUBLANES, D_RNN), F32),
                        pltpu.VMEM((SCAN_ROWS + 2 * SUBLANES, D_RNN), F32),
                        tile_f32(D_RNN), tile_f32(D_RNN), tile_f32(D_RNN), tile_f32(D_RNN),
                        tile_f32(D_SSM), tile_f32(D_SSM), tile_f32(D_SSM), tile_f32(D_SSM),
                        pltpu.VMEM((S5_BLOCKS, SCAN_ROWS, 2 * S5_BLK_STATES), F32),
                        pltpu.VMEM((S5_BLOCKS, SCAN_ROWS, 2 * S5_BLK_STATES), F32),
                        tile_f32(2 * D_SSM), tile_f32(2 * D_SSM),
                        tile_f32(D_RNN), tile_f32(D_RNN), tile_f32(D_SSM), tile_f32(D_SSM),
                        pltpu.VMEM((SUBLANES, D_RNN), F32),
                        pltpu.VMEM((SUBLANES, 2 * N_STATE), F32)],
        compiler_params=pltpu.CompilerParams(
            dimension_semantics=("arbitrary",), vmem_limit_bytes=VMEM_LIMIT),
        name="scan",
    )(ax, ax, ax, u, ax, ax, ax, u, *params)


def _window_sums(load, ha, hb):
    sa = load(-ha)
    for o in range(-ha + 1, ha):
        sa = sa + load(o)
    sb = sa
    for o in list(range(-hb, -ha)) + list(range(ha, hb)):
        sb = sb + load(o)
    return sa, sb


def _count(pos, half, n):
    return (jnp.minimum(pos + half, n) - jnp.maximum(pos - half, 0)).astype(F32)


def _pool_halves():
    is_a = lax.broadcasted_iota(jnp.int32, (1, LANES), 1) < POOL_GROUP
    for half in range(D_POOL // LANES):
        wa, wb = POOL_WINDOWS[2 * half], POOL_WINDOWS[2 * half + 1]
        yield slice(half * LANES, (half + 1) * LANES), wa // 2, wb // 2, is_a


def _pool_ctx_kernel(v_ref, m_ref, cp):
    pad = POOL_PAD * BATCH
    tok = lax.broadcasted_iota(jnp.int32, (CTX_ROWS, 1), 0) // BATCH
    for lanes, ha, hb, is_a in _pool_halves():
        cp[...] = jnp.zeros_like(cp)
        cp[pad:pad + CTX_ROWS, :] = v_ref[:, lanes]
        ca, cb = _window_sums(lambda o: cp[pad + o * BATCH:pad + o * BATCH + CTX_ROWS, :], ha, hb)
        pooled = jnp.where(is_a, ca / _count(tok, ha, CTX_LEN), cb / _count(tok, hb, CTX_LEN))
        m_ref[:, lanes] = pooled - v_ref[:, lanes]


def _pool_ctx(vc):
    return pl.pallas_call(
        _pool_ctx_kernel,
        out_shape=jax.ShapeDtypeStruct((CTX_ROWS, D_POOL), F32),
        scratch_shapes=[pltpu.VMEM((CTX_ROWS + 2 * POOL_PAD * BATCH, LANES), F32)],
        name="pool_ctx",
    )(vc)


def _pool_kernel(vprev_ref, v_ref, vnext_ref, m_ref, vp, sp):
    i = pl.program_id(0)
    g0 = i * POOL_GROWS
    pad = POOL_PAD * BATCH
    col = lax.broadcasted_iota(jnp.int32, (GROW, 1), 0) // BATCH
    have_prev = i > 0
    have_next = i < pl.num_programs(0) - 1
    for lanes, ha, hb, is_a in _pool_halves():
        vp[...] = jnp.zeros_like(vp)

        def fill(src_ref, first, keep, lanes=lanes):
            def body(q, _):
                dst = pl.multiple_of((first + q) * PROW + pad, SUBLANES)
                src = pl.multiple_of(q * GROW, SUBLANES)
                vp[pl.ds(dst, GROW), :] = jnp.where(keep, src_ref[pl.ds(src, GROW), lanes], 0.0)
                return 0
            return body

        lax.fori_loop(0, POOL_PAD, fill(vprev_ref, 0, have_prev), 0)
        lax.fori_loop(0, POOL_GROWS, fill(v_ref, POOL_PAD, True), 0)
        lax.fori_loop(0, POOL_PAD, fill(vnext_ref, POOL_PAD + POOL_GROWS, have_next), 0)
        ccnt_a, ccnt_b = _count(col, ha, GRID_W), _count(col, hb, GRID_W)

        def body(r, _, lanes=lanes, ha=ha, hb=hb, is_a=is_a, ccnt_a=ccnt_a, ccnt_b=ccnt_b):
            base = (r + POOL_PAD) * PROW
            sa, sb = _window_sums(
                lambda o: vp[pl.ds(pl.multiple_of(base + o * PROW, SUBLANES), PROW), :], ha, hb)
            sp[...] = jnp.where(is_a, sa, sb)
            ba, bb = _window_sums(lambda o: sp[pad + o * BATCH:pad + o * BATCH + GROW, :], ha, hb)
            rcnt_a, rcnt_b = _count(g0 + r, ha, GRID_H), _count(g0 + r, hb, GRID_H)
            pooled = jnp.where(is_a, ba / (rcnt_a * ccnt_a), bb / (rcnt_b * ccnt_b))
            src = pl.multiple_of(r * GROW, SUBLANES)
            m_ref[pl.ds(src, GROW), lanes] = pooled - v_ref[pl.ds(src, GROW), lanes]
            return 0

        lax.fori_loop(0, POOL_GROWS, body, 0)


def _pool(v):
    n = SEQ_ROWS // POOL_TILE
    per = POOL_TILE // POOL_HALO
    nh = SEQ_ROWS // POOL_HALO
    return pl.pallas_call(
        _pool_kernel,
        grid=(n,),
        in_specs=[pl.BlockSpec((POOL_HALO, D_POOL), lambda i: (jnp.maximum(i * per - 1, 0), 0)),
                  pl.BlockSpec((POOL_TILE, D_POOL), lambda i: (i, 0)),
                  pl.BlockSpec((POOL_HALO, D_POOL), lambda i: (jnp.minimum((i + 1) * per, nh - 1), 0))],
        out_specs=pl.BlockSpec((POOL_TILE, D_POOL), lambda i: (i, 0)),
        out_shape=jax.ShapeDtypeStruct((SEQ_ROWS, D_POOL), F32),
        scratch_shapes=[pltpu.VMEM(((POOL_GROWS + 2 * POOL_PAD) * PROW, LANES), F32),
                        pltpu.VMEM((PROW, LANES), F32)],
        compiler_params=pltpu.CompilerParams(
            dimension_semantics=("parallel",), vmem_limit_bytes=VMEM_LIMIT),
        name="pool",
    )(v, v, v)


def _tail_kernel(final, *refs):
    if final:
        gate_ref, hf_ref, hb_ref, u_ref, ysf_ref, ysb_ref, m_ref, x_ref, mod_ref = refs[:9]
        rest = refs[9:]
    else:
        gate_ref, hf_ref, hb_ref, u_ref, ysf_ref, ysb_ref, m_ref, mc_ref, x_ref, mod_ref = refs[:10]
        rest = refs[10:]
    (dvec_ref, wglu_ref, bglu_ref, pw_ref, pb_ref, ps_ref, wo_ref,
     g2_ref, wg_ref, wu_ref, wd_ref, fg_ref, o_ref) = rest[:13]
    dot = functools.partial(jnp.dot, preferred_element_type=F32)
    f32 = lambda ref: ref[...].astype(F32)

    y_a = jax.nn.gelu(f32(gate_ref)) * (f32(hf_ref) + f32(hb_ref))
    y_s = dvec_ref[...] * f32(u_ref) + f32(ysf_ref) + f32(ysb_ref)
    z = jax.nn.gelu(y_s)
    y_b = z * jax.nn.sigmoid(dot(z.astype(BF16), wglu_ref[...]) + bglu_ref[...])
    m = m_ref[...] if final else jnp.where(pl.program_id(0) < TAIL_CTX_TILES, mc_ref[...], m_ref[...])
    y_c = (dot(m.astype(BF16), pw_ref[...]) + pb_ref[...]) * ps_ref[...]
    y = jnp.concatenate([y_a.astype(BF16), y_b.astype(BF16), y_c.astype(BF16)], axis=1)
    x = x_ref[...] + _rows(mod_ref[2], TAIL_TILE) * dot(y, wo_ref[...])

    h = _rms(x) * g2_ref[...]
    h = (h * (1.0 + _rows(mod_ref[4], TAIL_TILE)) + _rows(mod_ref[3], TAIL_TILE)).astype(BF16)
    ff = None
    for c0, c1 in zip(FF_EDGES[:-1], FF_EDGES[1:]):
        cols = slice(c0, c1)
        act = (jax.nn.silu(dot(h, wg_ref[:, cols])) * dot(h, wu_ref[:, cols])).astype(BF16)
        part = dot(act, wd_ref[cols, :])
        ff = part if ff is None else ff + part
    y = x + _rows(mod_ref[5], TAIL_TILE) * ff
    if not final:
        o_ref[...] = y
        return
    y = _rms(y) * fg_ref[...]
    slab = rest[13]
    for s in range(D_MODEL // LANES):
        slab[s] = y[:, s * LANES:(s + 1) * LANES]
    for b in range(BATCH):
        for s in range(D_MODEL // LANES):
            o_ref[b, :, s * LANES:(s + 1) * LANES] = slab[s, pl.ds(b, TAIL_TILE // BATCH, stride=BATCH), :]


def _tail(final, layer, gate, hf, hb, u, ysf, ysb, m, mc, xt, modtab, weights):
    off = TAIL_CTX_TILES if final else 0
    n_tiles = N_ROWS // TAIL_TILE - off
    tok = lambda w: pl.BlockSpec((TAIL_TILE, w), lambda i: (i + off, 0))
    acts = [gate, hf, hb, u, ysf, ysb, m]
    act_specs = [tok(D_RNN), tok(D_RNN), tok(D_RNN), tok(D_SSM), tok(D_SSM), tok(D_SSM),
                 pl.BlockSpec((TAIL_TILE, D_POOL), lambda i: (jnp.maximum(i + off - TAIL_CTX_TILES, 0), 0))]
    if final:
        toks = TAIL_TILE // BATCH
        out_spec = pl.BlockSpec((BATCH, toks, D_MODEL), lambda i: (0, i, 0))
        out_shape = jax.ShapeDtypeStruct((BATCH, n_tiles * toks, D_MODEL), F32)
        scratch = [pltpu.VMEM((D_MODEL // LANES, TAIL_TILE, LANES), F32)]
    else:
        acts.append(mc)
        act_specs.append(pl.BlockSpec((TAIL_TILE, D_POOL), lambda i: (jnp.minimum(i, TAIL_CTX_TILES - 1), 0)))
        out_spec = pl.BlockSpec((TAIL_TILE, D_MODEL), lambda i: (i, 0))
        out_shape = jax.ShapeDtypeStruct((N_ROWS, D_MODEL), F32)
        scratch = []
    return pl.pallas_call(
        functools.partial(_tail_kernel, final),
        grid=(n_tiles,),
        in_specs=(act_specs + [tok(D_MODEL), _mod_spec(layer, off, TAIL_CTX_TILES)]
                  + [_const_spec(w.shape) if w.ndim == 2 else _layer_spec(w.shape, layer) for w in weights]),
        out_specs=out_spec,
        out_shape=out_shape,
        scratch_shapes=scratch,
        compiler_params=pltpu.CompilerParams(
            dimension_semantics=("parallel",), vmem_limit_bytes=VMEM_LIMIT),
        name="tail_final" if final else "tail",
    )(*acts, xt, modtab, *weights)


def _block_diag(w):
    n, di, dj = w.shape[-3:]
    out = jnp.einsum('...nij,nm->...nimj', w, jnp.eye(n, dtype=w.dtype))
    return out.reshape(w.shape[:-3] + (n * di, n * dj))


def _s5_params(lam_re, lam_im, log_dt, b_re, b_im, c_re, c_im):
    gpb = SSM_GROUPS // S5_BLOCKS
    depth = lam_re.shape[0]
    f32 = lambda t: t.astype(F32)

    in_blocks = lambda w: _block_diag(
        jnp.swapaxes(f32(w), -1, -2).reshape(depth, S5_BLOCKS, gpb, SSM_GROUP, SSM_STATE))
    out_blocks = lambda w: _block_diag(
        jnp.swapaxes(w, -1, -2).reshape(depth, 2, S5_BLOCKS, gpb, SSM_STATE, SSM_GROUP))

    bmat = jnp.concatenate([in_blocks(b_re), in_blocks(b_im)], axis=-1).astype(BF16)
    lr, li = f32(lam_re), f32(lam_im)
    dt = jnp.exp(f32(log_dt))[..., None]
    mag = jnp.exp(lr * dt)
    ang = li * dt
    bar_r, bar_i = mag * jnp.cos(ang), mag * jnp.sin(ang)
    den = lr * lr + li * li
    fr = ((bar_r - 1.0) * lr + bar_i * li) / den
    fi = (bar_i * lr - (bar_r - 1.0) * li) / den
    cr, ci = f32(c_re), f32(c_im)
    cfr = cr * fr[..., None, :] - ci * fi[..., None, :]
    cfi = cr * fi[..., None, :] + ci * fr[..., None, :]
    cm = jnp.concatenate([out_blocks(cfr), out_blocks(-cfi)], axis=-2)
    cmat = jnp.concatenate([cm[:, 0], cm[:, 1]], axis=-1).astype(BF16)
    lam = jnp.stack([bar_r.reshape(depth, 2, N_STATE), bar_i.reshape(depth, 2, N_STATE)], axis=1)
    lam = jnp.repeat(lam, HALF, axis=2)
    return bmat, cmat, lam


def kernel(x, c, ctx, c_ctx, w_mod, b_mod, norm1_g, norm2_g, w_in, w_out, lru_conv_w, lru_conv_b,
           lru_wa, lru_ba, lru_wi, lru_bi, lru_lambda, s5_lambda_re, s5_lambda_im, s5_log_dt,
           s5_b_re, s5_b_im, s5_c_re, s5_c_im, s5_d, s5_glu_w, s5_glu_b, pool_w, pool_b, pool_scale,
           ffn_w_gate, ffn_w_up, ffn_w_down, final_g):
    cc = jnp.zeros((SUBLANES, D_MODEL), F32).at[:BATCH].set(c).at[BATCH].set(c_ctx)
    mod = _modulation(cc, w_mod, b_mod)
    xt = (ctx, x)
    fg = final_g.reshape(1, D_MODEL)
    w_in_b, w_out_b, glu_b = w_in.astype(BF16), w_out.astype(BF16), s5_glu_w.astype(BF16)
    ffn_gate_b, ffn_up_b, ffn_down_b = (w.astype(BF16) for w in (ffn_w_gate, ffn_w_up, ffn_w_down))

    row = lambda t: t.reshape(t.shape[:-1] + (1, t.shape[-1]))
    lat = jnp.transpose(mod[:, :BATCH].reshape(DEPTH, BATCH, 6, D_MODEL), (0, 2, 1, 3))
    lat = jnp.concatenate([lat, lat], axis=2)
    cm = jnp.broadcast_to(mod[:, BATCH].reshape(DEPTH, 6, 1, D_MODEL), (DEPTH, 6, SUBLANES, D_MODEL))
    modtab = jnp.stack([cm, lat], axis=1)
    heads = lru_wa.shape[2]
    pairs = lambda w: _block_diag(w.reshape(DEPTH, 2, D_RNN // LANES, heads * LANES // D_RNN,
                                            D_RNN // heads, D_RNN // heads))
    wg = (-LOG2_E * jnp.concatenate([pairs(lru_wa), pairs(lru_wi)], axis=-1)).astype(BF16)
    bg = -LOG2_E * row(jnp.concatenate([lru_ba, lru_bi], axis=-1))
    spl = row(-LRU_C * LOG2_E * jax.nn.softplus(-lru_lambda.astype(F32)))
    scan_params = (lru_conv_w, row(lru_conv_b), wg, bg, spl) + _s5_params(
        s5_lambda_re, s5_lambda_im, s5_log_dt, s5_b_re, s5_b_im, s5_c_re, s5_c_im)
    weights = (row(s5_d), glu_b, row(s5_glu_b), _block_diag(pool_w).astype(BF16), row(pool_b),
               row(pool_scale), w_out_b, row(norm2_g), ffn_gate_b, ffn_up_b, ffn_down_b, fg)
    g1 = row(norm1_g)

    for l in range(DEPTH):
        last = l == DEPTH - 1
        xt, ax, u, gate, v, vc = _inproj(xt, modtab, g1, w_in_b, l)
        hf, ysf, hb, ysb = _scan(l, ax, u, *scan_params)
        m = _pool(v)
        mc = None if last else _pool_ctx(vc)
        xt = _tail(last, l, gate, hf, hb, u, ysf, ysb, m, mc, xt, modtab, weights)
    return xt
```

```python
import functools

import jax
import jax.numpy as jnp
from jax import lax
from jax.experimental import pallas as pl
from jax.experimental.pallas import tpu as pltpu

F32 = jnp.float32
BF16 = jnp.bfloat16

D_MODEL = 1024
BATCH = 4
SEQ = 8192
DEPTH = 2
CTX_LEN = 256
GRID_W = 64
GRID_H = SEQ // GRID_W
EPS = 1e-6

D_RNN = 384
CONV_WIDTH = 4
LRU_C = 8.0
LOG2_E = 1.4426950408889634

D_SSM = 384
SSM_GROUP = 16
SSM_GROUPS = D_SSM // SSM_GROUP
SSM_STATE = 64
N_STATE = SSM_GROUPS * SSM_STATE

D_POOL = 256
POOL_WINDOWS = (2, 4, 8, 16)
POOL_GROUP = D_POOL // len(POOL_WINDOWS)
POOL_PAD = 8

D_IN = 2 * D_RNN + D_SSM + D_POOL
D_MIX = D_RNN + D_SSM + D_POOL
D_FF = 2816
MXU_DIM = 256
FF_EDGES = (0, 3 * MXU_DIM, 6 * MXU_DIM, 9 * MXU_DIM, D_FF)

LANES = 128
SUBLANES = 8
HALF = SUBLANES // 2
assert BATCH == HALF

N_TOK = CTX_LEN + SEQ
N_ROWS = N_TOK * BATCH
CTX_ROWS = CTX_LEN * BATCH
SEQ_ROWS = SEQ * BATCH

ROW_TILE = 1024
N_ROW_TILES = N_ROWS // ROW_TILE
assert CTX_ROWS == ROW_TILE
TAIL_TILE = 1024
TAIL_CTX_TILES = CTX_ROWS // TAIL_TILE
TAIL_SLAB = 512

SCAN_TOK = 128
SCAN_ROWS = SCAN_TOK * BATCH
SCAN_TILES = N_TOK // SCAN_TOK
SCAN_CTX_TILES = CTX_LEN // SCAN_TOK
SCAN_BLKS = SCAN_ROWS // SUBLANES
S5_BLOCKS = D_SSM // LANES
S5_BLK_STATES = N_STATE // S5_BLOCKS
SCAN_SEGS = 4

POOL_GROWS = 16
POOL_TILE = POOL_GROWS * GRID_W * BATCH
POOL_HALO = POOL_PAD * GRID_W * BATCH
GROW = GRID_W * BATCH
PROW = (GRID_W + 2 * POOL_PAD) * BATCH

VMEM_LIMIT = 56 * 1024 * 1024
TAIL_VMEM_LIMIT = 60 * 1024 * 1024


def _const_spec(shape):
    nd = len(shape)
    return pl.BlockSpec(shape, lambda *_: (0,) * nd, pipeline_mode=pl.Buffered(1))


def _layer_spec(shape, layer):
    nd = len(shape) - 1
    return pl.BlockSpec((None,) + tuple(shape[1:]), lambda *_: (layer,) + (0,) * nd,
                        pipeline_mode=pl.Buffered(1))


def _rms(x):
    return x * lax.rsqrt(jnp.mean(x * x, axis=-1, keepdims=True) + EPS)


def _rows(pat, n):
    return jnp.broadcast_to(pat[None], (n // SUBLANES,) + pat.shape).reshape(n, pat.shape[-1])


def _mod_spec(layer, off=0, ctx_tiles=1):
    return pl.BlockSpec((None, None, 6, SUBLANES, D_MODEL),
                        lambda i: (layer, jnp.minimum((i + off) // ctx_tiles, 1), 0, 0, 0))


def _mod_kernel(c_ref, w_ref, b_ref, o_ref):
    s = jax.nn.silu(c_ref[...])
    o_ref[...] = jnp.dot(s.astype(BF16), w_ref[...].astype(BF16),
                         preferred_element_type=F32) + b_ref[...]


def _modulation(cc, w_mod, b_mod):
    n_col = 6 * D_MODEL
    blk = 1536
    return pl.pallas_call(
        _mod_kernel,
        grid=(DEPTH, n_col // blk),
        in_specs=[pl.BlockSpec((SUBLANES, D_MODEL), lambda l, j: (0, 0)),
                  pl.BlockSpec((None, D_MODEL, blk), lambda l, j: (l, 0, j)),
                  pl.BlockSpec((None, 1, blk), lambda l, j: (l, 0, j))],
        out_specs=pl.BlockSpec((None, SUBLANES, blk), lambda l, j: (l, 0, j)),
        out_shape=jax.ShapeDtypeStruct((DEPTH, SUBLANES, n_col), F32),
        compiler_params=pltpu.CompilerParams(vmem_limit_bytes=VMEM_LIMIT),
        name="modulation",
    )(cc, w_mod, b_mod.reshape(DEPTH, 1, n_col))


def _interleave(src_ref, slab):
    for b in range(BATCH):
        for s in range(D_MODEL // LANES):
            slab[s, pl.ds(b, ROW_TILE // BATCH, stride=BATCH), :] = src_ref[b, :, s * LANES:(s + 1) * LANES]


def _inproj_body(x, mod_ref, g_ref, w_ref, ax_ref, u_ref, gate_ref, v_ref, vc_ref):
    h = _rms(x) * g_ref[...]
    h = h * (1.0 + _rows(mod_ref[1], ROW_TILE)) + _rows(mod_ref[0], ROW_TILE)
    p = jnp.dot(h.astype(BF16), w_ref[...], preferred_element_type=F32)
    ax_ref[...] = p[:, 0:D_RNN]
    u_ref[...] = p[:, D_RNN:D_RNN + D_SSM].astype(BF16)
    gate_ref[...] = p[:, D_RNN + D_SSM:2 * D_RNN + D_SSM].astype(BF16)
    v = p[:, 2 * D_RNN + D_SSM:]
    v_ref[...] = v

    @pl.when(pl.program_id(0) == 0)
    def _():
        vc_ref[...] = v


def _inproj_kernel(x_ref, *rest):
    _inproj_body(x_ref[...], *rest)


def _inproj_first_kernel(ctx_ref, x_ref, mod_ref, g_ref, w_ref, xt_ref, *rest):
    slab = rest[-1]
    i = pl.program_id(0)

    @pl.when(i == 0)
    def _():
        _interleave(ctx_ref, slab)

    @pl.when(i > 0)
    def _():
        _interleave(x_ref, slab)

    x = jnp.concatenate([slab[s] for s in range(D_MODEL // LANES)], axis=1)
    xt_ref[...] = x
    _inproj_body(x, mod_ref, g_ref, w_ref, *rest[:-1])


def _inproj(xt, modtab, g, w_in, layer):
    first = isinstance(xt, tuple)
    tok = lambda w: pl.BlockSpec((ROW_TILE, w), lambda i: (i, 0))
    out = lambda w, dt=F32: jax.ShapeDtypeStruct((N_ROWS, w), dt)
    out_specs = [tok(D_RNN), tok(D_SSM), tok(D_RNN),
                 pl.BlockSpec((ROW_TILE, D_POOL), lambda i: (jnp.maximum(i - 1, 0), 0)),
                 pl.BlockSpec((CTX_ROWS, D_POOL), lambda i: (0, 0))]
    out_shape = [out(D_RNN), out(D_SSM, BF16), out(D_RNN, BF16),
                 jax.ShapeDtypeStruct((SEQ_ROWS, D_POOL), F32),
                 jax.ShapeDtypeStruct((CTX_ROWS, D_POOL), F32)]
    common = [_mod_spec(layer), _layer_spec(g.shape, layer), _layer_spec(w_in.shape, layer)]
    params = pltpu.CompilerParams(dimension_semantics=("arbitrary",), vmem_limit_bytes=VMEM_LIMIT)
    if not first:
        return (xt,) + tuple(pl.pallas_call(
            _inproj_kernel, grid=(N_ROW_TILES,), in_specs=[tok(D_MODEL)] + common,
            out_specs=out_specs, out_shape=out_shape, compiler_params=params, name="inproj",
        )(xt, modtab, g, w_in))
    toks = ROW_TILE // BATCH
    src = [pl.BlockSpec((BATCH, CTX_LEN, D_MODEL), lambda i: (0, 0, 0), pipeline_mode=pl.Buffered(1)),
           pl.BlockSpec((BATCH, toks, D_MODEL), lambda i: (0, jnp.maximum(i - 1, 0), 0))]
    return tuple(pl.pallas_call(
        _inproj_first_kernel, grid=(N_ROW_TILES,), in_specs=src + common,
        out_specs=[tok(D_MODEL)] + out_specs, out_shape=[out(D_MODEL)] + out_shape,
        scratch_shapes=[pltpu.VMEM((D_MODEL // LANES, ROW_TILE, LANES), F32)],
        compiler_params=params, name="inproj_first",
    )(xt[0], xt[1], modtab, g, w_in))


def _fwd_tile(i):
    return i


def _bwd_tile(i):
    return jnp.where(i < SCAN_CTX_TILES, SCAN_CTX_TILES - 1 - i, SCAN_TILES + SCAN_CTX_TILES - 1 - i)


def _scan_kernel(axp_f, ax_f, axn_f, u_f, axp_b, ax_b, axn_b, u_b,
                 cw_ref, cb_ref, wg_ref, bg_ref, spl_ref, bmat_ref, cmat_ref, lam_ref,
                 hf_ref, ysf_ref, hb_ref, ysb_ref,
                 axs_f, axs_b, a_f, b_f, a_b, b_b, uf32, ub32, us1, us2, xs1, xs2, ys1, ys2,
                 hfs, hbs, yfs, ybs, lcar, scar):
    i = pl.program_id(0)
    uf32[...] = u_f[...].astype(F32)
    ub32[...] = u_b[...].astype(F32)

    @pl.when(i == 0)
    def _():
        lcar[...] = jnp.zeros_like(lcar)
        scar[...] = jnp.zeros_like(scar)

    def stage_conv(tile, axp, ax, axn, axs):
        prev_ok = jnp.logical_and(tile != 0, tile != SCAN_CTX_TILES)
        next_ok = jnp.logical_and(tile != SCAN_CTX_TILES - 1, tile != SCAN_TILES - 1)
        axs[0:SUBLANES, :] = jnp.where(prev_ok, axp[...], 0.0)
        axs[SUBLANES:SUBLANES + SCAN_ROWS, :] = ax[...]
        axs[SUBLANES + SCAN_ROWS:, :] = jnp.where(next_ok, axn[...], 0.0)

    def gates(d, k, axs, a_out, b_out):
        lanes = slice(k * LANES, (k + 1) * LANES)
        xc = cb_ref[:, lanes] + axs[0:SCAN_ROWS, lanes] * cw_ref[0:1, lanes]
        for t in range(1, CONV_WIDTH):
            xc = xc + axs[t * BATCH:t * BATCH + SCAN_ROWS, lanes] * cw_ref[t:t + 1, lanes]
        g = jnp.dot(xc.astype(BF16), wg_ref[d, k], preferred_element_type=F32)
        r = 1.0 / (1.0 + jnp.exp2(g[:, :LANES] + bg_ref[d, :, lanes]))
        ig = 1.0 / (1.0 + jnp.exp2(g[:, LANES:] + bg_ref[d, :, D_RNN + k * LANES:D_RNN + (k + 1) * LANES]))
        a = jnp.exp2(r * spl_ref[d, :, lanes])
        a_out[:, lanes] = a
        b_out[:, lanes] = jnp.sqrt(1.0 - a * a) * (ig * xc)

    lo = lax.broadcasted_iota(jnp.int32, (SUBLANES, 1), 0) < HALF

    def swap(x):
        return pltpu.roll(x, HALF, 0)

    def rows_of(j):
        jb = SCAN_BLKS - 1 - j
        return slice(j * SUBLANES, (j + 1) * SUBLANES), slice(jb * SUBLANES, (jb + 1) * SUBLANES)

    def steps(vf, vb):
        return jnp.where(lo, vf, vb), swap(jnp.where(lo, vb, vf))

    def unsteps(h1, h2):
        h2s = swap(h2)
        return jnp.where(lo, h1, h2s), jnp.where(lo, h2s, h1)

    def lru_scan(h, j0, j1):
        for j in range(j0, j1):
            rf, rb = rows_of(j)
            a1, a2 = steps(a_f[rf, :], a_b[rb, :])
            b1, b2 = steps(b_f[rf, :], b_b[rb, :])
            h1 = a1 * h + b1
            h = a2 * h1 + b2
            hfs[rf, :], hbs[rb, :] = unsteps(h1, h)
        return h

    for j in range(SCAN_BLKS):
        rf, rb = rows_of(j)
        us1[rf, :], us2[rf, :] = steps(uf32[rf, :], ub32[rb, :])
    stage_conv(_fwd_tile(i), axp_f, ax_f, axn_f, axs_f)
    stage_conv(_bwd_tile(i), axp_b, ax_b, axn_b, axs_b)

    nst = S5_BLK_STATES

    def s5_scan(jb, carry, j0, j1):
        x1, x2 = xs1.at[jb], xs2.at[jb]
        re, im = slice(0, nst), slice(nst, 2 * nst)
        lr, li = lam_ref[0, :, jb * nst:(jb + 1) * nst], lam_ref[1, :, jb * nst:(jb + 1) * nst]
        sr, si = carry
        for j in range(j0, j1):
            rows = slice(j * SUBLANES, (j + 1) * SUBLANES)
            r1 = lr * sr - li * si + x1[rows, re]
            i1 = lr * si + li * sr + x1[rows, im]
            sr = lr * r1 - li * i1 + x2[rows, re]
            si = lr * i1 + li * r1 + x2[rows, im]
            x1[rows, re] = r1
            x1[rows, im] = i1
            x2[rows, re] = sr
            x2[rows, im] = si
        return sr, si

    def out_proj(jb, piece):
        xs, ys = ((xs1, ys1), (xs2, ys2))[piece // 2]
        half = slice(0, nst) if piece % 2 == 0 else slice(nst, 2 * nst)
        cols = slice(2 * jb * LANES, 2 * (jb + 1) * LANES)
        y = jnp.dot(xs[jb, :, half].astype(BF16), cmat_ref[jb, half, :], preferred_element_type=F32)
        ys[:, cols] = y if piece % 2 == 0 else ys[:, cols] + y

    for jb in range(S5_BLOCKS):
        blk = slice(jb * LANES, (jb + 1) * LANES)
        gates(0, jb, axs_f, a_f, b_f)
        xs1[jb] = jnp.dot(us1[:, blk].astype(BF16), bmat_ref[jb], preferred_element_type=F32)
        gates(1, jb, axs_b, a_b, b_b)
        xs2[jb] = jnp.dot(us2[:, blk].astype(BF16), bmat_ref[jb], preferred_element_type=F32)
    seg = SCAN_BLKS // SCAN_SEGS
    for jb in range(S5_BLOCKS):
        carry = (scar[:, jb * nst:(jb + 1) * nst], scar[:, N_STATE + jb * nst:N_STATE + (jb + 1) * nst])
        for q in range(SCAN_SEGS):
            carry = s5_scan(jb, carry, q * seg, (q + 1) * seg)
            if jb > 0:
                out_proj(jb - 1, q)
        scar[:, jb * nst:(jb + 1) * nst] = carry[0]
        scar[:, N_STATE + jb * nst:N_STATE + (jb + 1) * nst] = carry[1]
    h = lcar[...]
    for q in range(SCAN_SEGS):
        h = lru_scan(h, q * seg, (q + 1) * seg)
        out_proj(S5_BLOCKS - 1, q)
    lcar[...] = h
    hf_ref[...] = hfs[...].astype(BF16)
    hb_ref[...] = hbs[...].astype(BF16)

    for j in range(SCAN_BLKS):
        rf, rb = rows_of(j)
        of, ob = unsteps(ys1[rf, :], ys2[rf, :])
        for jb in range(S5_BLOCKS):
            yfs[rf, jb * LANES:(jb + 1) * LANES] = of[:, 2 * jb * LANES:(2 * jb + 1) * LANES]
            ybs[rb, jb * LANES:(jb + 1) * LANES] = ob[:, (2 * jb + 1) * LANES:2 * (jb + 1) * LANES]
    ysf_ref[...] = yfs[...].astype(BF16)
    ysb_ref[...] = ybs[...].astype(BF16)


def _scan(layer, ax, u, *params):
    tpb = SCAN_ROWS // SUBLANES
    nb8 = N_ROWS // SUBLANES

    def specs(tile_of):
        tok = lambda w: pl.BlockSpec((SCAN_ROWS, w), lambda i: (tile_of(i), 0))
        prev = pl.BlockSpec((SUBLANES, D_RNN), lambda i: (jnp.maximum(tile_of(i) * tpb - 1, 0), 0))
        nxt = pl.BlockSpec((SUBLANES, D_RNN), lambda i: (jnp.minimum((tile_of(i) + 1) * tpb, nb8 - 1), 0))
        return tok, prev, nxt

    tok_f, prev_f, next_f = specs(_fwd_tile)
    tok_b, prev_b, next_b = specs(_bwd_tile)
    out = jax.ShapeDtypeStruct((N_ROWS, D_RNN), BF16)
    tile_f32 = lambda w: pltpu.VMEM((SCAN_ROWS, w), F32)
    return pl.pallas_call(
        _scan_kernel,
        grid=(SCAN_TILES,),
        in_specs=[prev_f, tok_f(D_RNN), next_f, tok_f(D_SSM),
                  prev_b, tok_b(D_RNN), next_b, tok_b(D_SSM)] + [_layer_spec(p.shape, layer) for p in params],
        out_specs=[tok_f(D_RNN), tok_f(D_SSM), tok_b(D_RNN), tok_b(D_SSM)],
        out_shape=[out, out, out, out],
        scratch_shapes=[pltpu.VMEM((SCAN_ROWS + 2 * SUBLANES, D_RNN), F32),
                        pltpu.VMEM((SCAN_ROWS + 2 * SUBLANES, D_RNN), F32),
                        tile_f32(D_RNN), tile_f32(D_RNN), tile_f32(D_RNN), tile_f32(D_RNN),
                        tile_f32(D_SSM), tile_f32(D_SSM), tile_f32(D_SSM), tile_f32(D_SSM),
                        pltpu.VMEM((S5_BLOCKS, SCAN_ROWS, 2 * S5_BLK_STATES), F32),
                        pltpu.VMEM((S5_BLOCKS, SCAN_ROWS, 2 * S5_BLK_STATES), F32),
                        tile_f32(2 * D_SSM), tile_f32(2 * D_SSM),
                        tile_f32(D_RNN), tile_f32(D_RNN), tile_f32(D_SSM), tile_f32(D_SSM),
                        pltpu.VMEM((SUBLANES, D_RNN), F32),
                        pltpu.VMEM((SUBLANES, 2 * N_STATE), F32)],
        compiler_params=pltpu.CompilerParams(
            dimension_semantics=("arbitrary",), vmem_limit_bytes=VMEM_LIMIT),
        name="scan",
    )(ax, ax, ax, u, ax, ax, ax, u, *params)


def _window_sums(load, ha, hb):
    sa = load(-ha)
    for o in range(-ha + 1, ha):
        sa = sa + load(o)
    sb = sa
    for o in list(range(-hb, -ha)) + list(range(ha, hb)):
        sb = sb + load(o)
    return sa, sb


def _count(pos, half, n):
    return (jnp.minimum(pos + half, n) - jnp.maximum(pos - half, 0)).astype(F32)


def _pool_halves():
    is_a = lax.broadcasted_iota(jnp.int32, (1, LANES), 1) < POOL_GROUP
    for half in range(D_POOL // LANES):
        wa, wb = POOL_WINDOWS[2 * half], POOL_WINDOWS[2 * half + 1]
        yield slice(half * LANES, (half + 1) * LANES), wa // 2, wb // 2, is_a


def _pool_ctx_kernel(v_ref, m_ref, cp):
    pad = POOL_PAD * BATCH
    tok = lax.broadcasted_iota(jnp.int32, (CTX_ROWS, 1), 0) // BATCH
    for lanes, ha, hb, is_a in _pool_halves():
        cp[...] = jnp.zeros_like(cp)
        cp[pad:pad + CTX_ROWS, :] = v_ref[:, lanes]
        ca, cb = _window_sums(lambda o: cp[pad + o * BATCH:pad + o * BATCH + CTX_ROWS, :], ha, hb)
        pooled = jnp.where(is_a, ca / _count(tok, ha, CTX_LEN), cb / _count(tok, hb, CTX_LEN))
        m_ref[:, lanes] = (pooled - v_ref[:, lanes]).astype(BF16)


def _pool_ctx(vc):
    return pl.pallas_call(
        _pool_ctx_kernel,
        out_shape=jax.ShapeDtypeStruct((CTX_ROWS, D_POOL), BF16),
        scratch_shapes=[pltpu.VMEM((CTX_ROWS + 2 * POOL_PAD * BATCH, LANES), F32)],
        name="pool_ctx",
    )(vc)


def _pool_kernel(vprev_ref, v_ref, vnext_ref, m_ref, vp, sp):
    i = pl.program_id(0)
    g0 = i * POOL_GROWS
    pad = POOL_PAD * BATCH
    col = lax.broadcasted_iota(jnp.int32, (GROW, 1), 0) // BATCH
    have_prev = i > 0
    have_next = i < pl.num_programs(0) - 1
    for lanes, ha, hb, is_a in _pool_halves():
        vp[...] = jnp.zeros_like(vp)

        def fill(src_ref, first, keep, lanes=lanes):
            def body(q, _):
                dst = pl.multiple_of((first + q) * PROW + pad, SUBLANES)
                src = pl.multiple_of(q * GROW, SUBLANES)
                vp[pl.ds(dst, GROW), :] = jnp.where(keep, src_ref[pl.ds(src, GROW), lanes], 0.0)
                return 0
            return body

        lax.fori_loop(0, POOL_PAD, fill(vprev_ref, 0, have_prev), 0)
        lax.fori_loop(0, POOL_GROWS, fill(v_ref, POOL_PAD, True), 0)
        lax.fori_loop(0, POOL_PAD, fill(vnext_ref, POOL_PAD + POOL_GROWS, have_next), 0)
        ccnt_a, ccnt_b = _count(col, ha, GRID_W), _count(col, hb, GRID_W)

        def body(r, _, lanes=lanes, ha=ha, hb=hb, is_a=is_a, ccnt_a=ccnt_a, ccnt_b=ccnt_b):
            base = (r + POOL_PAD) * PROW
            sa, sb = _window_sums(
                lambda o: vp[pl.ds(pl.multiple_of(base + o * PROW, SUBLANES), PROW), :], ha, hb)
            sp[...] = jnp.where(is_a, sa, sb)
            ba, bb = _window_sums(lambda o: sp[pad + o * BATCH:pad + o * BATCH + GROW, :], ha, hb)
            rcnt_a, rcnt_b = _count(g0 + r, ha, GRID_H), _count(g0 + r, hb, GRID_H)
            pooled = jnp.where(is_a, ba / (rcnt_a * ccnt_a), bb / (rcnt_b * ccnt_b))
            src = pl.multiple_of(r * GROW, GROW)
            m_ref[pl.ds(src, GROW), lanes] = (pooled - v_ref[pl.ds(src, GROW), lanes]).astype(BF16)
            return 0

        lax.fori_loop(0, POOL_GROWS, body, 0)


def _pool(v):
    n = SEQ_ROWS // POOL_TILE
    per = POOL_TILE // POOL_HALO
    nh = SEQ_ROWS // POOL_HALO
    return pl.pallas_call(
        _pool_kernel,
        grid=(n,),
        in_specs=[pl.BlockSpec((POOL_HALO, D_POOL), lambda i: (jnp.maximum(i * per - 1, 0), 0)),
                  pl.BlockSpec((POOL_TILE, D_POOL), lambda i: (i, 0)),
                  pl.BlockSpec((POOL_HALO, D_POOL), lambda i: (jnp.minimum((i + 1) * per, nh - 1), 0))],
        out_specs=pl.BlockSpec((POOL_TILE, D_POOL), lambda i: (i, 0)),
        out_shape=jax.ShapeDtypeStruct((SEQ_ROWS, D_POOL), BF16),
        scratch_shapes=[pltpu.VMEM(((POOL_GROWS + 2 * POOL_PAD) * PROW, LANES), F32),
                        pltpu.VMEM((PROW, LANES), F32)],
        compiler_params=pltpu.CompilerParams(
            dimension_semantics=("parallel",), vmem_limit_bytes=VMEM_LIMIT),
        name="pool",
    )(v, v, v)


def _tail_kernel(final, *refs):
    if final:
        gate_ref, hf_ref, hb_ref, u_ref, ysf_ref, ysb_ref, m_ref, x_ref, mod_ref = refs[:9]
        rest = refs[9:]
    else:
        gate_ref, hf_ref, hb_ref, u_ref, ysf_ref, ysb_ref, m_ref, mc_ref, x_ref, mod_ref = refs[:10]
        rest = refs[10:]
    (dvec_ref, wglu_ref, bglu_ref, pw_ref, pb_ref, ps_ref, wo_ref,
     g2_ref, wg_ref, wu_ref, wd_ref, fg_ref, o_ref) = rest[:13]
    dot = functools.partial(jnp.dot, preferred_element_type=F32)
    f32 = lambda ref: ref[...].astype(F32)

    y_a = jax.nn.gelu(f32(gate_ref)) * (f32(hf_ref) + f32(hb_ref))
    y_s = dvec_ref[...] * f32(u_ref) + f32(ysf_ref) + f32(ysb_ref)
    z = jax.nn.gelu(y_s)
    y_b = z * jax.nn.sigmoid(dot(z.astype(BF16), wglu_ref[...]) + bglu_ref[...])
    m = m_ref[...] if final else jnp.where(pl.program_id(0) < TAIL_CTX_TILES, mc_ref[...], m_ref[...])
    y_c = (dot(m.astype(BF16), pw_ref[...]) + pb_ref[...]) * ps_ref[...]
    y = jnp.concatenate([y_a.astype(BF16), y_b.astype(BF16), y_c.astype(BF16)], axis=1)
    x = x_ref[...] + _rows(mod_ref[2], TAIL_TILE) * dot(y, wo_ref[...])

    h = _rms(x) * g2_ref[...]
    h = (h * (1.0 + _rows(mod_ref[4], TAIL_TILE)) + _rows(mod_ref[3], TAIL_TILE)).astype(BF16)
    ff = None
    for c0, c1 in zip(FF_EDGES[:-1], FF_EDGES[1:]):
        cols = slice(c0, c1)
        act = (jax.nn.silu(dot(h, wg_ref[:, cols])) * dot(h, wu_ref[:, cols])).astype(BF16)
        part = dot(act, wd_ref[cols, :])
        ff = part if ff is None else ff + part
    y = x + _rows(mod_ref[5], TAIL_TILE) * ff
    if not final:
        o_ref[...] = y
        return
    y = _rms(y) * fg_ref[...]
    slab = rest[13]
    rows, toks = slab.shape[1], slab.shape[1] // BATCH
    for part in range(TAIL_TILE // rows):
        for s in range(D_MODEL // LANES):
            slab[s] = y[part * rows:(part + 1) * rows, s * LANES:(s + 1) * LANES]
        for b in range(BATCH):
            for s in range(D_MODEL // LANES):
                o_ref[b, part * toks:(part + 1) * toks, s * LANES:(s + 1) * LANES] = (
                    slab[s, pl.ds(b, toks, stride=BATCH), :])


def _tail(final, layer, gate, hf, hb, u, ysf, ysb, m, mc, xt, modtab, weights):
    off = TAIL_CTX_TILES if final else 0
    n_tiles = N_ROWS // TAIL_TILE - off
    tok = lambda w: pl.BlockSpec((TAIL_TILE, w), lambda i: (i + off, 0))
    acts = [gate, hf, hb, u, ysf, ysb, m]
    act_specs = [tok(D_RNN), tok(D_RNN), tok(D_RNN), tok(D_SSM), tok(D_SSM), tok(D_SSM),
                 pl.BlockSpec((TAIL_TILE, D_POOL), lambda i: (jnp.maximum(i + off - TAIL_CTX_TILES, 0), 0))]
    if final:
        toks = TAIL_TILE // BATCH
        out_spec = pl.BlockSpec((BATCH, toks, D_MODEL), lambda i: (0, i, 0), pipeline_mode=pl.Buffered(1))
        out_shape = jax.ShapeDtypeStruct((BATCH, n_tiles * toks, D_MODEL), F32)
        scratch = [pltpu.VMEM((D_MODEL // LANES, TAIL_SLAB, LANES), F32)]
    else:
        acts.append(mc)
        act_specs.append(pl.BlockSpec((TAIL_TILE, D_POOL), lambda i: (jnp.minimum(i, TAIL_CTX_TILES - 1), 0)))
        out_spec = pl.BlockSpec((TAIL_TILE, D_MODEL), lambda i: (i, 0), pipeline_mode=pl.Buffered(1))
        out_shape = jax.ShapeDtypeStruct((N_ROWS, D_MODEL), F32)
        scratch = []
    return pl.pallas_call(
        functools.partial(_tail_kernel, final),
        grid=(n_tiles,),
        in_specs=(act_specs + [tok(D_MODEL), _mod_spec(layer, off, TAIL_CTX_TILES)]
                  + [_const_spec(w.shape) if w.ndim == 2 else _layer_spec(w.shape, layer) for w in weights]),
        out_specs=out_spec,
        out_shape=out_shape,
        scratch_shapes=scratch,
        compiler_params=pltpu.CompilerParams(
            dimension_semantics=("parallel",), vmem_limit_bytes=TAIL_VMEM_LIMIT),
        name="tail_final" if final else "tail",
    )(*acts, xt, modtab, *weights)


def _block_diag(w):
    n, di, dj = w.shape[-3:]
    out = jnp.einsum('...nij,nm->...nimj', w, jnp.eye(n, dtype=w.dtype))
    return out.reshape(w.shape[:-3] + (n * di, n * dj))


def _s5_params(lam_re, lam_im, log_dt, b_re, b_im, c_re, c_im):
    gpb = SSM_GROUPS // S5_BLOCKS
    depth = lam_re.shape[0]
    f32 = lambda t: t.astype(F32)

    in_blocks = lambda w: _block_diag(
        jnp.swapaxes(f32(w), -1, -2).reshape(depth, S5_BLOCKS, gpb, SSM_GROUP, SSM_STATE))
    out_blocks = lambda w: _block_diag(
        jnp.swapaxes(w, -1, -2).reshape(depth, 2, S5_BLOCKS, gpb, SSM_STATE, SSM_GROUP))

    bmat = jnp.concatenate([in_blocks(b_re), in_blocks(b_im)], axis=-1).astype(BF16)
    lr, li = f32(lam_re), f32(lam_im)
    dt = jnp.exp(f32(log_dt))[..., None]
    mag = jnp.exp(lr * dt)
    ang = li * dt
    bar_r, bar_i = mag * jnp.cos(ang), mag * jnp.sin(ang)
    den = lr * lr + li * li
    fr = ((bar_r - 1.0) * lr + bar_i * li) / den
    fi = (bar_i * lr - (bar_r - 1.0) * li) / den
    cr, ci = f32(c_re), f32(c_im)
    cfr = cr * fr[..., None, :] - ci * fi[..., None, :]
    cfi = cr * fi[..., None, :] + ci * fr[..., None, :]
    cm = jnp.concatenate([out_blocks(cfr), out_blocks(-cfi)], axis=-2)
    cmat = jnp.concatenate([cm[:, 0], cm[:, 1]], axis=-1).astype(BF16)
    lam = jnp.stack([bar_r.reshape(depth, 2, N_STATE), bar_i.reshape(depth, 2, N_STATE)], axis=1)
    lam = jnp.repeat(lam, HALF, axis=2)
    return bmat, cmat, lam


def kernel(x, c, ctx, c_ctx, w_mod, b_mod, norm1_g, norm2_g, w_in, w_out, lru_conv_w, lru_conv_b,
           lru_wa, lru_ba, lru_wi, lru_bi, lru_lambda, s5_lambda_re, s5_lambda_im, s5_log_dt,
           s5_b_re, s5_b_im, s5_c_re, s5_c_im, s5_d, s5_glu_w, s5_glu_b, pool_w, pool_b, pool_scale,
           ffn_w_gate, ffn_w_up, ffn_w_down, final_g):
    cc = jnp.zeros((SUBLANES, D_MODEL), F32).at[:BATCH].set(c).at[BATCH].set(c_ctx)
    mod = _modulation(cc, w_mod, b_mod)
    xt = (ctx, x)
    fg = final_g.reshape(1, D_MODEL)
    w_in_b, w_out_b, glu_b = w_in.astype(BF16), w_out.astype(BF16), s5_glu_w.astype(BF16)
    ffn_gate_b, ffn_up_b, ffn_down_b = (w.astype(BF16) for w in (ffn_w_gate, ffn_w_up, ffn_w_down))

    row = lambda t: t.reshape(t.shape[:-1] + (1, t.shape[-1]))
    lat = jnp.transpose(mod[:, :BATCH].reshape(DEPTH, BATCH, 6, D_MODEL), (0, 2, 1, 3))
    lat = jnp.concatenate([lat, lat], axis=2)
    cm = jnp.broadcast_to(mod[:, BATCH].reshape(DEPTH, 6, 1, D_MODEL), (DEPTH, 6, SUBLANES, D_MODEL))
    modtab = jnp.stack([cm, lat], axis=1)
    heads = lru_wa.shape[2]
    pairs = lambda w: _block_diag(w.reshape(DEPTH, 2, D_RNN // LANES, heads * LANES // D_RNN,
                                            D_RNN // heads, D_RNN // heads))
    wg = (-LOG2_E * jnp.concatenate([pairs(lru_wa), pairs(lru_wi)], axis=-1)).astype(BF16)
    bg = -LOG2_E * row(jnp.concatenate([lru_ba, lru_bi], axis=-1))
    spl = row(-LRU_C * LOG2_E * jax.nn.softplus(-lru_lambda.astype(F32)))
    scan_params = (lru_conv_w, row(lru_conv_b), wg, bg, spl) + _s5_params(
        s5_lambda_re, s5_lambda_im, s5_log_dt, s5_b_re, s5_b_im, s5_c_re, s5_c_im)
    weights = (row(s5_d), glu_b, row(s5_glu_b), _block_diag(pool_w).astype(BF16), row(pool_b),
               row(pool_scale), w_out_b, row(norm2_g), ffn_gate_b, ffn_up_b, ffn_down_b, fg)
    g1 = row(norm1_g)

    for l in range(DEPTH):
        last = l == DEPTH - 1
        xt, ax, u, gate, v, vc = _inproj(xt, modtab, g1, w_in_b, l)
        hf, ysf, hb, ysb = _scan(l, ax, u, *scan_params)
        m = _pool(v)
        mc = None if last else _pool_ctx(vc)
        xt = _tail(last, l, gate, hf, hb, u, ysf, ysb, m, mc, xt, modtab, weights)
    return xt
```

```python
import functools

import jax
import jax.numpy as jnp
from jax import lax
from jax.experimental import pallas as pl
from jax.experimental.pallas import tpu as pltpu

F32 = jnp.float32
BF16 = jnp.bfloat16

D_MODEL = 1024
BATCH = 4
SEQ = 8192
DEPTH = 2
CTX_LEN = 256
GRID_W = 64
GRID_H = SEQ // GRID_W
EPS = 1e-6

D_RNN = 384
CONV_WIDTH = 4
LRU_C = 8.0
LOG2_E = 1.4426950408889634

D_SSM = 384
SSM_GROUP = 16
SSM_GROUPS = D_SSM // SSM_GROUP
SSM_STATE = 64
N_STATE = SSM_GROUPS * SSM_STATE

D_POOL = 256
POOL_WINDOWS = (2, 4, 8, 16)
POOL_GROUP = D_POOL // len(POOL_WINDOWS)
POOL_PAD = 8

D_IN = 2 * D_RNN + D_SSM + D_POOL
D_MIX = D_RNN + D_SSM + D_POOL
D_FF = 2816
MXU_DIM = 256
FF_EDGES = (0, 3 * MXU_DIM, 6 * MXU_DIM, 9 * MXU_DIM, D_FF)

LANES = 128
SUBLANES = 8
HALF = SUBLANES // 2
assert BATCH == HALF

N_TOK = CTX_LEN + SEQ
N_ROWS = N_TOK * BATCH
CTX_ROWS = CTX_LEN * BATCH
SEQ_ROWS = SEQ * BATCH

ROW_TILE = 1024
N_ROW_TILES = N_ROWS // ROW_TILE
assert CTX_ROWS == ROW_TILE
TAIL_TILE = 1024
TAIL_CTX_TILES = CTX_ROWS // TAIL_TILE
TAIL_SLAB = 512

SCAN_TOK = 128
SCAN_ROWS = SCAN_TOK * BATCH
SCAN_TILES = N_TOK // SCAN_TOK
SCAN_CTX_TILES = CTX_LEN // SCAN_TOK
SCAN_BLKS = SCAN_ROWS // SUBLANES
S5_BLOCKS = D_SSM // LANES
S5_BLK_STATES = N_STATE // S5_BLOCKS
SCAN_SEGS = 4

POOL_GROWS = 16
POOL_TILE = POOL_GROWS * GRID_W * BATCH
POOL_HALO = POOL_PAD * GRID_W * BATCH
GROW = GRID_W * BATCH
PROW = (GRID_W + 2 * POOL_PAD) * BATCH

VMEM_LIMIT = 56 * 1024 * 1024
TAIL_VMEM_LIMIT = 62 * 1024 * 1024


def _const_spec(shape):
    nd = len(shape)
    return pl.BlockSpec(shape, lambda *_: (0,) * nd, pipeline_mode=pl.Buffered(1))


def _layer_spec(shape, layer):
    nd = len(shape) - 1
    return pl.BlockSpec((None,) + tuple(shape[1:]), lambda *_: (layer,) + (0,) * nd,
                        pipeline_mode=pl.Buffered(1))


def _rms(x):
    return x * lax.rsqrt(jnp.mean(x * x, axis=-1, keepdims=True) + EPS)


def _rows(pat, n):
    return jnp.broadcast_to(pat[None], (n // SUBLANES,) + pat.shape).reshape(n, pat.shape[-1])


def _mod_spec(layer, off=0, ctx_tiles=1):
    return pl.BlockSpec((None, None, 6, SUBLANES, D_MODEL),
                        lambda i: (layer, jnp.minimum((i + off) // ctx_tiles, 1), 0, 0, 0))


def _mod_kernel(c_ref, w_ref, b_ref, o_ref):
    s = jax.nn.silu(c_ref[...])
    o_ref[...] = jnp.dot(s.astype(BF16), w_ref[...].astype(BF16),
                         preferred_element_type=F32) + b_ref[...]


def _modulation(cc, w_mod, b_mod):
    n_col = 6 * D_MODEL
    blk = 1536
    return pl.pallas_call(
        _mod_kernel,
        grid=(DEPTH, n_col // blk),
        in_specs=[pl.BlockSpec((SUBLANES, D_MODEL), lambda l, j: (0, 0)),
                  pl.BlockSpec((None, D_MODEL, blk), lambda l, j: (l, 0, j)),
                  pl.BlockSpec((None, 1, blk), lambda l, j: (l, 0, j))],
        out_specs=pl.BlockSpec((None, SUBLANES, blk), lambda l, j: (l, 0, j)),
        out_shape=jax.ShapeDtypeStruct((DEPTH, SUBLANES, n_col), F32),
        compiler_params=pltpu.CompilerParams(vmem_limit_bytes=VMEM_LIMIT),
        name="modulation",
    )(cc, w_mod, b_mod.reshape(DEPTH, 1, n_col))


def _interleave(src_ref, slab):
    for b in range(BATCH):
        for s in range(D_MODEL // LANES):
            slab[s, pl.ds(b, ROW_TILE // BATCH, stride=BATCH), :] = src_ref[b, :, s * LANES:(s + 1) * LANES]


def _inproj_body(x, mod_ref, g_ref, w_ref, ax_ref, u_ref, gate_ref, v_ref, vc_ref):
    h = _rms(x) * g_ref[...]
    h = h * (1.0 + _rows(mod_ref[1], ROW_TILE)) + _rows(mod_ref[0], ROW_TILE)
    p = jnp.dot(h.astype(BF16), w_ref[...], preferred_element_type=F32)
    ax_ref[...] = p[:, 0:D_RNN]
    u_ref[...] = p[:, D_RNN:D_RNN + D_SSM].astype(BF16)
    gate_ref[...] = p[:, D_RNN + D_SSM:2 * D_RNN + D_SSM].astype(BF16)
    v = p[:, 2 * D_RNN + D_SSM:]
    v_ref[...] = v

    @pl.when(pl.program_id(0) == 0)
    def _():
        vc_ref[...] = v


def _inproj_kernel(x_ref, *rest):
    _inproj_body(x_ref[...], *rest)


def _inproj_first_kernel(ctx_ref, x_ref, mod_ref, g_ref, w_ref, xt_ref, *rest):
    slab = rest[-1]
    i = pl.program_id(0)

    @pl.when(i == 0)
    def _():
        _interleave(ctx_ref, slab)

    @pl.when(i > 0)
    def _():
        _interleave(x_ref, slab)

    x = jnp.concatenate([slab[s] for s in range(D_MODEL // LANES)], axis=1)
    xt_ref[...] = x
    _inproj_body(x, mod_ref, g_ref, w_ref, *rest[:-1])


def _inproj(xt, modtab, g, w_in, layer):
    first = isinstance(xt, tuple)
    tok = lambda w: pl.BlockSpec((ROW_TILE, w), lambda i: (i, 0))
    out = lambda w, dt=F32: jax.ShapeDtypeStruct((N_ROWS, w), dt)
    out_specs = [tok(D_RNN), tok(D_SSM), tok(D_RNN),
                 pl.BlockSpec((ROW_TILE, D_POOL), lambda i: (jnp.maximum(i - 1, 0), 0)),
                 pl.BlockSpec((CTX_ROWS, D_POOL), lambda i: (0, 0))]
    out_shape = [out(D_RNN), out(D_SSM, BF16), out(D_RNN, BF16),
                 jax.ShapeDtypeStruct((SEQ_ROWS, D_POOL), F32),
                 jax.ShapeDtypeStruct((CTX_ROWS, D_POOL), F32)]
    common = [_mod_spec(layer), _layer_spec(g.shape, layer), _layer_spec(w_in.shape, layer)]
    params = pltpu.CompilerParams(dimension_semantics=("arbitrary",), vmem_limit_bytes=VMEM_LIMIT)
    if not first:
        return (xt,) + tuple(pl.pallas_call(
            _inproj_kernel, grid=(N_ROW_TILES,), in_specs=[tok(D_MODEL)] + common,
            out_specs=out_specs, out_shape=out_shape, compiler_params=params, name="inproj",
        )(xt, modtab, g, w_in))
    toks = ROW_TILE // BATCH
    src = [pl.BlockSpec((BATCH, CTX_LEN, D_MODEL), lambda i: (0, 0, 0), pipeline_mode=pl.Buffered(1)),
           pl.BlockSpec((BATCH, toks, D_MODEL), lambda i: (0, jnp.maximum(i - 1, 0), 0))]
    return tuple(pl.pallas_call(
        _inproj_first_kernel, grid=(N_ROW_TILES,), in_specs=src + common,
        out_specs=[tok(D_MODEL)] + out_specs, out_shape=[out(D_MODEL)] + out_shape,
        scratch_shapes=[pltpu.VMEM((D_MODEL // LANES, ROW_TILE, LANES), F32)],
        compiler_params=params, name="inproj_first",
    )(xt[0], xt[1], modtab, g, w_in))


def _fwd_tile(i):
    return i


def _bwd_tile(i):
    return jnp.where(i < SCAN_CTX_TILES, SCAN_CTX_TILES - 1 - i, SCAN_TILES + SCAN_CTX_TILES - 1 - i)


def _scan_kernel(axp_f, ax_f, axn_f, u_f, axp_b, ax_b, axn_b, u_b,
                 cw_ref, cb_ref, wg_ref, bg_ref, spl_ref, bmat_ref, cmat_ref, lam_ref,
                 hf_ref, ysf_ref, hb_ref, ysb_ref,
                 axs_f, axs_b, a_f, b_f, a_b, b_b, uf32, ub32, us1, us2, xs1, xs2, ys1, ys2,
                 hfs, hbs, yfs, ybs, lcar, scar):
    i = pl.program_id(0)
    uf32[...] = u_f[...].astype(F32)
    ub32[...] = u_b[...].astype(F32)

    @pl.when(i == 0)
    def _():
        lcar[...] = jnp.zeros_like(lcar)
        scar[...] = jnp.zeros_like(scar)

    def stage_conv(tile, axp, ax, axn, axs):
        prev_ok = jnp.logical_and(tile != 0, tile != SCAN_CTX_TILES)
        next_ok = jnp.logical_and(tile != SCAN_CTX_TILES - 1, tile != SCAN_TILES - 1)
        axs[0:SUBLANES, :] = jnp.where(prev_ok, axp[...], 0.0)
        axs[SUBLANES:SUBLANES + SCAN_ROWS, :] = ax[...]
        axs[SUBLANES + SCAN_ROWS:, :] = jnp.where(next_ok, axn[...], 0.0)

    def gates(d, k, axs, a_out, b_out):
        lanes = slice(k * LANES, (k + 1) * LANES)
        xc = cb_ref[:, lanes] + axs[0:SCAN_ROWS, lanes] * cw_ref[0:1, lanes]
        for t in range(1, CONV_WIDTH):
            xc = xc + axs[t * BATCH:t * BATCH + SCAN_ROWS, lanes] * cw_ref[t:t + 1, lanes]
        g = jnp.dot(xc.astype(BF16), wg_ref[d, k], preferred_element_type=F32)
        r = 1.0 / (1.0 + jnp.exp2(g[:, :LANES] + bg_ref[d, :, lanes]))
        ig = 1.0 / (1.0 + jnp.exp2(g[:, LANES:] + bg_ref[d, :, D_RNN + k * LANES:D_RNN + (k + 1) * LANES]))
        a = jnp.exp2(r * spl_ref[d, :, lanes])
        a_out[:, lanes] = a
        b_out[:, lanes] = jnp.sqrt(1.0 - a * a) * (ig * xc)

    lo = lax.broadcasted_iota(jnp.int32, (SUBLANES, 1), 0) < HALF

    def swap(x):
        return pltpu.roll(x, HALF, 0)

    def rows_of(j):
        jb = SCAN_BLKS - 1 - j
        return slice(j * SUBLANES, (j + 1) * SUBLANES), slice(jb * SUBLANES, (jb + 1) * SUBLANES)

    def steps(vf, vb):
        return jnp.where(lo, vf, vb), swap(jnp.where(lo, vb, vf))

    def unsteps(h1, h2):
        h2s = swap(h2)
        return jnp.where(lo, h1, h2s), jnp.where(lo, h2s, h1)

    def lru_scan(h, j0, j1):
        for j in range(j0, j1):
            rf, rb = rows_of(j)
            a1, a2 = steps(a_f[rf, :], a_b[rb, :])
            b1, b2 = steps(b_f[rf, :], b_b[rb, :])
            h1 = a1 * h + b1
            h = a2 * h1 + b2
            hfs[rf, :], hbs[rb, :] = unsteps(h1, h)
        return h

    for j in range(SCAN_BLKS):
        rf, rb = rows_of(j)
        us1[rf, :], us2[rf, :] = steps(uf32[rf, :], ub32[rb, :])
    stage_conv(_fwd_tile(i), axp_f, ax_f, axn_f, axs_f)
    stage_conv(_bwd_tile(i), axp_b, ax_b, axn_b, axs_b)

    nst = S5_BLK_STATES

    def s5_scan(jb, carry, j0, j1):
        x1, x2 = xs1.at[jb], xs2.at[jb]
        re, im = slice(0, nst), slice(nst, 2 * nst)
        lr, li = lam_ref[0, :, jb * nst:(jb + 1) * nst], lam_ref[1, :, jb * nst:(jb + 1) * nst]
        sr, si = carry
        for j in range(j0, j1):
            rows = slice(j * SUBLANES, (j + 1) * SUBLANES)
            r1 = lr * sr - li * si + x1[rows, re]
            i1 = lr * si + li * sr + x1[rows, im]
            sr = lr * r1 - li * i1 + x2[rows, re]
            si = lr * i1 + li * r1 + x2[rows, im]
            x1[rows, re] = r1
            x1[rows, im] = i1
            x2[rows, re] = sr
            x2[rows, im] = si
        return sr, si

    def out_proj(jb, piece):
        xs, ys = ((xs1, ys1), (xs2, ys2))[piece // 2]
        half = slice(0, nst) if piece % 2 == 0 else slice(nst, 2 * nst)
        cols = slice(2 * jb * LANES, 2 * (jb + 1) * LANES)
        y = jnp.dot(xs[jb, :, half].astype(BF16), cmat_ref[jb, half, :], preferred_element_type=F32)
        ys[:, cols] = y if piece % 2 == 0 else ys[:, cols] + y

    for jb in range(S5_BLOCKS):
        blk = slice(jb * LANES, (jb + 1) * LANES)
        gates(0, jb, axs_f, a_f, b_f)
        xs1[jb] = jnp.dot(us1[:, blk].astype(BF16), bmat_ref[jb], preferred_element_type=F32)
        gates(1, jb, axs_b, a_b, b_b)
        xs2[jb] = jnp.dot(us2[:, blk].astype(BF16), bmat_ref[jb], preferred_element_type=F32)
    seg = SCAN_BLKS // SCAN_SEGS
    for jb in range(S5_BLOCKS):
        carry = (scar[:, jb * nst:(jb + 1) * nst], scar[:, N_STATE + jb * nst:N_STATE + (jb + 1) * nst])
        for q in range(SCAN_SEGS):
            carry = s5_scan(jb, carry, q * seg, (q + 1) * seg)
            if jb > 0:
                out_proj(jb - 1, q)
        scar[:, jb * nst:(jb + 1) * nst] = carry[0]
        scar[:, N_STATE + jb * nst:N_STATE + (jb + 1) * nst] = carry[1]
    h = lcar[...]
    for q in range(SCAN_SEGS):
        h = lru_scan(h, q * seg, (q + 1) * seg)
        out_proj(S5_BLOCKS - 1, q)
    lcar[...] = h
    hf_ref[...] = hfs[...].astype(BF16)
    hb_ref[...] = hbs[...].astype(BF16)

    for j in range(SCAN_BLKS):
        rf, rb = rows_of(j)
        of, ob = unsteps(ys1[rf, :], ys2[rf, :])
        for jb in range(S5_BLOCKS):
            yfs[rf, jb * LANES:(jb + 1) * LANES] = of[:, 2 * jb * LANES:(2 * jb + 1) * LANES]
            ybs[rb, jb * LANES:(jb + 1) * LANES] = ob[:, (2 * jb + 1) * LANES:2 * (jb + 1) * LANES]
    ysf_ref[...] = yfs[...].astype(BF16)
    ysb_ref[...] = ybs[...].astype(BF16)


def _scan(layer, ax, u, *params):
    tpb = SCAN_ROWS // SUBLANES
    nb8 = N_ROWS // SUBLANES

    def specs(tile_of):
        tok = lambda w: pl.BlockSpec((SCAN_ROWS, w), lambda i: (tile_of(i), 0))
        prev = pl.BlockSpec((SUBLANES, D_RNN), lambda i: (jnp.maximum(tile_of(i) * tpb - 1, 0), 0))
        nxt = pl.BlockSpec((SUBLANES, D_RNN), lambda i: (jnp.minimum((tile_of(i) + 1) * tpb, nb8 - 1), 0))
        return tok, prev, nxt

    tok_f, prev_f, next_f = specs(_fwd_tile)
    tok_b, prev_b, next_b = specs(_bwd_tile)
    out = jax.ShapeDtypeStruct((N_ROWS, D_RNN), BF16)
    tile_f32 = lambda w: pltpu.VMEM((SCAN_ROWS, w), F32)
    return pl.pallas_call(
        _scan_kernel,
        grid=(SCAN_TILES,),
        in_specs=[prev_f, tok_f(D_RNN), next_f, tok_f(D_SSM),
                  prev_b, tok_b(D_RNN), next_b, tok_b(D_SSM)] + [_layer_spec(p.shape, layer) for p in params],
        out_specs=[tok_f(D_RNN), tok_f(D_SSM), tok_b(D_RNN), tok_b(D_SSM)],
        out_shape=[out, out, out, out],
        scratch_shapes=[pltpu.VMEM((SCAN_ROWS + 2 * SUBLANES, D_RNN), F32),
                        pltpu.VMEM((SCAN_ROWS + 2 * SUBLANES, D_RNN), F32),
                        tile_f32(D_RNN), tile_f32(D_RNN), tile_f32(D_RNN), tile_f32(D_RNN),
                        tile_f32(D_SSM), tile_f32(D_SSM), tile_f32(D_SSM), tile_f32(D_SSM),
                        pltpu.VMEM((S5_BLOCKS, SCAN_ROWS, 2 * S5_BLK_STATES), F32),
                        pltpu.VMEM((S5_BLOCKS, SCAN_ROWS, 2 * S5_BLK_STATES), F32),
                        tile_f32(2 * D_SSM), tile_f32(2 * D_SSM),
                        tile_f32(D_RNN), tile_f32(D_RNN), tile_f32(D_SSM), tile_f32(D_SSM),
                        pltpu.VMEM((SUBLANES, D_RNN), F32),
                        pltpu.VMEM((SUBLANES, 2 * N_STATE), F32)],
        compiler_params=pltpu.CompilerParams(
            dimension_semantics=("arbitrary",), vmem_limit_bytes=VMEM_LIMIT),
        name="scan",
    )(ax, ax, ax, u, ax, ax, ax, u, *params)


def _window_sums(load, ha, hb):
    sa = load(-ha)
    for o in range(-ha + 1, ha):
        sa = sa + load(o)
    sb = sa
    for o in list(range(-hb, -ha)) + list(range(ha, hb)):
        sb = sb + load(o)
    return sa, sb


def _count(pos, half, n):
    return (jnp.minimum(pos + half, n) - jnp.maximum(pos - half, 0)).astype(F32)


def _pool_halves():
    is_a = lax.broadcasted_iota(jnp.int32, (1, LANES), 1) < POOL_GROUP
    for half in range(D_POOL // LANES):
        wa, wb = POOL_WINDOWS[2 * half], POOL_WINDOWS[2 * half + 1]
        yield slice(half * LANES, (half + 1) * LANES), wa // 2, wb // 2, is_a


def _pool_ctx_kernel(v_ref, m_ref, cp):
    pad = POOL_PAD * BATCH
    tok = lax.broadcasted_iota(jnp.int32, (CTX_ROWS, 1), 0) // BATCH
    for lanes, ha, hb, is_a in _pool_halves():
        cp[...] = jnp.zeros_like(cp)
        cp[pad:pad + CTX_ROWS, :] = v_ref[:, lanes]
        ca, cb = _window_sums(lambda o: cp[pad + o * BATCH:pad + o * BATCH + CTX_ROWS, :], ha, hb)
        pooled = jnp.where(is_a, ca / _count(tok, ha, CTX_LEN), cb / _count(tok, hb, CTX_LEN))
        m_ref[:, lanes] = (pooled - v_ref[:, lanes]).astype(BF16)


def _pool_ctx(vc):
    return pl.pallas_call(
        _pool_ctx_kernel,
        out_shape=jax.ShapeDtypeStruct((CTX_ROWS, D_POOL), BF16),
        scratch_shapes=[pltpu.VMEM((CTX_ROWS + 2 * POOL_PAD * BATCH, LANES), F32)],
        name="pool_ctx",
    )(vc)


def _pool_kernel(vprev_ref, v_ref, vnext_ref, m_ref, vp, sp):
    i = pl.program_id(0)
    g0 = i * POOL_GROWS
    pad = POOL_PAD * BATCH
    col = lax.broadcasted_iota(jnp.int32, (GROW, 1), 0) // BATCH
    have_prev = i > 0
    have_next = i < pl.num_programs(0) - 1
    for lanes, ha, hb, is_a in _pool_halves():
        vp[...] = jnp.zeros_like(vp)

        def fill(src_ref, first, keep, lanes=lanes):
            def body(q, _):
                dst = pl.multiple_of((first + q) * PROW + pad, SUBLANES)
                src = pl.multiple_of(q * GROW, SUBLANES)
                vp[pl.ds(dst, GROW), :] = jnp.where(keep, src_ref[pl.ds(src, GROW), lanes], 0.0)
                return 0
            return body

        lax.fori_loop(0, POOL_PAD, fill(vprev_ref, 0, have_prev), 0)
        lax.fori_loop(0, POOL_GROWS, fill(v_ref, POOL_PAD, True), 0)
        lax.fori_loop(0, POOL_PAD, fill(vnext_ref, POOL_PAD + POOL_GROWS, have_next), 0)
        ccnt_a, ccnt_b = _count(col, ha, GRID_W), _count(col, hb, GRID_W)

        def body(r, _, lanes=lanes, ha=ha, hb=hb, is_a=is_a, ccnt_a=ccnt_a, ccnt_b=ccnt_b):
            base = (r + POOL_PAD) * PROW
            sa, sb = _window_sums(
                lambda o: vp[pl.ds(pl.multiple_of(base + o * PROW, SUBLANES), PROW), :], ha, hb)
            sp[...] = jnp.where(is_a, sa, sb)
            ba, bb = _window_sums(lambda o: sp[pad + o * BATCH:pad + o * BATCH + GROW, :], ha, hb)
            rcnt_a, rcnt_b = _count(g0 + r, ha, GRID_H), _count(g0 + r, hb, GRID_H)
            pooled = jnp.where(is_a, ba / (rcnt_a * ccnt_a), bb / (rcnt_b * ccnt_b))
            src = pl.multiple_of(r * GROW, GROW)
            m_ref[pl.ds(src, GROW), lanes] = (pooled - v_ref[pl.ds(src, GROW), lanes]).astype(BF16)
            return 0

        lax.fori_loop(0, POOL_GROWS, body, 0)


def _pool(v):
    n = SEQ_ROWS // POOL_TILE
    per = POOL_TILE // POOL_HALO
    nh = SEQ_ROWS // POOL_HALO
    return pl.pallas_call(
        _pool_kernel,
        grid=(n,),
        in_specs=[pl.BlockSpec((POOL_HALO, D_POOL), lambda i: (jnp.maximum(i * per - 1, 0), 0)),
                  pl.BlockSpec((POOL_TILE, D_POOL), lambda i: (i, 0)),
                  pl.BlockSpec((POOL_HALO, D_POOL), lambda i: (jnp.minimum((i + 1) * per, nh - 1), 0))],
        out_specs=pl.BlockSpec((POOL_TILE, D_POOL), lambda i: (i, 0)),
        out_shape=jax.ShapeDtypeStruct((SEQ_ROWS, D_POOL), BF16),
        scratch_shapes=[pltpu.VMEM(((POOL_GROWS + 2 * POOL_PAD) * PROW, LANES), F32),
                        pltpu.VMEM((PROW, LANES), F32)],
        compiler_params=pltpu.CompilerParams(
            dimension_semantics=("parallel",), vmem_limit_bytes=VMEM_LIMIT),
        name="pool",
    )(v, v, v)


def _tail_kernel(final, *refs):
    if final:
        gate_ref, hf_ref, hb_ref, u_ref, ysf_ref, ysb_ref, m_ref, x_ref, mod_ref = refs[:9]
        rest = refs[9:]
    else:
        gate_ref, hf_ref, hb_ref, u_ref, ysf_ref, ysb_ref, m_ref, mc_ref, x_ref, mod_ref = refs[:10]
        rest = refs[10:]
    (dvec_ref, wglu_ref, bglu_ref, pw_ref, pb_ref, ps_ref, wo_ref,
     g2_ref, wg_ref, wu_ref, wd_ref, fg_ref, o_ref) = rest[:13]
    dot = functools.partial(jnp.dot, preferred_element_type=F32)
    f32 = lambda ref: ref[...].astype(F32)

    y_a = jax.nn.gelu(f32(gate_ref)) * (f32(hf_ref) + f32(hb_ref))
    y_s = dvec_ref[...] * f32(u_ref) + f32(ysf_ref) + f32(ysb_ref)
    z = jax.nn.gelu(y_s)
    y_b = z * jax.nn.sigmoid(dot(z.astype(BF16), wglu_ref[...]) + bglu_ref[...])
    m = m_ref[...] if final else jnp.where(pl.program_id(0) < TAIL_CTX_TILES, mc_ref[...], m_ref[...])
    y_c = (dot(m.astype(BF16), pw_ref[...]) + pb_ref[...]) * ps_ref[...]
    y = jnp.concatenate([y_a.astype(BF16), y_b.astype(BF16), y_c.astype(BF16)], axis=1)
    x = x_ref[...] + _rows(mod_ref[2], TAIL_TILE) * dot(y, wo_ref[...])

    h = _rms(x) * g2_ref[...]
    h = (h * (1.0 + _rows(mod_ref[4], TAIL_TILE)) + _rows(mod_ref[3], TAIL_TILE)).astype(BF16)
    ff = None
    for c0, c1 in zip(FF_EDGES[:-1], FF_EDGES[1:]):
        cols = slice(c0, c1)
        act = (jax.nn.silu(dot(h, wg_ref[:, cols])) * dot(h, wu_ref[:, cols])).astype(BF16)
        part = dot(act, wd_ref[cols, :])
        ff = part if ff is None else ff + part
    y = x + _rows(mod_ref[5], TAIL_TILE) * ff
    if not final:
        o_ref[...] = y
        return
    y = _rms(y) * fg_ref[...]
    slab = rest[13]
    rows, toks = slab.shape[1], slab.shape[1] // BATCH
    for part in range(TAIL_TILE // rows):
        for s in range(D_MODEL // LANES):
            slab[s] = y[part * rows:(part + 1) * rows, s * LANES:(s + 1) * LANES]
        for b in range(BATCH):
            for s in range(D_MODEL // LANES):
                o_ref[b, part * toks:(part + 1) * toks, s * LANES:(s + 1) * LANES] = (
                    slab[s, pl.ds(b, toks, stride=BATCH), :])


def _tail(final, layer, gate, hf, hb, u, ysf, ysb, m, mc, xt, modtab, weights):
    off = TAIL_CTX_TILES if final else 0
    n_tiles = N_ROWS // TAIL_TILE - off
    tok = lambda w: pl.BlockSpec((TAIL_TILE, w), lambda i: (i + off, 0))
    acts = [gate, hf, hb, u, ysf, ysb, m]
    act_specs = [tok(D_RNN), tok(D_RNN), tok(D_RNN), tok(D_SSM), tok(D_SSM), tok(D_SSM),
                 pl.BlockSpec((TAIL_TILE, D_POOL), lambda i: (jnp.maximum(i + off - TAIL_CTX_TILES, 0), 0))]
    if final:
        toks = TAIL_TILE // BATCH
        out_spec = pl.BlockSpec((BATCH, toks, D_MODEL), lambda i: (0, i, 0))
        out_shape = jax.ShapeDtypeStruct((BATCH, n_tiles * toks, D_MODEL), F32)
        scratch = [pltpu.VMEM((D_MODEL // LANES, TAIL_SLAB, LANES), F32)]
    else:
        acts.append(mc)
        act_specs.append(pl.BlockSpec((TAIL_TILE, D_POOL), lambda i: (jnp.minimum(i, TAIL_CTX_TILES - 1), 0)))
        out_spec = pl.BlockSpec((TAIL_TILE, D_MODEL), lambda i: (i, 0))
        out_shape = jax.ShapeDtypeStruct((N_ROWS, D_MODEL), F32)
        scratch = []
    return pl.pallas_call(
        functools.partial(_tail_kernel, final),
        grid=(n_tiles,),
        in_specs=(act_specs + [tok(D_MODEL), _mod_spec(layer, off, TAIL_CTX_TILES)]
                  + [_const_spec(w.shape) if w.ndim == 2 else _layer_spec(w.shape, layer) for w in weights]),
        out_specs=out_spec,
        out_shape=out_shape,
        scratch_shapes=scratch,
        compiler_params=pltpu.CompilerParams(
            dimension_semantics=("parallel",), vmem_limit_bytes=TAIL_VMEM_LIMIT),
        name="tail_final" if final else "tail",
    )(*acts, xt, modtab, *weights)


def _block_diag(w):
    n, di, dj = w.shape[-3:]
    out = jnp.einsum('...nij,nm->...nimj', w, jnp.eye(n, dtype=w.dtype))
    return out.reshape(w.shape[:-3] + (n * di, n * dj))


def _s5_params(lam_re, lam_im, log_dt, b_re, b_im, c_re, c_im):
    gpb = SSM_GROUPS // S5_BLOCKS
    depth = lam_re.shape[0]
    f32 = lambda t: t.astype(F32)

    in_blocks = lambda w: _block_diag(
        jnp.swapaxes(f32(w), -1, -2).reshape(depth, S5_BLOCKS, gpb, SSM_GROUP, SSM_STATE))
    out_blocks = lambda w: _block_diag(
        jnp.swapaxes(w, -1, -2).reshape(depth, 2, S5_BLOCKS, gpb, SSM_STATE, SSM_GROUP))

    bmat = jnp.concatenate([in_blocks(b_re), in_blocks(b_im)], axis=-1).astype(BF16)
    lr, li = f32(lam_re), f32(lam_im)
    dt = jnp.exp(f32(log_dt))[..., None]
    mag = jnp.exp(lr * dt)
    ang = li * dt
    bar_r, bar_i = mag * jnp.cos(ang), mag * jnp.sin(ang)
    den = lr * lr + li * li
    fr = ((bar_r - 1.0) * lr + bar_i * li) / den
    fi = (bar_i * lr - (bar_r - 1.0) * li) / den
    cr, ci = f32(c_re), f32(c_im)
    cfr = cr * fr[..., None, :] - ci * fi[..., None, :]
    cfi = cr * fi[..., None, :] + ci * fr[..., None, :]
    cm = jnp.concatenate([out_blocks(cfr), out_blocks(-cfi)], axis=-2)
    cmat = jnp.concatenate([cm[:, 0], cm[:, 1]], axis=-1).astype(BF16)
    lam = jnp.stack([bar_r.reshape(depth, 2, N_STATE), bar_i.reshape(depth, 2, N_STATE)], axis=1)
    lam = jnp.repeat(lam, HALF, axis=2)
    return bmat, cmat, lam


def kernel(x, c, ctx, c_ctx, w_mod, b_mod, norm1_g, norm2_g, w_in, w_out, lru_conv_w, lru_conv_b,
           lru_wa, lru_ba, lru_wi, lru_bi, lru_lambda, s5_lambda_re, s5_lambda_im, s5_log_dt,
           s5_b_re, s5_b_im, s5_c_re, s5_c_im, s5_d, s5_glu_w, s5_glu_b, pool_w, pool_b, pool_scale,
           ffn_w_gate, ffn_w_up, ffn_w_down, final_g):
    cc = jnp.zeros((SUBLANES, D_MODEL), F32).at[:BATCH].set(c).at[BATCH].set(c_ctx)
    mod = _modulation(cc, w_mod, b_mod)
    xt = (ctx, x)
    fg = final_g.reshape(1, D_MODEL)
    w_in_b, w_out_b, glu_b = w_in.astype(BF16), w_out.astype(BF16), s5_glu_w.astype(BF16)
    ffn_gate_b, ffn_up_b, ffn_down_b = (w.astype(BF16) for w in (ffn_w_gate, ffn_w_up, ffn_w_down))

    row = lambda t: t.reshape(t.shape[:-1] + (1, t.shape[-1]))
    lat = jnp.transpose(mod[:, :BATCH].reshape(DEPTH, BATCH, 6, D_MODEL), (0, 2, 1, 3))
    lat = jnp.concatenate([lat, lat], axis=2)
    cm = jnp.broadcast_to(mod[:, BATCH].reshape(DEPTH, 6, 1, D_MODEL), (DEPTH, 6, SUBLANES, D_MODEL))
    modtab = jnp.stack([cm, lat], axis=1)
    heads = lru_wa.shape[2]
    pairs = lambda w: _block_diag(w.reshape(DEPTH, 2, D_RNN // LANES, heads * LANES // D_RNN,
                                            D_RNN // heads, D_RNN // heads))
    wg = (-LOG2_E * jnp.concatenate([pairs(lru_wa), pairs(lru_wi)], axis=-1)).astype(BF16)
    bg = -LOG2_E * row(jnp.concatenate([lru_ba, lru_bi], axis=-1))
    spl = row(-LRU_C * LOG2_E * jax.nn.softplus(-lru_lambda.astype(F32)))
    scan_params = (lru_conv_w, row(lru_conv_b), wg, bg, spl) + _s5_params(
        s5_lambda_re, s5_lambda_im, s5_log_dt, s5_b_re, s5_b_im, s5_c_re, s5_c_im)
    weights = (row(s5_d), glu_b, row(s5_glu_b), _block_diag(pool_w).astype(BF16), row(pool_b),
               row(pool_scale), w_out_b, row(norm2_g), ffn_gate_b, ffn_up_b, ffn_down_b, fg)
    g1 = row(norm1_g)

    for l in range(DEPTH):
        last = l == DEPTH - 1
        xt, ax, u, gate, v, vc = _inproj(xt, modtab, g1, w_in_b, l)
        hf, ysf, hb, ysb = _scan(l, ax, u, *scan_params)
        m = _pool(v)
        mc = None if last else _pool_ctx(vc)
        xt = _tail(last, l, gate, hf, hb, u, ysf, ysb, m, mc, xt, modtab, weights)
    return xt
```

```python
import functools

import jax
import jax.numpy as jnp
from jax import lax
from jax.experimental import pallas as pl
from jax.experimental.pallas import tpu as pltpu

F32 = jnp.float32
BF16 = jnp.bfloat16

D_MODEL = 1024
BATCH = 4
SEQ = 8192
DEPTH = 2
CTX_LEN = 256
GRID_W = 64
GRID_H = SEQ // GRID_W
EPS = 1e-6

D_RNN = 384
CONV_WIDTH = 4
LRU_C = 8.0
LOG2_E = 1.4426950408889634

D_SSM = 384
SSM_GROUP = 16
SSM_GROUPS = D_SSM // SSM_GROUP
SSM_STATE = 64
N_STATE = SSM_GROUPS * SSM_STATE

D_POOL = 256
POOL_WINDOWS = (2, 4, 8, 16)
POOL_GROUP = D_POOL // len(POOL_WINDOWS)
POOL_PAD = 8

D_IN = 2 * D_RNN + D_SSM + D_POOL
D_MIX = D_RNN + D_SSM + D_POOL
D_FF = 2816
MXU_DIM = 256
FF_EDGES = (0, 3 * MXU_DIM, 6 * MXU_DIM, 9 * MXU_DIM, D_FF)

LANES = 128
SUBLANES = 8
HALF = SUBLANES // 2
assert BATCH == HALF

N_TOK = CTX_LEN + SEQ
N_ROWS = N_TOK * BATCH
CTX_ROWS = CTX_LEN * BATCH
SEQ_ROWS = SEQ * BATCH

ROW_TILE = 1024
N_ROW_TILES = N_ROWS // ROW_TILE
assert CTX_ROWS == ROW_TILE
TAIL_TILE = 1024
TAIL_CTX_TILES = CTX_ROWS // TAIL_TILE
TAIL_SLAB = 512

SCAN_TOK = 128
SCAN_ROWS = SCAN_TOK * BATCH
SCAN_TILES = N_TOK // SCAN_TOK
SCAN_CTX_TILES = CTX_LEN // SCAN_TOK
SCAN_BLKS = SCAN_ROWS // SUBLANES
S5_BLOCKS = D_SSM // LANES
S5_BLK_STATES = N_STATE // S5_BLOCKS
SCAN_SEGS = 4

POOL_GROWS = 16
POOL_TILE = POOL_GROWS * GRID_W * BATCH
POOL_HALO = POOL_PAD * GRID_W * BATCH
GROW = GRID_W * BATCH
PROW = (GRID_W + 2 * POOL_PAD) * BATCH

VMEM_LIMIT = 56 * 1024 * 1024
TAIL_VMEM_LIMIT = 62 * 1024 * 1024


def _const_spec(shape):
    nd = len(shape)
    return pl.BlockSpec(shape, lambda *_: (0,) * nd, pipeline_mode=pl.Buffered(1))


def _layer_spec(shape, layer):
    nd = len(shape) - 1
    return pl.BlockSpec((None,) + tuple(shape[1:]), lambda *_: (layer,) + (0,) * nd,
                        pipeline_mode=pl.Buffered(1))


def _rms(x):
    return x * lax.rsqrt(jnp.mean(x * x, axis=-1, keepdims=True) + EPS)


def _rows(pat, n):
    return jnp.broadcast_to(pat[None], (n // SUBLANES,) + pat.shape).reshape(n, pat.shape[-1])


def _mod_spec(layer, off=0, ctx_tiles=1):
    return pl.BlockSpec((None, None, 6, SUBLANES, D_MODEL),
                        lambda i: (layer, jnp.minimum((i + off) // ctx_tiles, 1), 0, 0, 0))


def _mod_kernel(c_ref, w_ref, b_ref, o_ref):
    s = jax.nn.silu(c_ref[...])
    o_ref[...] = jnp.dot(s.astype(BF16), w_ref[...].astype(BF16),
                         preferred_element_type=F32) + b_ref[...]


def _modulation(cc, w_mod, b_mod):
    n_col = 6 * D_MODEL
    blk = 1536
    return pl.pallas_call(
        _mod_kernel,
        grid=(DEPTH, n_col // blk),
        in_specs=[pl.BlockSpec((SUBLANES, D_MODEL), lambda l, j: (0, 0)),
                  pl.BlockSpec((None, D_MODEL, blk), lambda l, j: (l, 0, j)),
                  pl.BlockSpec((None, 1, blk), lambda l, j: (l, 0, j))],
        out_specs=pl.BlockSpec((None, SUBLANES, blk), lambda l, j: (l, 0, j)),
        out_shape=jax.ShapeDtypeStruct((DEPTH, SUBLANES, n_col), F32),
        compiler_params=pltpu.CompilerParams(vmem_limit_bytes=VMEM_LIMIT),
        name="modulation",
    )(cc, w_mod, b_mod.reshape(DEPTH, 1, n_col))


def _inproj_body(x, mod_ref, g_ref, w_ref, ax_ref, u_ref, gate_ref, v_ref, vc_ref):
    h = _rms(x) * g_ref[...]
    h = h * (1.0 + _rows(mod_ref[1], ROW_TILE)) + _rows(mod_ref[0], ROW_TILE)
    p = jnp.dot(h.astype(BF16), w_ref[...], preferred_element_type=F32)
    ax_ref[...] = p[:, 0:D_RNN]
    u_ref[...] = p[:, D_RNN:D_RNN + D_SSM].astype(BF16)
    gate_ref[...] = p[:, D_RNN + D_SSM:2 * D_RNN + D_SSM].astype(BF16)
    v = p[:, 2 * D_RNN + D_SSM:]
    v_ref[...] = v

    @pl.when(pl.program_id(0) == 0)
    def _():
        vc_ref[...] = v


def _inproj_kernel(x_ref, *rest):
    _inproj_body(x_ref[...], *rest)


def _inproj_first_kernel(ctx_ref, x_ref, mod_ref, g_ref, w_ref, xt_ref, *rest):
    slab = rest[-1]
    is_ctx = pl.program_id(0) == 0
    for b in range(BATCH):
        for s in range(D_MODEL // LANES):
            lanes = slice(s * LANES, (s + 1) * LANES)
            src = jnp.where(is_ctx, ctx_ref[b, :, lanes], x_ref[b, :, lanes])
            slab[s, pl.ds(b, ROW_TILE // BATCH, stride=BATCH), :] = src
    x = jnp.concatenate([slab[s] for s in range(D_MODEL // LANES)], axis=1)
    xt_ref[...] = x
    _inproj_body(x, mod_ref, g_ref, w_ref, *rest[:-1])


def _inproj(xt, modtab, g, w_in, layer):
    first = isinstance(xt, tuple)
    tok = lambda w: pl.BlockSpec((ROW_TILE, w), lambda i: (i, 0))
    out = lambda w, dt=F32: jax.ShapeDtypeStruct((N_ROWS, w), dt)
    out_specs = [tok(D_RNN), tok(D_SSM), tok(D_RNN),
                 pl.BlockSpec((ROW_TILE, D_POOL), lambda i: (jnp.maximum(i - 1, 0), 0)),
                 pl.BlockSpec((CTX_ROWS, D_POOL), lambda i: (0, 0))]
    out_shape = [out(D_RNN), out(D_SSM, BF16), out(D_RNN, BF16),
                 jax.ShapeDtypeStruct((SEQ_ROWS, D_POOL), F32),
                 jax.ShapeDtypeStruct((CTX_ROWS, D_POOL), F32)]
    common = [_mod_spec(layer), _layer_spec(g.shape, layer), _layer_spec(w_in.shape, layer)]
    params = pltpu.CompilerParams(dimension_semantics=("arbitrary",), vmem_limit_bytes=VMEM_LIMIT)
    if not first:
        return (xt,) + tuple(pl.pallas_call(
            _inproj_kernel, grid=(N_ROW_TILES,), in_specs=[tok(D_MODEL)] + common,
            out_specs=out_specs, out_shape=out_shape, compiler_params=params, name="inproj",
        )(xt, modtab, g, w_in))
    toks = ROW_TILE // BATCH
    src = [pl.BlockSpec((BATCH, CTX_LEN, D_MODEL), lambda i: (0, 0, 0), pipeline_mode=pl.Buffered(1)),
           pl.BlockSpec((BATCH, toks, D_MODEL), lambda i: (0, jnp.maximum(i - 1, 0), 0))]
    return tuple(pl.pallas_call(
        _inproj_first_kernel, grid=(N_ROW_TILES,), in_specs=src + common,
        out_specs=[tok(D_MODEL)] + out_specs, out_shape=[out(D_MODEL)] + out_shape,
        scratch_shapes=[pltpu.VMEM((D_MODEL // LANES, ROW_TILE, LANES), F32)],
        compiler_params=params, name="inproj_first",
    )(xt[0], xt[1], modtab, g, w_in))


def _fwd_tile(i):
    return i


def _bwd_tile(i):
    return jnp.where(i < SCAN_CTX_TILES, SCAN_CTX_TILES - 1 - i, SCAN_TILES + SCAN_CTX_TILES - 1 - i)


def _scan_kernel(axp_f, ax_f, axn_f, u_f, axp_b, ax_b, axn_b, u_b,
                 cw_ref, cb_ref, wg_ref, bg_ref, spl_ref, bmat_ref, cmat_ref, lam_ref,
                 hf_ref, ysf_ref, hb_ref, ysb_ref,
                 axs_f, axs_b, a_f, b_f, a_b, b_b, uf32, ub32, us1, us2, xs1, xs2, ys1, ys2,
                 hfs, hbs, yfs, ybs, lcar, scar):
    i = pl.program_id(0)
    uf32[...] = u_f[...].astype(F32)
    ub32[...] = u_b[...].astype(F32)

    @pl.when(i == 0)
    def _():
        lcar[...] = jnp.zeros_like(lcar)
        scar[...] = jnp.zeros_like(scar)

    def stage_conv(tile, axp, ax, axn, axs):
        prev_ok = jnp.logical_and(tile != 0, tile != SCAN_CTX_TILES)
        next_ok = jnp.logical_and(tile != SCAN_CTX_TILES - 1, tile != SCAN_TILES - 1)
        axs[0:SUBLANES, :] = jnp.where(prev_ok, axp[...], 0.0)
        axs[SUBLANES:SUBLANES + SCAN_ROWS, :] = ax[...]
        axs[SUBLANES + SCAN_ROWS:, :] = jnp.where(next_ok, axn[...], 0.0)

    def gates(d, k, axs, a_out, b_out):
        lanes = slice(k * LANES, (k + 1) * LANES)
        xc = cb_ref[:, lanes] + axs[0:SCAN_ROWS, lanes] * cw_ref[0:1, lanes]
        for t in range(1, CONV_WIDTH):
            xc = xc + axs[t * BATCH:t * BATCH + SCAN_ROWS, lanes] * cw_ref[t:t + 1, lanes]
        g = jnp.dot(xc.astype(BF16), wg_ref[d, k], preferred_element_type=F32)
        r = 1.0 / (1.0 + jnp.exp2(g[:, :LANES] + bg_ref[d, :, lanes]))
        ig = 1.0 / (1.0 + jnp.exp2(g[:, LANES:] + bg_ref[d, :, D_RNN + k * LANES:D_RNN + (k + 1) * LANES]))
        a = jnp.exp2(r * spl_ref[d, :, lanes])
        a_out[:, lanes] = a
        b_out[:, lanes] = jnp.sqrt(1.0 - a * a) * (ig * xc)

    lo = lax.broadcasted_iota(jnp.int32, (SUBLANES, 1), 0) < HALF

    def swap(x):
        return pltpu.roll(x, HALF, 0)

    def rows_of(j):
        jb = SCAN_BLKS - 1 - j
        return slice(j * SUBLANES, (j + 1) * SUBLANES), slice(jb * SUBLANES, (jb + 1) * SUBLANES)

    def steps(vf, vb):
        return jnp.where(lo, vf, vb), swap(jnp.where(lo, vb, vf))

    def unsteps(h1, h2):
        h2s = swap(h2)
        return jnp.where(lo, h1, h2s), jnp.where(lo, h2s, h1)

    def lru_scan(h, j0, j1):
        for j in range(j0, j1):
            rf, rb = rows_of(j)
            a1, a2 = steps(a_f[rf, :], a_b[rb, :])
            b1, b2 = steps(b_f[rf, :], b_b[rb, :])
            h1 = a1 * h + b1
            h = a2 * h1 + b2
            hfs[rf, :], hbs[rb, :] = unsteps(h1, h)
        return h

    for j in range(SCAN_BLKS):
        rf, rb = rows_of(j)
        us1[rf, :], us2[rf, :] = steps(uf32[rf, :], ub32[rb, :])
    stage_conv(_fwd_tile(i), axp_f, ax_f, axn_f, axs_f)
    stage_conv(_bwd_tile(i), axp_b, ax_b, axn_b, axs_b)

    nst = S5_BLK_STATES

    def s5_scan(jb, carry, j0, j1):
        x1, x2 = xs1.at[jb], xs2.at[jb]
        re, im = slice(0, nst), slice(nst, 2 * nst)
        lr, li = lam_ref[0, :, jb * nst:(jb + 1) * nst], lam_ref[1, :, jb * nst:(jb + 1) * nst]
        sr, si = carry
        for j in range(j0, j1):
            rows = slice(j * SUBLANES, (j + 1) * SUBLANES)
            r1 = lr * sr - li * si + x1[rows, re]
            i1 = lr * si + li * sr + x1[rows, im]
            sr = lr * r1 - li * i1 + x2[rows, re]
            si = lr * i1 + li * r1 + x2[rows, im]
            x1[rows, re] = r1
            x1[rows, im] = i1
            x2[rows, re] = sr
            x2[rows, im] = si
        return sr, si

    def out_proj(jb, piece):
        xs, ys = ((xs1, ys1), (xs2, ys2))[piece // 2]
        half = slice(0, nst) if piece % 2 == 0 else slice(nst, 2 * nst)
        cols = slice(2 * jb * LANES, 2 * (jb + 1) * LANES)
        y = jnp.dot(xs[jb, :, half].astype(BF16), cmat_ref[jb, half, :], preferred_element_type=F32)
        ys[:, cols] = y if piece % 2 == 0 else ys[:, cols] + y

    for jb in range(S5_BLOCKS):
        blk = slice(jb * LANES, (jb + 1) * LANES)
        gates(0, jb, axs_f, a_f, b_f)
        xs1[jb] = jnp.dot(us1[:, blk].astype(BF16), bmat_ref[jb], preferred_element_type=F32)
        gates(1, jb, axs_b, a_b, b_b)
        xs2[jb] = jnp.dot(us2[:, blk].astype(BF16), bmat_ref[jb], preferred_element_type=F32)
    seg = SCAN_BLKS // SCAN_SEGS
    for jb in range(S5_BLOCKS):
        carry = (scar[:, jb * nst:(jb + 1) * nst], scar[:, N_STATE + jb * nst:N_STATE + (jb + 1) * nst])
        for q in range(SCAN_SEGS):
            carry = s5_scan(jb, carry, q * seg, (q + 1) * seg)
            if jb > 0:
                out_proj(jb - 1, q)
        scar[:, jb * nst:(jb + 1) * nst] = carry[0]
        scar[:, N_STATE + jb * nst:N_STATE + (jb + 1) * nst] = carry[1]
    h = lcar[...]
    for q in range(SCAN_SEGS):
        h = lru_scan(h, q * seg, (q + 1) * seg)
        out_proj(S5_BLOCKS - 1, q)
    lcar[...] = h
    hf_ref[...] = hfs[...].astype(BF16)
    hb_ref[...] = hbs[...].astype(BF16)

    for j in range(SCAN_BLKS):
        rf, rb = rows_of(j)
        of, ob = unsteps(ys1[rf, :], ys2[rf, :])
        for jb in range(S5_BLOCKS):
            yfs[rf, jb * LANES:(jb + 1) * LANES] = of[:, 2 * jb * LANES:(2 * jb + 1) * LANES]
            ybs[rb, jb * LANES:(jb + 1) * LANES] = ob[:, (2 * jb + 1) * LANES:2 * (jb + 1) * LANES]
    ysf_ref[...] = yfs[...].astype(BF16)
    ysb_ref[...] = ybs[...].astype(BF16)


def _scan(layer, ax, u, *params):
    tpb = SCAN_ROWS // SUBLANES
    nb8 = N_ROWS // SUBLANES

    def specs(tile_of):
        tok = lambda w: pl.BlockSpec((SCAN_ROWS, w), lambda i: (tile_of(i), 0))
        prev = pl.BlockSpec((SUBLANES, D_RNN), lambda i: (jnp.maximum(tile_of(i) * tpb - 1, 0), 0))
        nxt = pl.BlockSpec((SUBLANES, D_RNN), lambda i: (jnp.minimum((tile_of(i) + 1) * tpb, nb8 - 1), 0))
        return tok, prev, nxt

    tok_f, prev_f, next_f = specs(_fwd_tile)
    tok_b, prev_b, next_b = specs(_bwd_tile)
    out = jax.ShapeDtypeStruct((N_ROWS, D_RNN), BF16)
    tile_f32 = lambda w: pltpu.VMEM((SCAN_ROWS, w), F32)
    return pl.pallas_call(
        _scan_kernel,
        grid=(SCAN_TILES,),
        in_specs=[prev_f, tok_f(D_RNN), next_f, tok_f(D_SSM),
                  prev_b, tok_b(D_RNN), next_b, tok_b(D_SSM)] + [_layer_spec(p.shape, layer) for p in params],
        out_specs=[tok_f(D_RNN), tok_f(D_SSM), tok_b(D_RNN), tok_b(D_SSM)],
        out_shape=[out, out, out, out],
        scratch_shapes=[pltpu.VMEM((SCAN_ROWS + 2 * SUBLANES, D_RNN), F32),
                        pltpu.VMEM((SCAN_ROWS + 2 * SUBLANES, D_RNN), F32),
                        tile_f32(D_RNN), tile_f32(D_RNN), tile_f32(D_RNN), tile_f32(D_RNN),
                        tile_f32(D_SSM), tile_f32(D_SSM), tile_f32(D_SSM), tile_f32(D_SSM),
                        pltpu.VMEM((S5_BLOCKS, SCAN_ROWS, 2 * S5_BLK_STATES), F32),
                        pltpu.VMEM((S5_BLOCKS, SCAN_ROWS, 2 * S5_BLK_STATES), F32),
                        tile_f32(2 * D_SSM), tile_f32(2 * D_SSM),
                        tile_f32(D_RNN), tile_f32(D_RNN), tile_f32(D_SSM), tile_f32(D_SSM),
                        pltpu.VMEM((SUBLANES, D_RNN), F32),
                        pltpu.VMEM((SUBLANES, 2 * N_STATE), F32)],
        compiler_params=pltpu.CompilerParams(
            dimension_semantics=("arbitrary",), vmem_limit_bytes=VMEM_LIMIT),
        name="scan",
    )(ax, ax, ax, u, ax, ax, ax, u, *params)


def _window_sums(load, ha, hb):
    sa = load(-ha)
    for o in range(-ha + 1, ha):
        sa = sa + load(o)
    sb = sa
    for o in list(range(-hb, -ha)) + list(range(ha, hb)):
        sb = sb + load(o)
    return sa, sb


def _count(pos, half, n):
    return (jnp.minimum(pos + half, n) - jnp.maximum(pos - half, 0)).astype(F32)


def _pool_halves():
    is_a = lax.broadcasted_iota(jnp.int32, (1, LANES), 1) < POOL_GROUP
    for half in range(D_POOL // LANES):
        wa, wb = POOL_WINDOWS[2 * half], POOL_WINDOWS[2 * half + 1]
        yield slice(half * LANES, (half + 1) * LANES), wa // 2, wb // 2, is_a


def _pool_ctx_kernel(v_ref, m_ref, cp):
    pad = POOL_PAD * BATCH
    tok = lax.broadcasted_iota(jnp.int32, (CTX_ROWS, 1), 0) // BATCH
    for lanes, ha, hb, is_a in _pool_halves():
        cp[...] = jnp.zeros_like(cp)
        cp[pad:pad + CTX_ROWS, :] = v_ref[:, lanes]
        ca, cb = _window_sums(lambda o: cp[pad + o * BATCH:pad + o * BATCH + CTX_ROWS, :], ha, hb)
        pooled = jnp.where(is_a, ca / _count(tok, ha, CTX_LEN), cb / _count(tok, hb, CTX_LEN))
        m_ref[:, lanes] = (pooled - v_ref[:, lanes]).astype(BF16)


def _pool_ctx(vc):
    return pl.pallas_call(
        _pool_ctx_kernel,
        out_shape=jax.ShapeDtypeStruct((CTX_ROWS, D_POOL), BF16),
        scratch_shapes=[pltpu.VMEM((CTX_ROWS + 2 * POOL_PAD * BATCH, LANES), F32)],
        name="pool_ctx",
    )(vc)


def _pool_kernel(vprev_ref, v_ref, vnext_ref, m_ref, vp, sp):
    i = pl.program_id(0)
    g0 = i * POOL_GROWS
    pad = POOL_PAD * BATCH
    col = lax.broadcasted_iota(jnp.int32, (GROW, 1), 0) // BATCH
    have_prev = i > 0
    have_next = i < pl.num_programs(0) - 1
    for lanes, ha, hb, is_a in _pool_halves():
        vp[...] = jnp.zeros_like(vp)

        def fill(src_ref, first, keep, lanes=lanes):
            def body(q, _):
                dst = pl.multiple_of((first + q) * PROW + pad, SUBLANES)
                src = pl.multiple_of(q * GROW, SUBLANES)
                vp[pl.ds(dst, GROW), :] = jnp.where(keep, src_ref[pl.ds(src, GROW), lanes], 0.0)
                return 0
            return body

        lax.fori_loop(0, POOL_PAD, fill(vprev_ref, 0, have_prev), 0)
        lax.fori_loop(0, POOL_GROWS, fill(v_ref, POOL_PAD, True), 0)
        lax.fori_loop(0, POOL_PAD, fill(vnext_ref, POOL_PAD + POOL_GROWS, have_next), 0)
        ccnt_a, ccnt_b = _count(col, ha, GRID_W), _count(col, hb, GRID_W)

        def body(r, _, lanes=lanes, ha=ha, hb=hb, is_a=is_a, ccnt_a=ccnt_a, ccnt_b=ccnt_b):
            base = (r + POOL_PAD) * PROW
            sa, sb = _window_sums(
                lambda o: vp[pl.ds(pl.multiple_of(base + o * PROW, SUBLANES), PROW), :], ha, hb)
            sp[...] = jnp.where(is_a, sa, sb)
            ba, bb = _window_sums(lambda o: sp[pad + o * BATCH:pad + o * BATCH + GROW, :], ha, hb)
            rcnt_a, rcnt_b = _count(g0 + r, ha, GRID_H), _count(g0 + r, hb, GRID_H)
            pooled = jnp.where(is_a, ba / (rcnt_a * ccnt_a), bb / (rcnt_b * ccnt_b))
            src = pl.multiple_of(r * GROW, GROW)
            m_ref[pl.ds(src, GROW), lanes] = (pooled - v_ref[pl.ds(src, GROW), lanes]).astype(BF16)
            return 0

        lax.fori_loop(0, POOL_GROWS, body, 0)


def _pool(v):
    n = SEQ_ROWS // POOL_TILE
    per = POOL_TILE // POOL_HALO
    nh = SEQ_ROWS // POOL_HALO
    return pl.pallas_call(
        _pool_kernel,
        grid=(n,),
        in_specs=[pl.BlockSpec((POOL_HALO, D_POOL), lambda i: (jnp.maximum(i * per - 1, 0), 0)),
                  pl.BlockSpec((POOL_TILE, D_POOL), lambda i: (i, 0)),
                  pl.BlockSpec((POOL_HALO, D_POOL), lambda i: (jnp.minimum((i + 1) * per, nh - 1), 0))],
        out_specs=pl.BlockSpec((POOL_TILE, D_POOL), lambda i: (i, 0)),
        out_shape=jax.ShapeDtypeStruct((SEQ_ROWS, D_POOL), BF16),
        scratch_shapes=[pltpu.VMEM(((POOL_GROWS + 2 * POOL_PAD) * PROW, LANES), F32),
                        pltpu.VMEM((PROW, LANES), F32)],
        compiler_params=pltpu.CompilerParams(
            dimension_semantics=("parallel",), vmem_limit_bytes=VMEM_LIMIT),
        name="pool",
    )(v, v, v)


def _tail_kernel(final, *refs):
    if final:
        gate_ref, hf_ref, hb_ref, u_ref, ysf_ref, ysb_ref, m_ref, x_ref, mod_ref = refs[:9]
        rest = refs[9:]
    else:
        gate_ref, hf_ref, hb_ref, u_ref, ysf_ref, ysb_ref, m_ref, mc_ref, x_ref, mod_ref = refs[:10]
        rest = refs[10:]
    (dvec_ref, wglu_ref, bglu_ref, pw_ref, pb_ref, ps_ref, wo_ref,
     g2_ref, wg_ref, wu_ref, wd_ref, fg_ref, o_ref) = rest[:13]
    dot = functools.partial(jnp.dot, preferred_element_type=F32)
    f32 = lambda ref: ref[...].astype(F32)

    y_a = jax.nn.gelu(f32(gate_ref)) * (f32(hf_ref) + f32(hb_ref))
    y_s = dvec_ref[...] * f32(u_ref) + f32(ysf_ref) + f32(ysb_ref)
    z = jax.nn.gelu(y_s)
    y_b = z * jax.nn.sigmoid(dot(z.astype(BF16), wglu_ref[...]) + bglu_ref[...])
    m = m_ref[...] if final else jnp.where(pl.program_id(0) < TAIL_CTX_TILES, mc_ref[...], m_ref[...])
    y_c = (dot(m.astype(BF16), pw_ref[...]) + pb_ref[...]) * ps_ref[...]
    y = jnp.concatenate([y_a.astype(BF16), y_b.astype(BF16), y_c.astype(BF16)], axis=1)
    x = x_ref[...] + _rows(mod_ref[2], TAIL_TILE) * dot(y, wo_ref[...])

    h = _rms(x) * g2_ref[...]
    h = (h * (1.0 + _rows(mod_ref[4], TAIL_TILE)) + _rows(mod_ref[3], TAIL_TILE)).astype(BF16)
    ff = None
    for c0, c1 in zip(FF_EDGES[:-1], FF_EDGES[1:]):
        cols = slice(c0, c1)
        act = (jax.nn.silu(dot(h, wg_ref[:, cols])) * dot(h, wu_ref[:, cols])).astype(BF16)
        part = dot(act, wd_ref[cols, :])
        ff = part if ff is None else ff + part
    y = x + _rows(mod_ref[5], TAIL_TILE) * ff
    if not final:
        o_ref[...] = y
        return
    y = _rms(y) * fg_ref[...]
    slab = rest[13]
    rows, toks = slab.shape[1], slab.shape[1] // BATCH
    for part in range(TAIL_TILE // rows):
        for s in range(D_MODEL // LANES):
            slab[s] = y[part * rows:(part + 1) * rows, s * LANES:(s + 1) * LANES]
        for b in range(BATCH):
            for s in range(D_MODEL // LANES):
                o_ref[b, part * toks:(part + 1) * toks, s * LANES:(s + 1) * LANES] = (
                    slab[s, pl.ds(b, toks, stride=BATCH), :])


def _tail(final, layer, gate, hf, hb, u, ysf, ysb, m, mc, xt, modtab, weights):
    off = TAIL_CTX_TILES if final else 0
    n_tiles = N_ROWS // TAIL_TILE - off
    tok = lambda w: pl.BlockSpec((TAIL_TILE, w), lambda i: (i + off, 0))
    acts = [gate, hf, hb, u, ysf, ysb, m]
    act_specs = [tok(D_RNN), tok(D_RNN), tok(D_RNN), tok(D_SSM), tok(D_SSM), tok(D_SSM),
                 pl.BlockSpec((TAIL_TILE, D_POOL), lambda i: (jnp.maximum(i + off - TAIL_CTX_TILES, 0), 0))]
    if final:
        toks = TAIL_TILE // BATCH
        out_spec = pl.BlockSpec((BATCH, toks, D_MODEL), lambda i: (0, i, 0))
        out_shape = jax.ShapeDtypeStruct((BATCH, n_tiles * toks, D_MODEL), F32)
        scratch = [pltpu.VMEM((D_MODEL // LANES, TAIL_SLAB, LANES), F32)]
    else:
        acts.append(mc)
        act_specs.append(pl.BlockSpec((TAIL_TILE, D_POOL), lambda i: (jnp.minimum(i, TAIL_CTX_TILES - 1), 0)))
        out_spec = pl.BlockSpec((TAIL_TILE, D_MODEL), lambda i: (i, 0))
        out_shape = jax.ShapeDtypeStruct((N_ROWS, D_MODEL), F32)
        scratch = []
    return pl.pallas_call(
        functools.partial(_tail_kernel, final),
        grid=(n_tiles,),
        in_specs=(act_specs + [tok(D_MODEL), _mod_spec(layer, off, TAIL_CTX_TILES)]
                  + [_const_spec(w.shape) if w.ndim == 2 else _layer_spec(w.shape, layer) for w in weights]),
        out_specs=out_spec,
        out_shape=out_shape,
        scratch_shapes=scratch,
        compiler_params=pltpu.CompilerParams(
            dimension_semantics=("parallel",), vmem_limit_bytes=TAIL_VMEM_LIMIT),
        name="tail_final" if final else "tail",
    )(*acts, xt, modtab, *weights)


def _block_diag(w):
    n, di, dj = w.shape[-3:]
    out = jnp.einsum('...nij,nm->...nimj', w, jnp.eye(n, dtype=w.dtype))
    return out.reshape(w.shape[:-3] + (n * di, n * dj))


def _s5_params(lam_re, lam_im, log_dt, b_re, b_im, c_re, c_im):
    gpb = SSM_GROUPS // S5_BLOCKS
    depth = lam_re.shape[0]
    f32 = lambda t: t.astype(F32)

    in_blocks = lambda w: _block_diag(
        jnp.swapaxes(f32(w), -1, -2).reshape(depth, S5_BLOCKS, gpb, SSM_GROUP, SSM_STATE))
    out_blocks = lambda w: _block_diag(
        jnp.swapaxes(w, -1, -2).reshape(depth, 2, S5_BLOCKS, gpb, SSM_STATE, SSM_GROUP))

    bmat = jnp.concatenate([in_blocks(b_re), in_blocks(b_im)], axis=-1).astype(BF16)
    lr, li = f32(lam_re), f32(lam_im)
    dt = jnp.exp(f32(log_dt))[..., None]
    mag = jnp.exp(lr * dt)
    ang = li * dt
    bar_r, bar_i = mag * jnp.cos(ang), mag * jnp.sin(ang)
    den = lr * lr + li * li
    fr = ((bar_r - 1.0) * lr + bar_i * li) / den
    fi = (bar_i * lr - (bar_r - 1.0) * li) / den
    cr, ci = f32(c_re), f32(c_im)
    cfr = cr * fr[..., None, :] - ci * fi[..., None, :]
    cfi = cr * fi[..., None, :] + ci * fr[..., None, :]
    cm = jnp.concatenate([out_blocks(cfr), out_blocks(-cfi)], axis=-2)
    cmat = jnp.concatenate([cm[:, 0], cm[:, 1]], axis=-1).astype(BF16)
    lam = jnp.stack([bar_r.reshape(depth, 2, N_STATE), bar_i.reshape(depth, 2, N_STATE)], axis=1)
    lam = jnp.repeat(lam, HALF, axis=2)
    return bmat, cmat, lam


def kernel(x, c, ctx, c_ctx, w_mod, b_mod, norm1_g, norm2_g, w_in, w_out, lru_conv_w, lru_conv_b,
           lru_wa, lru_ba, lru_wi, lru_bi, lru_lambda, s5_lambda_re, s5_lambda_im, s5_log_dt,
           s5_b_re, s5_b_im, s5_c_re, s5_c_im, s5_d, s5_glu_w, s5_glu_b, pool_w, pool_b, pool_scale,
           ffn_w_gate, ffn_w_up, ffn_w_down, final_g):
    cc = jnp.zeros((SUBLANES, D_MODEL), F32).at[:BATCH].set(c).at[BATCH].set(c_ctx)
    mod = _modulation(cc, w_mod, b_mod)
    xt = (ctx, x)
    fg = final_g.reshape(1, D_MODEL)
    w_in_b, w_out_b, glu_b = w_in.astype(BF16), w_out.astype(BF16), s5_glu_w.astype(BF16)
    ffn_gate_b, ffn_up_b, ffn_down_b = (w.astype(BF16) for w in (ffn_w_gate, ffn_w_up, ffn_w_down))

    row = lambda t: t.reshape(t.shape[:-1] + (1, t.shape[-1]))
    lat = jnp.transpose(mod[:, :BATCH].reshape(DEPTH, BATCH, 6, D_MODEL), (0, 2, 1, 3))
    lat = jnp.concatenate([lat, lat], axis=2)
    cm = jnp.broadcast_to(mod[:, BATCH].reshape(DEPTH, 6, 1, D_MODEL), (DEPTH, 6, SUBLANES, D_MODEL))
    modtab = jnp.stack([cm, lat], axis=1)
    heads = lru_wa.shape[2]
    pairs = lambda w: _block_diag(w.reshape(DEPTH, 2, D_RNN // LANES, heads * LANES // D_RNN,
                                            D_RNN // heads, D_RNN // heads))
    wg = (-LOG2_E * jnp.concatenate([pairs(lru_wa), pairs(lru_wi)], axis=-1)).astype(BF16)
    bg = -LOG2_E * row(jnp.concatenate([lru_ba, lru_bi], axis=-1))
    spl = row(-LRU_C * LOG2_E * jax.nn.softplus(-lru_lambda.astype(F32)))
    scan_params = (lru_conv_w, row(lru_conv_b), wg, bg, spl) + _s5_params(
        s5_lambda_re, s5_lambda_im, s5_log_dt, s5_b_re, s5_b_im, s5_c_re, s5_c_im)
    weights = (row(s5_d), glu_b, row(s5_glu_b), _block_diag(pool_w).astype(BF16), row(pool_b),
               row(pool_scale), w_out_b, row(norm2_g), ffn_gate_b, ffn_up_b, ffn_down_b, fg)
    g1 = row(norm1_g)

    for l in range(DEPTH):
        last = l == DEPTH - 1
        xt, ax, u, gate, v, vc = _inproj(xt, modtab, g1, w_in_b, l)
        hf, ysf, hb, ysb = _scan(l, ax, u, *scan_params)
        m = _pool(v)
        mc = None if last else _pool_ctx(vc)
        xt = _tail(last, l, gate, hf, hb, u, ysf, ysb, m, mc, xt, modtab, weights)
    return xt
```

```python
import functools

import jax
import jax.numpy as jnp
from jax import lax
from jax.experimental import pallas as pl
from jax.experimental.pallas import tpu as pltpu

F32 = jnp.float32
BF16 = jnp.bfloat16

D_MODEL = 1024
BATCH = 4
SEQ = 8192
DEPTH = 2
CTX_LEN = 256
GRID_W = 64
GRID_H = SEQ // GRID_W
EPS = 1e-6

D_RNN = 384
CONV_WIDTH = 4
LRU_C = 8.0
LOG2_E = 1.4426950408889634

D_SSM = 384
SSM_GROUP = 16
SSM_GROUPS = D_SSM // SSM_GROUP
SSM_STATE = 64
N_STATE = SSM_GROUPS * SSM_STATE

D_POOL = 256
POOL_WINDOWS = (2, 4, 8, 16)
POOL_GROUP = D_POOL // len(POOL_WINDOWS)
POOL_PAD = 8

D_IN = 2 * D_RNN + D_SSM + D_POOL
D_MIX = D_RNN + D_SSM + D_POOL
D_FF = 2816
MXU_DIM = 256
FF_EDGES = (0, 3 * MXU_DIM, 6 * MXU_DIM, 9 * MXU_DIM, D_FF)

LANES = 128
SUBLANES = 8
HALF = SUBLANES // 2
assert BATCH == HALF

N_TOK = CTX_LEN + SEQ
N_ROWS = N_TOK * BATCH
CTX_ROWS = CTX_LEN * BATCH
SEQ_ROWS = SEQ * BATCH

ROW_TILE = 1024
N_ROW_TILES = N_ROWS // ROW_TILE
assert CTX_ROWS == ROW_TILE
TAIL_TILE = 1024
TAIL_CTX_TILES = CTX_ROWS // TAIL_TILE
TAIL_SLAB = 512

SCAN_TOK = 128
SCAN_ROWS = SCAN_TOK * BATCH
SCAN_TILES = N_TOK // SCAN_TOK
SCAN_CTX_TILES = CTX_LEN // SCAN_TOK
SCAN_BLKS = SCAN_ROWS // SUBLANES
S5_BLOCKS = D_SSM // LANES
S5_BLK_STATES = N_STATE // S5_BLOCKS
SCAN_SEGS = 4

POOL_GROWS = 16
POOL_TILE = POOL_GROWS * GRID_W * BATCH
POOL_HALO = POOL_PAD * GRID_W * BATCH
GROW = GRID_W * BATCH
PROW = (GRID_W + 2 * POOL_PAD) * BATCH

VMEM_LIMIT = 56 * 1024 * 1024
TAIL_VMEM_LIMIT = 62 * 1024 * 1024


def _const_spec(shape):
    nd = len(shape)
    return pl.BlockSpec(shape, lambda *_: (0,) * nd, pipeline_mode=pl.Buffered(1))


def _layer_spec(shape, layer):
    nd = len(shape) - 1
    return pl.BlockSpec((None,) + tuple(shape[1:]), lambda *_: (layer,) + (0,) * nd,
                        pipeline_mode=pl.Buffered(1))


def _rms(x):
    return x * lax.rsqrt(jnp.mean(x * x, axis=-1, keepdims=True) + EPS)


def _rows(pat, n):
    return jnp.broadcast_to(pat[None], (n // SUBLANES,) + pat.shape).reshape(n, pat.shape[-1])


def _mod_spec(layer, off=0, ctx_tiles=1):
    return pl.BlockSpec((None, None, 6, SUBLANES, D_MODEL),
                        lambda i: (layer, jnp.minimum((i + off) // ctx_tiles, 1), 0, 0, 0))


def _mod_kernel(c_ref, w_ref, b_ref, o_ref):
    s = jax.nn.silu(c_ref[...])
    o_ref[...] = jnp.dot(s.astype(BF16), w_ref[...].astype(BF16),
                         preferred_element_type=F32) + b_ref[...]


def _modulation(cc, w_mod, b_mod):
    n_col = 6 * D_MODEL
    blk = 1536
    return pl.pallas_call(
        _mod_kernel,
        grid=(DEPTH, n_col // blk),
        in_specs=[pl.BlockSpec((SUBLANES, D_MODEL), lambda l, j: (0, 0)),
                  pl.BlockSpec((None, D_MODEL, blk), lambda l, j: (l, 0, j)),
                  pl.BlockSpec((None, 1, blk), lambda l, j: (l, 0, j))],
        out_specs=pl.BlockSpec((None, SUBLANES, blk), lambda l, j: (l, 0, j)),
        out_shape=jax.ShapeDtypeStruct((DEPTH, SUBLANES, n_col), F32),
        compiler_params=pltpu.CompilerParams(vmem_limit_bytes=VMEM_LIMIT),
        name="modulation",
    )(cc, w_mod, b_mod.reshape(DEPTH, 1, n_col))


def _inproj_body(x, mod_ref, g_ref, w_ref, ax_ref, u_ref, gate_ref, v_ref, vc_ref):
    h = _rms(x) * g_ref[...]
    h = h * (1.0 + _rows(mod_ref[1], ROW_TILE)) + _rows(mod_ref[0], ROW_TILE)
    p = jnp.dot(h.astype(BF16), w_ref[...], preferred_element_type=F32)
    ax_ref[...] = p[:, 0:D_RNN]
    u_ref[...] = p[:, D_RNN:D_RNN + D_SSM].astype(BF16)
    gate_ref[...] = p[:, D_RNN + D_SSM:2 * D_RNN + D_SSM].astype(BF16)
    v = p[:, 2 * D_RNN + D_SSM:]
    v_ref[...] = v

    @pl.when(pl.program_id(0) == 0)
    def _():
        vc_ref[...] = v


X_RING = 3


def _inproj_kernel(x_hbm, *rest):
    refs, xbuf, sem = rest[:-2], rest[-2], rest[-1]
    i = pl.program_id(0)

    def fetch(tile, slot):
        rows = pl.ds(pl.multiple_of(tile * ROW_TILE, ROW_TILE), ROW_TILE)
        return pltpu.make_async_copy(x_hbm.at[rows, :], xbuf.at[slot], sem.at[slot])

    @pl.when(i == 0)
    def _():
        for t in range(X_RING - 1):
            fetch(t, t).start()

    ahead = i + X_RING - 1

    @pl.when(ahead < N_ROW_TILES)
    def _():
        fetch(ahead, ahead % X_RING).start()

    slot = i % X_RING
    fetch(i, slot).wait()
    _inproj_body(xbuf[slot], *refs)


def _inproj_first_kernel(ctx_ref, x_ref, mod_ref, g_ref, w_ref, xt_ref, *rest):
    slab = rest[-1]
    is_ctx = pl.program_id(0) == 0
    for b in range(BATCH):
        for s in range(D_MODEL // LANES):
            lanes = slice(s * LANES, (s + 1) * LANES)
            src = jnp.where(is_ctx, ctx_ref[b, :, lanes], x_ref[b, :, lanes])
            slab[s, pl.ds(b, ROW_TILE // BATCH, stride=BATCH), :] = src
    x = jnp.concatenate([slab[s] for s in range(D_MODEL // LANES)], axis=1)
    xt_ref[...] = x
    _inproj_body(x, mod_ref, g_ref, w_ref, *rest[:-1])


def _inproj(xt, modtab, g, w_in, layer):
    first = isinstance(xt, tuple)
    tok = lambda w: pl.BlockSpec((ROW_TILE, w), lambda i: (i, 0))
    out = lambda w, dt=F32: jax.ShapeDtypeStruct((N_ROWS, w), dt)
    out_specs = [tok(D_RNN), tok(D_SSM), tok(D_RNN),
                 pl.BlockSpec((ROW_TILE, D_POOL), lambda i: (jnp.maximum(i - 1, 0), 0)),
                 pl.BlockSpec((CTX_ROWS, D_POOL), lambda i: (0, 0))]
    out_shape = [out(D_RNN), out(D_SSM, BF16), out(D_RNN, BF16),
                 jax.ShapeDtypeStruct((SEQ_ROWS, D_POOL), F32),
                 jax.ShapeDtypeStruct((CTX_ROWS, D_POOL), F32)]
    common = [_mod_spec(layer), _layer_spec(g.shape, layer), _layer_spec(w_in.shape, layer)]
    params = pltpu.CompilerParams(dimension_semantics=("arbitrary",), vmem_limit_bytes=VMEM_LIMIT)
    if not first:
        return (xt,) + tuple(pl.pallas_call(
            _inproj_kernel, grid=(N_ROW_TILES,),
            in_specs=[pl.BlockSpec(memory_space=pl.ANY)] + common,
            out_specs=out_specs, out_shape=out_shape,
            scratch_shapes=[pltpu.VMEM((X_RING, ROW_TILE, D_MODEL), F32), pltpu.SemaphoreType.DMA((X_RING,))],
            compiler_params=params, name="inproj",
        )(xt, modtab, g, w_in))
    toks = ROW_TILE // BATCH
    src = [pl.BlockSpec((BATCH, CTX_LEN, D_MODEL), lambda i: (0, 0, 0), pipeline_mode=pl.Buffered(1)),
           pl.BlockSpec((BATCH, toks, D_MODEL), lambda i: (0, jnp.maximum(i - 1, 0), 0))]
    return tuple(pl.pallas_call(
        _inproj_first_kernel, grid=(N_ROW_TILES,), in_specs=src + common,
        out_specs=[tok(D_MODEL)] + out_specs, out_shape=[out(D_MODEL)] + out_shape,
        scratch_shapes=[pltpu.VMEM((D_MODEL // LANES, ROW_TILE, LANES), F32)],
        compiler_params=params, name="inproj_first",
    )(xt[0], xt[1], modtab, g, w_in))


def _fwd_tile(i):
    return i


def _bwd_tile(i):
    return jnp.where(i < SCAN_CTX_TILES, SCAN_CTX_TILES - 1 - i, SCAN_TILES + SCAN_CTX_TILES - 1 - i)


def _scan_kernel(axp_f, ax_f, axn_f, u_f, axp_b, ax_b, axn_b, u_b,
                 cw_ref, cb_ref, wg_ref, bg_ref, spl_ref, bmat_ref, cmat_ref, lam_ref,
                 hf_ref, ysf_ref, hb_ref, ysb_ref,
                 axs_f, axs_b, a_f, b_f, a_b, b_b, uf32, ub32, us1, us2, xs1, xs2, ys1, ys2,
                 hfs, hbs, yfs, ybs, lcar, scar):
    i = pl.program_id(0)
    uf32[...] = u_f[...].astype(F32)
    ub32[...] = u_b[...].astype(F32)

    @pl.when(i == 0)
    def _():
        lcar[...] = jnp.zeros_like(lcar)
        scar[...] = jnp.zeros_like(scar)

    def stage_conv(tile, axp, ax, axn, axs):
        prev_ok = jnp.logical_and(tile != 0, tile != SCAN_CTX_TILES)
        next_ok = jnp.logical_and(tile != SCAN_CTX_TILES - 1, tile != SCAN_TILES - 1)
        axs[0:SUBLANES, :] = jnp.where(prev_ok, axp[...], 0.0)
        axs[SUBLANES:SUBLANES + SCAN_ROWS, :] = ax[...]
        axs[SUBLANES + SCAN_ROWS:, :] = jnp.where(next_ok, axn[...], 0.0)

    def gates(d, k, axs, a_out, b_out):
        lanes = slice(k * LANES, (k + 1) * LANES)
        xc = cb_ref[:, lanes] + axs[0:SCAN_ROWS, lanes] * cw_ref[0:1, lanes]
        for t in range(1, CONV_WIDTH):
            xc = xc + axs[t * BATCH:t * BATCH + SCAN_ROWS, lanes] * cw_ref[t:t + 1, lanes]
        g = jnp.dot(xc.astype(BF16), wg_ref[d, k], preferred_element_type=F32)
        r = 1.0 / (1.0 + jnp.exp2(g[:, :LANES] + bg_ref[d, :, lanes]))
        ig = 1.0 / (1.0 + jnp.exp2(g[:, LANES:] + bg_ref[d, :, D_RNN + k * LANES:D_RNN + (k + 1) * LANES]))
        a = jnp.exp2(r * spl_ref[d, :, lanes])
        a_out[:, lanes] = a
        b_out[:, lanes] = jnp.sqrt(1.0 - a * a) * (ig * xc)

    lo = lax.broadcasted_iota(jnp.int32, (SUBLANES, 1), 0) < HALF

    def swap(x):
        return pltpu.roll(x, HALF, 0)

    def rows_of(j):
        jb = SCAN_BLKS - 1 - j
        return slice(j * SUBLANES, (j + 1) * SUBLANES), slice(jb * SUBLANES, (jb + 1) * SUBLANES)

    def steps(vf, vb):
        return jnp.where(lo, vf, vb), swap(jnp.where(lo, vb, vf))

    def unsteps(h1, h2):
        h2s = swap(h2)
        return jnp.where(lo, h1, h2s), jnp.where(lo, h2s, h1)

    def lru_scan(h, j0, j1):
        for j in range(j0, j1):
            rf, rb = rows_of(j)
            a1, a2 = steps(a_f[rf, :], a_b[rb, :])
            b1, b2 = steps(b_f[rf, :], b_b[rb, :])
            h1 = a1 * h + b1
            h = a2 * h1 + b2
            hfs[rf, :], hbs[rb, :] = unsteps(h1, h)
        return h

    for j in range(SCAN_BLKS):
        rf, rb = rows_of(j)
        us1[rf, :], us2[rf, :] = steps(uf32[rf, :], ub32[rb, :])
    stage_conv(_fwd_tile(i), axp_f, ax_f, axn_f, axs_f)
    stage_conv(_bwd_tile(i), axp_b, ax_b, axn_b, axs_b)

    nst = S5_BLK_STATES

    def s5_scan(jb, carry, j0, j1):
        x1, x2 = xs1.at[jb], xs2.at[jb]
        re, im = slice(0, nst), slice(nst, 2 * nst)
        lr, li = lam_ref[0, :, jb * nst:(jb + 1) * nst], lam_ref[1, :, jb * nst:(jb + 1) * nst]
        sr, si = carry
        for j in range(j0, j1):
            rows = slice(j * SUBLANES, (j + 1) * SUBLANES)
            r1 = lr * sr - li * si + x1[rows, re]
            i1 = lr * si + li * sr + x1[rows, im]
            sr = lr * r1 - li * i1 + x2[rows, re]
            si = lr * i1 + li * r1 + x2[rows, im]
            x1[rows, re] = r1
            x1[rows, im] = i1
            x2[rows, re] = sr
            x2[rows, im] = si
        return sr, si

    def out_proj(jb, piece):
        xs, ys = ((xs1, ys1), (xs2, ys2))[piece // 2]
        half = slice(0, nst) if piece % 2 == 0 else slice(nst, 2 * nst)
        cols = slice(2 * jb * LANES, 2 * (jb + 1) * LANES)
        y = jnp.dot(xs[jb, :, half].astype(BF16), cmat_ref[jb, half, :], preferred_element_type=F32)
        ys[:, cols] = y if piece % 2 == 0 else ys[:, cols] + y

    for jb in range(S5_BLOCKS):
        blk = slice(jb * LANES, (jb + 1) * LANES)
        gates(0, jb, axs_f, a_f, b_f)
        xs1[jb] = jnp.dot(us1[:, blk].astype(BF16), bmat_ref[jb], preferred_element_type=F32)
        gates(1, jb, axs_b, a_b, b_b)
        xs2[jb] = jnp.dot(us2[:, blk].astype(BF16), bmat_ref[jb], preferred_element_type=F32)
    seg = SCAN_BLKS // SCAN_SEGS
    for jb in range(S5_BLOCKS):
        carry = (scar[:, jb * nst:(jb + 1) * nst], scar[:, N_STATE + jb * nst:N_STATE + (jb + 1) * nst])
        for q in range(SCAN_SEGS):
            carry = s5_scan(jb, carry, q * seg, (q + 1) * seg)
            if jb > 0:
                out_proj(jb - 1, q)
        scar[:, jb * nst:(jb + 1) * nst] = carry[0]
        scar[:, N_STATE + jb * nst:N_STATE + (jb + 1) * nst] = carry[1]
    h = lcar[...]
    for q in range(SCAN_SEGS):
        h = lru_scan(h, q * seg, (q + 1) * seg)
        out_proj(S5_BLOCKS - 1, q)
    lcar[...] = h
    hf_ref[...] = hfs[...].astype(BF16)
    hb_ref[...] = hbs[...].astype(BF16)

    for j in range(SCAN_BLKS):
        rf, rb = rows_of(j)
        of, ob = unsteps(ys1[rf, :], ys2[rf, :])
        for jb in range(S5_BLOCKS):
            yfs[rf, jb * LANES:(jb + 1) * LANES] = of[:, 2 * jb * LANES:(2 * jb + 1) * LANES]
            ybs[rb, jb * LANES:(jb + 1) * LANES] = ob[:, (2 * jb + 1) * LANES:2 * (jb + 1) * LANES]
    ysf_ref[...] = yfs[...].astype(BF16)
    ysb_ref[...] = ybs[...].astype(BF16)


def _scan(layer, ax, u, *params):
    tpb = SCAN_ROWS // SUBLANES
    nb8 = N_ROWS // SUBLANES

    def specs(tile_of):
        tok = lambda w: pl.BlockSpec((SCAN_ROWS, w), lambda i: (tile_of(i), 0))
        prev = pl.BlockSpec((SUBLANES, D_RNN), lambda i: (jnp.maximum(tile_of(i) * tpb - 1, 0), 0))
        nxt = pl.BlockSpec((SUBLANES, D_RNN), lambda i: (jnp.minimum((tile_of(i) + 1) * tpb, nb8 - 1), 0))
        return tok, prev, nxt

    tok_f, prev_f, next_f = specs(_fwd_tile)
    tok_b, prev_b, next_b = specs(_bwd_tile)
    out = jax.ShapeDtypeStruct((N_ROWS, D_RNN), BF16)
    tile_f32 = lambda w: pltpu.VMEM((SCAN_ROWS, w), F32)
    return pl.pallas_call(
        _scan_kernel,
        grid=(SCAN_TILES,),
        in_specs=[prev_f, tok_f(D_RNN), next_f, tok_f(D_SSM),
                  prev_b, tok_b(D_RNN), next_b, tok_b(D_SSM)] + [_layer_spec(p.shape, layer) for p in params],
        out_specs=[tok_f(D_RNN), tok_f(D_SSM), tok_b(D_RNN), tok_b(D_SSM)],
        out_shape=[out, out, out, out],
        scratch_shapes=[pltpu.VMEM((SCAN_ROWS + 2 * SUBLANES, D_RNN), F32),
                        pltpu.VMEM((SCAN_ROWS + 2 * SUBLANES, D_RNN), F32),
                        tile_f32(D_RNN), tile_f32(D_RNN), tile_f32(D_RNN), tile_f32(D_RNN),
                        tile_f32(D_SSM), tile_f32(D_SSM), tile_f32(D_SSM), tile_f32(D_SSM),
                        pltpu.VMEM((S5_BLOCKS, SCAN_ROWS, 2 * S5_BLK_STATES), F32),
                        pltpu.VMEM((S5_BLOCKS, SCAN_ROWS, 2 * S5_BLK_STATES), F32),
                        tile_f32(2 * D_SSM), tile_f32(2 * D_SSM),
                        tile_f32(D_RNN), tile_f32(D_RNN), tile_f32(D_SSM), tile_f32(D_SSM),
                        pltpu.VMEM((SUBLANES, D_RNN), F32),
                        pltpu.VMEM((SUBLANES, 2 * N_STATE), F32)],
        compiler_params=pltpu.CompilerParams(
            dimension_semantics=("arbitrary",), vmem_limit_bytes=VMEM_LIMIT),
        name="scan",
    )(ax, ax, ax, u, ax, ax, ax, u, *params)


def _window_sums(load, ha, hb):
    sa = load(-ha)
    for o in range(-ha + 1, ha):
        sa = sa + load(o)
    sb = sa
    for o in list(range(-hb, -ha)) + list(range(ha, hb)):
        sb = sb + load(o)
    return sa, sb


def _count(pos, half, n):
    return (jnp.minimum(pos + half, n) - jnp.maximum(pos - half, 0)).astype(F32)


def _pool_halves():
    is_a = lax.broadcasted_iota(jnp.int32, (1, LANES), 1) < POOL_GROUP
    for half in range(D_POOL // LANES):
        wa, wb = POOL_WINDOWS[2 * half], POOL_WINDOWS[2 * half + 1]
        yield slice(half * LANES, (half + 1) * LANES), wa // 2, wb // 2, is_a


def _pool_ctx_kernel(v_ref, m_ref, cp):
    pad = POOL_PAD * BATCH
    tok = lax.broadcasted_iota(jnp.int32, (CTX_ROWS, 1), 0) // BATCH
    for lanes, ha, hb, is_a in _pool_halves():
        cp[...] = jnp.zeros_like(cp)
        cp[pad:pad + CTX_ROWS, :] = v_ref[:, lanes]
        ca, cb = _window_sums(lambda o: cp[pad + o * BATCH:pad + o * BATCH + CTX_ROWS, :], ha, hb)
        pooled = jnp.where(is_a, ca / _count(tok, ha, CTX_LEN), cb / _count(tok, hb, CTX_LEN))
        m_ref[:, lanes] = (pooled - v_ref[:, lanes]).astype(BF16)


def _pool_ctx(vc):
    return pl.pallas_call(
        _pool_ctx_kernel,
        out_shape=jax.ShapeDtypeStruct((CTX_ROWS, D_POOL), BF16),
        scratch_shapes=[pltpu.VMEM((CTX_ROWS + 2 * POOL_PAD * BATCH, LANES), F32)],
        name="pool_ctx",
    )(vc)


def _pool_kernel(vprev_ref, v_ref, vnext_ref, m_ref, vp, sp):
    i = pl.program_id(0)
    g0 = i * POOL_GROWS
    pad = POOL_PAD * BATCH
    col = lax.broadcasted_iota(jnp.int32, (GROW, 1), 0) // BATCH
    have_prev = i > 0
    have_next = i < pl.num_programs(0) - 1

    @pl.when(i == 0)
    def _():
        vp[...] = jnp.zeros_like(vp)

    for lanes, ha, hb, is_a in _pool_halves():

        def fill(src_ref, first, keep, lanes=lanes):
            def body(q, _):
                dst = pl.multiple_of((first + q) * PROW + pad, SUBLANES)
                src = pl.multiple_of(q * GROW, SUBLANES)
                vp[pl.ds(dst, GROW), :] = jnp.where(keep, src_ref[pl.ds(src, GROW), lanes], 0.0)
                return 0
            return body

        lax.fori_loop(0, POOL_PAD, fill(vprev_ref, 0, have_prev), 0)
        lax.fori_loop(0, POOL_GROWS, fill(v_ref, POOL_PAD, True), 0)
        lax.fori_loop(0, POOL_PAD, fill(vnext_ref, POOL_PAD + POOL_GROWS, have_next), 0)
        ccnt_a, ccnt_b = _count(col, ha, GRID_W), _count(col, hb, GRID_W)

        def body(r, _, lanes=lanes, ha=ha, hb=hb, is_a=is_a, ccnt_a=ccnt_a, ccnt_b=ccnt_b):
            base = (r + POOL_PAD) * PROW
            sa, sb = _window_sums(
                lambda o: vp[pl.ds(pl.multiple_of(base + o * PROW, SUBLANES), PROW), :], ha, hb)
            sp[...] = jnp.where(is_a, sa, sb)
            ba, bb = _window_sums(lambda o: sp[pad + o * BATCH:pad + o * BATCH + GROW, :], ha, hb)
            rcnt_a, rcnt_b = _count(g0 + r, ha, GRID_H), _count(g0 + r, hb, GRID_H)
            pooled = jnp.where(is_a, ba / (rcnt_a * ccnt_a), bb / (rcnt_b * ccnt_b))
            src = pl.multiple_of(r * GROW, GROW)
            m_ref[pl.ds(src, GROW), lanes] = (pooled - v_ref[pl.ds(src, GROW), lanes]).astype(BF16)
            return 0

        lax.fori_loop(0, POOL_GROWS, body, 0)


def _pool(v):
    n = SEQ_ROWS // POOL_TILE
    per = POOL_TILE // POOL_HALO
    nh = SEQ_ROWS // POOL_HALO
    return pl.pallas_call(
        _pool_kernel,
        grid=(n,),
        in_specs=[pl.BlockSpec((POOL_HALO, D_POOL), lambda i: (jnp.maximum(i * per - 1, 0), 0)),
                  pl.BlockSpec((POOL_TILE, D_POOL), lambda i: (i, 0)),
                  pl.BlockSpec((POOL_HALO, D_POOL), lambda i: (jnp.minimum((i + 1) * per, nh - 1), 0))],
        out_specs=pl.BlockSpec((POOL_TILE, D_POOL), lambda i: (i, 0)),
        out_shape=jax.ShapeDtypeStruct((SEQ_ROWS, D_POOL), BF16),
        scratch_shapes=[pltpu.VMEM(((POOL_GROWS + 2 * POOL_PAD) * PROW, LANES), F32),
                        pltpu.VMEM((PROW, LANES), F32)],
        compiler_params=pltpu.CompilerParams(
            dimension_semantics=("arbitrary",), vmem_limit_bytes=VMEM_LIMIT),
        name="pool",
    )(v, v, v)


def _tail_kernel(final, *refs):
    if final:
        gate_ref, hf_ref, hb_ref, u_ref, ysf_ref, ysb_ref, m_ref, x_ref, mod_ref = refs[:9]
        rest = refs[9:]
    else:
        gate_ref, hf_ref, hb_ref, u_ref, ysf_ref, ysb_ref, m_ref, mc_ref, x_ref, mod_ref = refs[:10]
        rest = refs[10:]
    (dvec_ref, wglu_ref, bglu_ref, pw_ref, pb_ref, ps_ref, wo_ref,
     g2_ref, wg_ref, wu_ref, wd_ref, fg_ref, o_ref) = rest[:13]
    dot = functools.partial(jnp.dot, preferred_element_type=F32)
    f32 = lambda ref: ref[...].astype(F32)

    y_a = jax.nn.gelu(f32(gate_ref)) * (f32(hf_ref) + f32(hb_ref))
    y_s = dvec_ref[...] * f32(u_ref) + f32(ysf_ref) + f32(ysb_ref)
    z = jax.nn.gelu(y_s)
    y_b = z * jax.nn.sigmoid(dot(z.astype(BF16), wglu_ref[...]) + bglu_ref[...])
    m = m_ref[...] if final else jnp.where(pl.program_id(0) < TAIL_CTX_TILES, mc_ref[...], m_ref[...])
    y_c = (dot(m.astype(BF16), pw_ref[...]) + pb_ref[...]) * ps_ref[...]
    y = jnp.concatenate([y_a.astype(BF16), y_b.astype(BF16), y_c.astype(BF16)], axis=1)
    x = x_ref[...] + _rows(mod_ref[2], TAIL_TILE) * dot(y, wo_ref[...])

    h = _rms(x) * g2_ref[...]
    h = (h * (1.0 + _rows(mod_ref[4], TAIL_TILE)) + _rows(mod_ref[3], TAIL_TILE)).astype(BF16)
    ff = None
    for c0, c1 in zip(FF_EDGES[:-1], FF_EDGES[1:]):
        cols = slice(c0, c1)
        act = (jax.nn.silu(dot(h, wg_ref[:, cols])) * dot(h, wu_ref[:, cols])).astype(BF16)
        part = dot(act, wd_ref[cols, :])
        ff = part if ff is None else ff + part
    y = x + _rows(mod_ref[5], TAIL_TILE) * ff
    if not final:
        o_ref[...] = y
        return
    y = _rms(y) * fg_ref[...]
    slab = rest[13]
    rows, toks = slab.shape[1], slab.shape[1] // BATCH
    for part in range(TAIL_TILE // rows):
        for s in range(D_MODEL // LANES):
            slab[s] = y[part * rows:(part + 1) * rows, s * LANES:(s + 1) * LANES]
        for b in range(BATCH):
            for s in range(D_MODEL // LANES):
                o_ref[b, part * toks:(part + 1) * toks, s * LANES:(s + 1) * LANES] = (
                    slab[s, pl.ds(b, toks, stride=BATCH), :])


def _tail(final, layer, gate, hf, hb, u, ysf, ysb, m, mc, xt, modtab, weights):
    off = TAIL_CTX_TILES if final else 0
    n_tiles = N_ROWS // TAIL_TILE - off
    tok = lambda w: pl.BlockSpec((TAIL_TILE, w), lambda i: (i + off, 0))
    acts = [gate, hf, hb, u, ysf, ysb, m]
    act_specs = [tok(D_RNN), tok(D_RNN), tok(D_RNN), tok(D_SSM), tok(D_SSM), tok(D_SSM),
                 pl.BlockSpec((TAIL_TILE, D_POOL), lambda i: (jnp.maximum(i + off - TAIL_CTX_TILES, 0), 0))]
    if final:
        toks = TAIL_TILE // BATCH
        out_spec = pl.BlockSpec((BATCH, toks, D_MODEL), lambda i: (0, i, 0))
        out_shape = jax.ShapeDtypeStruct((BATCH, n_tiles * toks, D_MODEL), F32)
        scratch = [pltpu.VMEM((D_MODEL // LANES, TAIL_SLAB, LANES), F32)]
    else:
        acts.append(mc)
        act_specs.append(pl.BlockSpec((TAIL_TILE, D_POOL), lambda i: (jnp.minimum(i, TAIL_CTX_TILES - 1), 0)))
        out_spec = pl.BlockSpec((TAIL_TILE, D_MODEL), lambda i: (i, 0))
        out_shape = jax.ShapeDtypeStruct((N_ROWS, D_MODEL), F32)
        scratch = []
    return pl.pallas_call(
        functools.partial(_tail_kernel, final),
        grid=(n_tiles,),
        in_specs=(act_specs + [tok(D_MODEL), _mod_spec(layer, off, TAIL_CTX_TILES)]
                  + [_const_spec(w.shape) if w.ndim == 2 else _layer_spec(w.shape, layer) for w in weights]),
        out_specs=out_spec,
        out_shape=out_shape,
        scratch_shapes=scratch,
        compiler_params=pltpu.CompilerParams(
            dimension_semantics=("parallel",), vmem_limit_bytes=TAIL_VMEM_LIMIT),
        name="tail_final" if final else "tail",
    )(*acts, xt, modtab, *weights)


def _block_diag(w):
    n, di, dj = w.shape[-3:]
    out = jnp.einsum('...nij,nm->...nimj', w, jnp.eye(n, dtype=w.dtype))
    return out.reshape(w.shape[:-3] + (n * di, n * dj))


def _s5_params(lam_re, lam_im, log_dt, b_re, b_im, c_re, c_im):
    gpb = SSM_GROUPS // S5_BLOCKS
    depth = lam_re.shape[0]
    f32 = lambda t: t.astype(F32)

    in_blocks = lambda w: _block_diag(
        jnp.swapaxes(f32(w), -1, -2).reshape(depth, S5_BLOCKS, gpb, SSM_GROUP, SSM_STATE))
    out_blocks = lambda w: _block_diag(
        jnp.swapaxes(w, -1, -2).reshape(depth, 2, S5_BLOCKS, gpb, SSM_STATE, SSM_GROUP))

    bmat = jnp.concatenate([in_blocks(b_re), in_blocks(b_im)], axis=-1).astype(BF16)
    lr, li = f32(lam_re), f32(lam_im)
    dt = jnp.exp(f32(log_dt))[..., None]
    mag = jnp.exp(lr * dt)
    ang = li * dt
    bar_r, bar_i = mag * jnp.cos(ang), mag * jnp.sin(ang)
    den = lr * lr + li * li
    fr = ((bar_r - 1.0) * lr + bar_i * li) / den
    fi = (bar_i * lr - (bar_r - 1.0) * li) / den
    cr, ci = f32(c_re), f32(c_im)
    cfr = cr * fr[..., None, :] - ci * fi[..., None, :]
    cfi = cr * fi[..., None, :] + ci * fr[..., None, :]
    cm = jnp.concatenate([out_blocks(cfr), out_blocks(-cfi)], axis=-2)
    cmat = jnp.concatenate([cm[:, 0], cm[:, 1]], axis=-1).astype(BF16)
    lam = jnp.stack([bar_r.reshape(depth, 2, N_STATE), bar_i.reshape(depth, 2, N_STATE)], axis=1)
    lam = jnp.repeat(lam, HALF, axis=2)
    return bmat, cmat, lam


def kernel(x, c, ctx, c_ctx, w_mod, b_mod, norm1_g, norm2_g, w_in, w_out, lru_conv_w, lru_conv_b,
           lru_wa, lru_ba, lru_wi, lru_bi, lru_lambda, s5_lambda_re, s5_lambda_im, s5_log_dt,
           s5_b_re, s5_b_im, s5_c_re, s5_c_im, s5_d, s5_glu_w, s5_glu_b, pool_w, pool_b, pool_scale,
           ffn_w_gate, ffn_w_up, ffn_w_down, final_g):
    cc = jnp.zeros((SUBLANES, D_MODEL), F32).at[:BATCH].set(c).at[BATCH].set(c_ctx)
    mod = _modulation(cc, w_mod, b_mod)
    xt = (ctx, x)
    fg = final_g.reshape(1, D_MODEL)
    w_in_b, w_out_b, glu_b = w_in.astype(BF16), w_out.astype(BF16), s5_glu_w.astype(BF16)
    ffn_gate_b, ffn_up_b, ffn_down_b = (w.astype(BF16) for w in (ffn_w_gate, ffn_w_up, ffn_w_down))

    row = lambda t: t.reshape(t.shape[:-1] + (1, t.shape[-1]))
    lat = jnp.transpose(mod[:, :BATCH].reshape(DEPTH, BATCH, 6, D_MODEL), (0, 2, 1, 3))
    lat = jnp.concatenate([lat, lat], axis=2)
    cm = jnp.broadcast_to(mod[:, BATCH].reshape(DEPTH, 6, 1, D_MODEL), (DEPTH, 6, SUBLANES, D_MODEL))
    modtab = jnp.stack([cm, lat], axis=1)
    heads = lru_wa.shape[2]
    pairs = lambda w: _block_diag(w.reshape(DEPTH, 2, D_RNN // LANES, heads * LANES // D_RNN,
                                            D_RNN // heads, D_RNN // heads))
    wg = (-LOG2_E * jnp.concatenate([pairs(lru_wa), pairs(lru_wi)], axis=-1)).astype(BF16)
    bg = -LOG2_E * row(jnp.concatenate([lru_ba, lru_bi], axis=-1))
    spl = row(-LRU_C * LOG2_E * jax.nn.softplus(-lru_lambda.astype(F32)))
    scan_params = (lru_conv_w, row(lru_conv_b), wg, bg, spl) + _s5_params(
        s5_lambda_re, s5_lambda_im, s5_log_dt, s5_b_re, s5_b_im, s5_c_re, s5_c_im)
    weights = (row(s5_d), glu_b, row(s5_glu_b), _block_diag(pool_w).astype(BF16), row(pool_b),
               row(pool_scale), w_out_b, row(norm2_g), ffn_gate_b, ffn_up_b, ffn_down_b, fg)
    g1 = row(norm1_g)

    for l in range(DEPTH):
        last = l == DEPTH - 1
        xt, ax, u, gate, v, vc = _inproj(xt, modtab, g1, w_in_b, l)
        hf, ysf, hb, ysb = _scan(l, ax, u, *scan_params)
        m = _pool(v)
        mc = None if last else _pool_ctx(vc)
        xt = _tail(last, l, gate, hf, hb, u, ysf, ysb, m, mc, xt, modtab, weights)
    return xt
```
